```python
import functools
import jax, jax.numpy as jnp
from jax import lax
import numpy as np

D_MODEL = 2048
BATCH = 2
SEQ = 4096
DEPTH = 1
DEC_BATCH = 32
DEC_SEQ = 8
PAST_LEN = 8192
PAGE_SIZE = 128

N_HEADS = 8
HEAD_DIM = 128
ATTN_DIM = N_HEADS * HEAD_DIM
N_IDX_HEADS = 8
IDX_DIM = 64
TOPK_MAX = 256
CONV_DIM = D_MODEL // 2
CONV_WIDTH = 3
N_GROUPS = 4
EXPERTS_PER_GROUP = 8
TOPK_EXPERTS = 2
EXPERT_FF = 512
ROPE_THETA = 10000.0
EPS = 1e-6
Q_BLOCK = 128
IN_SPLITS = (ATTN_DIM, ATTN_DIM, ATTN_DIM, N_IDX_HEADS * IDX_DIM, IDX_DIM, N_IDX_HEADS,
             CONV_DIM, CONV_DIM, CONV_DIM, D_MODEL, D_MODEL)
IN_DIM = sum(IN_SPLITS)

kernel_name = "hybrid_dsa_shortconv_hmoe_step"


def rms_norm(x, w):
    xf = x.astype(jnp.float32)
    y = xf * lax.rsqrt(jnp.mean(xf * xf, axis=-1, keepdims=True) + EPS)
    return (y * w.astype(jnp.float32)).astype(x.dtype)


def rotary(x, pos):
    half = x.shape[-1] // 2
    inv = 1.0 / jnp.power(ROPE_THETA, jnp.arange(half, dtype=jnp.float32) / half)
    ang = pos.astype(jnp.float32)[:, None] * inv[None, :]
    cos = jnp.cos(ang)[None, :, None, :]
    sin = jnp.sin(ang)[None, :, None, :]
    x1 = x[..., :half].astype(jnp.float32)
    x2 = x[..., half:].astype(jnp.float32)
    return jnp.concatenate([x1 * cos - x2 * sin, x2 * cos + x1 * sin], axis=-1).astype(x.dtype)


def indexer_topk(qi, wi, ki_all, q_pos, topk):
    s = jax.nn.relu(jnp.einsum('bqhd,bld->bqhl', qi.astype(jnp.float32), ki_all.astype(jnp.float32)))
    score = jnp.einsum('bqhl,bqh->bql', s, wi.astype(jnp.float32))
    key_pos = jnp.arange(ki_all.shape[1])
    admissible = key_pos[None, :] <= q_pos[:, None]
    score = jnp.where(admissible[None], score, -jnp.inf)
    _, idx = lax.top_k(score, topk)
    valid = idx <= q_pos[None, :, None]
    return idx, valid


def sparse_softmax_attend(q, k_sel, v_sel, valid):
    logits = jnp.einsum('bqhd,bqkhd->bqhk', q.astype(jnp.float32), k_sel.astype(jnp.float32)) * (HEAD_DIM ** -0.5)
    logits = jnp.where(valid[:, :, None, :], logits, -jnp.inf)
    p = jax.nn.softmax(logits, axis=-1)
    return jnp.einsum('bqhk,bqkhd->bqhd', p, v_sel.astype(jnp.float32)).astype(q.dtype)


_row_gather = jax.vmap(lambda arr, ix: arr[ix])


def prompt_sparse_attention(q, k, v, qi, ki, wi):
    B, T = q.shape[:2]
    topk = min(TOPK_MAX, T // 4)
    nblk = T // Q_BLOCK

    def to_blocks(a):
        return jnp.moveaxis(a.reshape((B, nblk, Q_BLOCK) + a.shape[2:]), 1, 0)

    pos_blocks = jnp.arange(T).reshape(nblk, Q_BLOCK)

    def one_block(args):
        qb, qib, wib, pb = args
        idx, valid = indexer_topk(qib, wib, ki, pb, topk)
        return sparse_softmax_attend(qb, _row_gather(k, idx), _row_gather(v, idx), valid)

    out = lax.map(one_block, (to_blocks(q), to_blocks(qi), to_blocks(wi), pos_blocks))
    return jnp.moveaxis(out, 0, 1).reshape(B, T, N_HEADS, HEAD_DIM)


def sample_sparse_attention(q, k, v, qi, ki, wi, cache_k, cache_v, cache_kidx, page_table):
    Bd, S = q.shape[:2]
    n_pages = page_table.shape[1]
    past = n_pages * PAGE_SIZE
    topk = min(TOPK_MAX, (past + S) // 4)
    ki_past = cache_kidx[page_table].reshape(Bd, past, IDX_DIM)
    ki_all = jnp.concatenate([ki_past, ki.astype(ki_past.dtype)], axis=1)
    q_pos = past + jnp.arange(S)
    idx, valid = indexer_topk(qi, wi, ki_all, q_pos, topk)
    is_past = (idx < past)[..., None, None]
    pidx = jnp.minimum(idx, past - 1)
    phys = jnp.take_along_axis(page_table, (pidx // PAGE_SIZE).reshape(Bd, -1), axis=1).reshape(idx.shape)
    slot = pidx % PAGE_SIZE
    nidx = jnp.clip(idx - past, 0, S - 1)
    k_sel = jnp.where(is_past, cache_k[phys, slot], _row_gather(k, nidx).astype(cache_k.dtype))
    v_sel = jnp.where(is_past, cache_v[phys, slot], _row_gather(v, nidx).astype(cache_v.dtype))
    return sparse_softmax_attend(q, k_sel, v_sel, valid)


def short_conv(xc, bg, cg, conv_w, buf):
    u = cg * xc
    ext = jnp.concatenate([buf.astype(u.dtype), u], axis=1)
    T = u.shape[1]
    y = sum(conv_w[j] * ext[:, j:j + T] for j in range(CONV_WIDTH))
    return bg * y, ext[:, -(CONV_WIDTH - 1):]


def hier_moe(h, w_gr, b_gr, w_er, b_er, w_g, w_u, w_d):
    B, T, D = h.shape
    t = h.reshape(-1, D)
    n = t.shape[0]
    g_logits = (t @ w_gr).astype(jnp.float32)
    g_prob = jax.nn.softmax(g_logits, axis=-1)
    g_sel = jnp.argmax(g_logits + b_gr.astype(jnp.float32), axis=-1)
    g_w = jnp.take_along_axis(g_prob, g_sel[:, None], axis=1)[:, 0]
    e_logits = (t @ w_er).astype(jnp.float32).reshape(n, N_GROUPS, EXPERTS_PER_GROUP)
    e_in = jnp.take_along_axis(e_logits, g_sel[:, None, None], axis=1)[:, 0]
    _, e_idx = lax.top_k(e_in + b_er.astype(jnp.float32)[g_sel], TOPK_EXPERTS)
    e_p = jax.nn.softmax(jnp.take_along_axis(e_in, e_idx, axis=1), axis=-1)
    sel_e = jnp.sum(jax.nn.one_hot(e_idx, EXPERTS_PER_GROUP, dtype=jnp.float32) * e_p[..., None], axis=1)
    gate = jax.nn.one_hot(g_sel, N_GROUPS, dtype=jnp.float32)[:, :, None] * (g_w[:, None] * sel_e)[:, None, :]
    out = jnp.zeros((n, D), jnp.float32)
    for gi in range(N_GROUPS):
        hid = jax.nn.silu(jnp.einsum('nd,edf->nef', t, w_g[gi])) * jnp.einsum('nd,edf->nef', t, w_u[gi])
        hid = hid.astype(jnp.float32) * gate[:, gi, :, None]
        out = out + jnp.einsum('nef,efd->nd', hid, w_d[gi].astype(jnp.float32))
    return out.reshape(B, T, D).astype(h.dtype)


def block_forward(x, c, pos, attend, conv_buf, lw):
    B, T, _ = x.shape
    mod = (jax.nn.silu(c) @ lw['w_ada'] + lw['b_ada'])[:, None, :]
    sh1, sc1, g1, sh2, sc2, g2 = jnp.split(mod, 6, axis=-1)
    h = rms_norm(x, lw['norm1_w']) * (1 + sc1) + sh1
    split_points = np.cumsum(IN_SPLITS)[:-1].tolist()
    q, k, v, qi, ki, wi, xc, bg, cg, ga, gc = jnp.split(h @ lw['w_in'], split_points, axis=-1)
    q = rotary(rms_norm(q.reshape(B, T, N_HEADS, HEAD_DIM), lw['q_norm_w']), pos)
    k = rotary(rms_norm(k.reshape(B, T, N_HEADS, HEAD_DIM), lw['k_norm_w']), pos)
    v = v.reshape(B, T, N_HEADS, HEAD_DIM)
    qi = rotary(qi.reshape(B, T, N_IDX_HEADS, IDX_DIM), pos)
    ki = rotary(ki[:, :, None, :], pos)[:, :, 0]
    wi = wi * ((N_IDX_HEADS ** -0.5) * (IDX_DIM ** -0.5))
    attn = attend(q, k, v, qi, ki, wi)
    conv, new_buf = short_conv(xc, bg, cg, lw['conv_w'], conv_buf)
    a_br = attn.reshape(B, T, ATTN_DIM) @ lw['w_branch_attn']
    c_br = conv @ lw['w_branch_conv']
    merged = jax.nn.sigmoid(ga) * a_br + jax.nn.sigmoid(gc) * c_br
    x = x + g1 * (merged @ lw['w_out'])
    h2 = rms_norm(x, lw['norm2_w']) * (1 + sc2) + sh2
    x = x + g2 * hier_moe(h2, lw['w_group_router'], lw['b_group_router'], lw['w_expert_router'],
                          lw['b_expert_router'], lw['w_e_gate'], lw['w_e_up'], lw['w_e_down'])
    return x, k, v, ki, new_buf


def setup_inputs(seed: int = 0) -> dict:
    key = jax.random.key(seed)
    ks = iter(jax.random.split(key, 40))

    def nrm(shape, scale):
        return jax.random.normal(next(ks), shape, jnp.float32) * scale

    n_pages = PAST_LEN // PAGE_SIZE
    n_used = DEC_BATCH * n_pages
    n_pool = n_used + max(1, n_used // 4)
    perm = jax.random.permutation(next(ks), n_pool)
    page_table = perm[:n_used].reshape(DEC_BATCH, n_pages).astype(jnp.int32)
    G, E, F = N_GROUPS, EXPERTS_PER_GROUP, EXPERT_FF
    return {
        'x_prompt': nrm((BATCH, SEQ, D_MODEL), 1.0),
        'x_sample': nrm((DEC_BATCH, DEC_SEQ, D_MODEL), 1.0),
        'c_prompt': nrm((BATCH, D_MODEL), 1.0),
        'c_sample': nrm((DEC_BATCH, D_MODEL), 1.0),
        'cache_k': nrm((DEPTH, n_pool, PAGE_SIZE, N_HEADS, HEAD_DIM), 1.0),
        'cache_v': nrm((DEPTH, n_pool, PAGE_SIZE, N_HEADS, HEAD_DIM), 1.0),
        'cache_kidx': nrm((DEPTH, n_pool, PAGE_SIZE, IDX_DIM), 1.0),
        'state_conv': nrm((DEPTH, DEC_BATCH, CONV_WIDTH - 1, CONV_DIM), 1.0),
        'page_table': page_table,
        'norm1_w': 1.0 + nrm((DEPTH, D_MODEL), 0.02),
        'norm2_w': 1.0 + nrm((DEPTH, D_MODEL), 0.02),
        'w_ada': nrm((DEPTH, D_MODEL, 6 * D_MODEL), 0.5 * D_MODEL ** -0.5),
        'b_ada': nrm((DEPTH, 6 * D_MODEL), 0.01),
        'w_in': nrm((DEPTH, D_MODEL, IN_DIM), D_MODEL ** -0.5),
        'q_norm_w': 1.0 + nrm((DEPTH, HEAD_DIM), 0.02),
        'k_norm_w': 1.0 + nrm((DEPTH, HEAD_DIM), 0.02),
        'conv_w': nrm((DEPTH, CONV_WIDTH, CONV_DIM), CONV_WIDTH ** -0.5),
        'w_branch_attn': nrm((DEPTH, ATTN_DIM, D_MODEL), ATTN_DIM ** -0.5),
        'w_branch_conv': nrm((DEPTH, CONV_DIM, D_MODEL), CONV_DIM ** -0.5),
        'w_out': nrm((DEPTH, D_MODEL, D_MODEL), D_MODEL ** -0.5),
        'w_group_router': nrm((DEPTH, D_MODEL, G), D_MODEL ** -0.5),
        'b_group_router': nrm((DEPTH, G), 0.01),
        'w_expert_router': nrm((DEPTH, D_MODEL, G * E), D_MODEL ** -0.5),
        'b_expert_router': nrm((DEPTH, G, E), 0.01),
        'w_e_gate': nrm((DEPTH, G, E, D_MODEL, F), D_MODEL ** -0.5),
        'w_e_up': nrm((DEPTH, G, E, D_MODEL, F), D_MODEL ** -0.5),
        'w_e_down': nrm((DEPTH, G, E, F, D_MODEL), F ** -0.5),
    }


def reference(x_prompt, x_sample, c_prompt, c_sample, cache_k, cache_v, cache_kidx, state_conv, page_table,
              norm1_w, norm2_w, w_ada, b_ada, w_in, q_norm_w, k_norm_w, conv_w, w_branch_attn, w_branch_conv,
              w_out, w_group_router, b_group_router, w_expert_router, b_expert_router, w_e_gate, w_e_up, w_e_down):
    B, T, _ = x_prompt.shape
    Bd, S, _ = x_sample.shape
    past = page_table.shape[1] * PAGE_SIZE
    pos_prompt = jnp.arange(T)
    pos_sample = past + jnp.arange(S)
    yp, ys = x_prompt, x_sample
    kp_l, vp_l, kip_l, cp_l, ks_l, vs_l, kis_l, cs_l = [], [], [], [], [], [], [], []
    for l in range(DEPTH):
        lw = {
            'norm1_w': norm1_w[l], 'norm2_w': norm2_w[l], 'w_ada': w_ada[l], 'b_ada': b_ada[l],
            'w_in': w_in[l], 'q_norm_w': q_norm_w[l], 'k_norm_w': k_norm_w[l], 'conv_w': conv_w[l],
            'w_branch_attn': w_branch_attn[l], 'w_branch_conv': w_branch_conv[l], 'w_out': w_out[l],
            'w_group_router': w_group_router[l], 'b_group_router': b_group_router[l],
            'w_expert_router': w_expert_router[l], 'b_expert_router': b_expert_router[l],
            'w_e_gate': w_e_gate[l], 'w_e_up': w_e_up[l], 'w_e_down': w_e_down[l],
        }
        zero_buf = jnp.zeros((B, CONV_WIDTH - 1, CONV_DIM), x_prompt.dtype)
        yp, kp, vp, kip, cp = block_forward(yp, c_prompt, pos_prompt, prompt_sparse_attention, zero_buf, lw)
        samp_attend = functools.partial(sample_sparse_attention, cache_k=cache_k[l], cache_v=cache_v[l],
                                        cache_kidx=cache_kidx[l], page_table=page_table)
        ys, ksm, vsm, kis, cs = block_forward(ys, c_sample, pos_sample, samp_attend, state_conv[l], lw)
        kp_l.append(kp); vp_l.append(vp); kip_l.append(kip); cp_l.append(cp)
        ks_l.append(ksm); vs_l.append(vsm); kis_l.append(kis); cs_l.append(cs)
    return (yp, ys,
            jnp.stack(kp_l), jnp.stack(vp_l), jnp.stack(kip_l), jnp.stack(cp_l),
            jnp.stack(ks_l), jnp.stack(vs_l), jnp.stack(kis_l), jnp.stack(cs_l))
```

```python
import functools

import jax
import jax.numpy as jnp
import numpy as np
from jax import lax
from jax.experimental import pallas as pl
from jax.experimental.pallas import tpu as pltpu

D_MODEL = 2048
PAGE_SIZE = 128
N_HEADS = 8
HEAD_DIM = 128
ATTN_DIM = N_HEADS * HEAD_DIM
N_IDX_HEADS = 8
IDX_DIM = 64
TOPK_MAX = 256
CONV_DIM = D_MODEL // 2
CONV_WIDTH = 3
N_GROUPS = 4
EXPERTS_PER_GROUP = 8
TOPK_EXPERTS = 2
EXPERT_FF = 512
ROPE_THETA = 10000.0
EPS = 1e-6
Q_BLOCK = 128
IN_SPLITS = (ATTN_DIM, ATTN_DIM, ATTN_DIM, N_IDX_HEADS * IDX_DIM, IDX_DIM, N_IDX_HEADS,
             CONV_DIM, CONV_DIM, CONV_DIM, D_MODEL, D_MODEL)

VMEM_LIMIT_BYTES = 56 * 1024 * 1024


def _mm_body(a_ref, b_ref, o_ref):
    a = a_ref[...].astype(jnp.bfloat16)
    b = b_ref[...].astype(jnp.bfloat16)
    o_ref[...] = jnp.dot(a, b, preferred_element_type=jnp.float32).astype(o_ref.dtype)


def _mm(a, b, *, tm=512, tn=512, out_dtype=jnp.float32):
    M, K = a.shape
    _, N = b.shape
    tm = min(tm, M)
    tn = min(tn, N)
    return pl.pallas_call(
        _mm_body,
        grid=(pl.cdiv(M, tm), pl.cdiv(N, tn)),
        in_specs=[pl.BlockSpec((tm, K), lambda i, j: (i, 0)),
                  pl.BlockSpec((K, tn), lambda i, j: (0, j))],
        out_specs=pl.BlockSpec((tm, tn), lambda i, j: (i, j)),
        out_shape=jax.ShapeDtypeStruct((M, N), out_dtype),
        compiler_params=pltpu.CompilerParams(
            dimension_semantics=("arbitrary", "arbitrary"),
            vmem_limit_bytes=VMEM_LIMIT_BYTES),
        name="mm",
    )(a, b)


def _bmm_up_body(x_ref, w_ref, o_ref):
    o_ref[...] = jnp.dot(x_ref[...], w_ref[...].astype(jnp.bfloat16),
                         preferred_element_type=jnp.float32)


def _bmm_up(x, w, *, tm=1024):
    N, D = x.shape
    E, _, F = w.shape
    tm = min(tm, N)
    return pl.pallas_call(
        _bmm_up_body,
        grid=(E, pl.cdiv(N, tm)),
        in_specs=[pl.BlockSpec((tm, D), lambda e, i: (i, 0)),
                  pl.BlockSpec((None, D, F), lambda e, i: (e, 0, 0))],
        out_specs=pl.BlockSpec((None, tm, F), lambda e, i: (e, i, 0)),
        out_shape=jax.ShapeDtypeStruct((E, N, F), jnp.float32),
        compiler_params=pltpu.CompilerParams(
            dimension_semantics=("arbitrary", "arbitrary"),
            vmem_limit_bytes=VMEM_LIMIT_BYTES),
        name="bmm_up",
    )(x, w)


def _bmm_down_body(h_ref, w_ref, o_ref):
    @pl.when(pl.program_id(1) == 0)
    def _():
        o_ref[...] = jnp.zeros_like(o_ref)

    o_ref[...] += jnp.dot(h_ref[...].astype(jnp.bfloat16), w_ref[...].astype(jnp.bfloat16),
                          preferred_element_type=jnp.float32)


def _bmm_down(hid, w, *, tm=1024):
    E, N, F = hid.shape
    D = w.shape[2]
    tm = min(tm, N)
    return pl.pallas_call(
        _bmm_down_body,
        grid=(pl.cdiv(N, tm), E),
        in_specs=[pl.BlockSpec((None, tm, F), lambda i, e: (e, i, 0)),
                  pl.BlockSpec((None, F, D), lambda i, e: (e, 0, 0))],
        out_specs=pl.BlockSpec((tm, D), lambda i, e: (i, 0)),
        out_shape=jax.ShapeDtypeStruct((N, D), jnp.float32),
        compiler_params=pltpu.CompilerParams(
            dimension_semantics=("arbitrary", "arbitrary"),
            vmem_limit_bytes=VMEM_LIMIT_BYTES),
        name="bmm_down",
    )(hid, w)


def _rms_norm(x, w):
    xf = x.astype(jnp.float32)
    y = xf * lax.rsqrt(jnp.mean(xf * xf, axis=-1, keepdims=True) + EPS)
    return y * w.astype(jnp.float32)


def _rotary(x, pos):
    half = x.shape[-1] // 2
    inv = 1.0 / jnp.power(ROPE_THETA, jnp.arange(half, dtype=jnp.float32) / half)
    ang = pos.astype(jnp.float32)[:, None] * inv[None, :]
    cos = jnp.cos(ang)[None, :, None, :]
    sin = jnp.sin(ang)[None, :, None, :]
    x1 = x[..., :half]
    x2 = x[..., half:]
    return jnp.concatenate([x1 * cos - x2 * sin, x2 * cos + x1 * sin], axis=-1)


def _indexer_topk(qi, wi, ki_all, q_pos, topk):
    s = jax.nn.relu(jnp.einsum('bqhd,bld->bqhl', qi, ki_all))
    score = jnp.einsum('bqhl,bqh->bql', s, wi)
    key_pos = jnp.arange(ki_all.shape[1])
    admissible = key_pos[None, :] <= q_pos[:, None]
    score = jnp.where(admissible[None], score, -jnp.inf)
    _, idx = lax.top_k(score, topk)
    valid = idx <= q_pos[None, :, None]
    return idx, valid


def _sparse_softmax_attend(q, k_sel, v_sel, valid):
    logits = jnp.einsum('bqhd,bqkhd->bqhk', q, k_sel) * (HEAD_DIM ** -0.5)
    logits = jnp.where(valid[:, :, None, :], logits, -jnp.inf)
    p = jax.nn.softmax(logits, axis=-1)
    return jnp.einsum('bqhk,bqkhd->bqhd', p, v_sel)


_row_gather = jax.vmap(lambda arr, ix: arr[ix])


def _prompt_sparse_attention(q, k, v, qi, ki, wi):
    B, T = q.shape[:2]
    topk = min(TOPK_MAX, T // 4)
    nblk = T // Q_BLOCK

    def to_blocks(a):
        return jnp.moveaxis(a.reshape((B, nblk, Q_BLOCK) + a.shape[2:]), 1, 0)

    pos_blocks = jnp.arange(T).reshape(nblk, Q_BLOCK)

    def one_block(args):
        qb, qib, wib, pb = args
        idx, valid = _indexer_topk(qib, wib, ki, pb, topk)
        return _sparse_softmax_attend(qb, _row_gather(k, idx), _row_gather(v, idx), valid)

    out = lax.map(one_block, (to_blocks(q), to_blocks(qi), to_blocks(wi), pos_blocks))
    return jnp.moveaxis(out, 0, 1).reshape(B, T, N_HEADS, HEAD_DIM)


def _sample_sparse_attention(q, k, v, qi, ki, wi, cache_k, cache_v, cache_kidx, page_table):
    Bd, S = q.shape[:2]
    n_pages = page_table.shape[1]
    past = n_pages * PAGE_SIZE
    topk = min(TOPK_MAX, (past + S) // 4)
    ki_past = cache_kidx[page_table].reshape(Bd, past, IDX_DIM)
    ki_all = jnp.concatenate([ki_past, ki], axis=1)
    q_pos = past + jnp.arange(S)
    idx, valid = _indexer_topk(qi, wi, ki_all, q_pos, topk)
    is_past = (idx < past)[..., None, None]
    pidx = jnp.minimum(idx, past - 1)
    phys = jnp.take_along_axis(page_table, (pidx // PAGE_SIZE).reshape(Bd, -1), axis=1).reshape(idx.shape)
    slot = pidx % PAGE_SIZE
    nidx = jnp.clip(idx - past, 0, S - 1)
    k_sel = jnp.where(is_past, cache_k[phys, slot], _row_gather(k, nidx))
    v_sel = jnp.where(is_past, cache_v[phys, slot], _row_gather(v, nidx))
    return _sparse_softmax_attend(q, k_sel, v_sel, valid)


def _short_conv(xc, bg, cg, conv_w, buf):
    u = cg * xc
    ext = jnp.concatenate([buf, u], axis=1)
    T = u.shape[1]
    y = sum(conv_w[j] * ext[:, j:j + T] for j in range(CONV_WIDTH))
    return bg * y, ext[:, -(CONV_WIDTH - 1):]


def _hier_moe(h, w_router, b_gr, b_er, w_g, w_u, w_d):
    B, T, D = h.shape
    t = h.reshape(-1, D)
    n = t.shape[0]
    r_logits = _mm(t, w_router, tn=128)
    g_logits = r_logits[:, :N_GROUPS]
    g_prob = jax.nn.softmax(g_logits, axis=-1)
    g_sel = jnp.argmax(g_logits + b_gr, axis=-1)
    g_w = jnp.take_along_axis(g_prob, g_sel[:, None], axis=1)[:, 0]
    e_logits = r_logits[:, N_GROUPS:N_GROUPS * (1 + EXPERTS_PER_GROUP)].reshape(n, N_GROUPS, EXPERTS_PER_GROUP)
    e_in = jnp.take_along_axis(e_logits, g_sel[:, None, None], axis=1)[:, 0]
    _, e_idx = lax.top_k(e_in + b_er[g_sel], TOPK_EXPERTS)
    e_p = jax.nn.softmax(jnp.take_along_axis(e_in, e_idx, axis=1), axis=-1)
    sel_e = jnp.sum(jax.nn.one_hot(e_idx, EXPERTS_PER_GROUP, dtype=jnp.float32) * e_p[..., None], axis=1)
    gate = jax.nn.one_hot(g_sel, N_GROUPS, dtype=jnp.float32)[:, :, None] * (g_w[:, None] * sel_e)[:, None, :]
    gate = gate.reshape(n, N_GROUPS * EXPERTS_PER_GROUP)
    tb = t.astype(jnp.bfloat16)
    F = w_g.shape[-1]
    hg = _bmm_up(tb, w_g.reshape(-1, D, F))
    hu = _bmm_up(tb, w_u.reshape(-1, D, F))
    hid = jax.nn.silu(hg) * hu * gate.T[:, :, None]
    out = _bmm_down(hid, w_d.reshape(-1, F, D))
    return out.reshape(B, T, D)


def _block_forward(x, mod, pos, attend, conv_buf, lw):
    B, T, _ = x.shape
    sh1, sc1, g1, sh2, sc2, g2 = jnp.split(mod[:, None, :], 6, axis=-1)
    h = _rms_norm(x, lw['norm1_w']) * (1 + sc1) + sh1
    split_points = np.cumsum(IN_SPLITS)[:-1].tolist()
    proj = _mm(h.reshape(B * T, D_MODEL), lw['w_in']).reshape(B, T, -1)
    q, k, v, qi, ki, wi, xc, bg, cg, ga, gc = jnp.split(proj, split_points, axis=-1)
    q = _rotary(_rms_norm(q.reshape(B, T, N_HEADS, HEAD_DIM), lw['q_norm_w']), pos)
    k = _rotary(_rms_norm(k.reshape(B, T, N_HEADS, HEAD_DIM), lw['k_norm_w']), pos)
    v = v.reshape(B, T, N_HEADS, HEAD_DIM)
    qi = _rotary(qi.reshape(B, T, N_IDX_HEADS, IDX_DIM), pos)
    ki = _rotary(ki[:, :, None, :], pos)[:, :, 0]
    wi = wi * ((N_IDX_HEADS ** -0.5) * (IDX_DIM ** -0.5))
    attn = attend(q, k, v, qi, ki, wi)
    conv, new_buf = _short_conv(xc, bg, cg, lw['conv_w'], conv_buf)
    a_br = _mm(attn.reshape(B * T, ATTN_DIM), lw['w_branch_attn']).reshape(B, T, D_MODEL)
    c_br = _mm(conv.reshape(B * T, CONV_DIM), lw['w_branch_conv']).reshape(B, T, D_MODEL)
    merged = jax.nn.sigmoid(ga) * a_br + jax.nn.sigmoid(gc) * c_br
    x = x + g1 * _mm(merged.reshape(B * T, D_MODEL), lw['w_out']).reshape(B, T, D_MODEL)
    h2 = _rms_norm(x, lw['norm2_w']) * (1 + sc2) + sh2
    x = x + g2 * _hier_moe(h2, lw['w_router'], lw['b_group_router'], lw['b_expert_router'],
                           lw['w_e_gate'], lw['w_e_up'], lw['w_e_down'])
    return x, k, v, ki, new_buf


def kernel(x_prompt, x_sample, c_prompt, c_sample, cache_k, cache_v, cache_kidx, state_conv, page_table, norm1_w, norm2_w, w_ada, b_ada, w_in, q_norm_w, k_norm_w, conv_w, w_branch_attn, w_branch_conv, w_out, w_group_router, b_group_router, w_expert_router, b_expert_router, w_e_gate, w_e_up, w_e_down):
    B, T, _ = x_prompt.shape
    Bd, S, _ = x_sample.shape
    depth = norm1_w.shape[0]
    past = page_table.shape[1] * PAGE_SIZE
    pos_prompt = jnp.arange(T)
    pos_sample = past + jnp.arange(S)
    yp, ys = x_prompt, x_sample
    outs = [[] for _ in range(8)]
    c_all = jnp.concatenate([c_prompt, c_sample], axis=0)
    n_c = c_all.shape[0]
    c_pad = jnp.pad(jax.nn.silu(c_all), ((0, (-n_c) % 8), (0, 0)))
    for l in range(depth):
        w_router = jnp.pad(jnp.concatenate([w_group_router[l], w_expert_router[l].reshape(D_MODEL, -1)], axis=1),
                           ((0, 0), (0, 128 - N_GROUPS * (1 + EXPERTS_PER_GROUP))))
        lw = {
            'norm1_w': norm1_w[l], 'norm2_w': norm2_w[l],
            'w_in': w_in[l], 'q_norm_w': q_norm_w[l], 'k_norm_w': k_norm_w[l], 'conv_w': conv_w[l],
            'w_branch_attn': w_branch_attn[l], 'w_branch_conv': w_branch_conv[l], 'w_out': w_out[l],
            'w_router': w_router, 'b_group_router': b_group_router[l],
            'b_expert_router': b_expert_router[l],
            'w_e_gate': w_e_gate[l], 'w_e_up': w_e_up[l], 'w_e_down': w_e_down[l],
        }
        mod = _mm(c_pad, w_ada[l], tn=1024)[:n_c] + b_ada[l]
        zero_buf = jnp.zeros((B, CONV_WIDTH - 1, CONV_DIM), x_prompt.dtype)
        yp, kp, vp, kip, cp = _block_forward(yp, mod[:B], pos_prompt, _prompt_sparse_attention, zero_buf, lw)
        samp_attend = functools.partial(_sample_sparse_attention, cache_k=cache_k[l], cache_v=cache_v[l],
                                        cache_kidx=cache_kidx[l], page_table=page_table)
        ys, ksm, vsm, kis, cs = _block_forward(ys, mod[B:], pos_sample, samp_attend, state_conv[l], lw)
        for lst, val in zip(outs, (kp, vp, kip, cp, ksm, vsm, kis, cs)):
            lst.append(val)
    return (yp, ys) + tuple(jnp.stack(o) for o in outs)
```

```python
import functools

import jax
import jax.numpy as jnp
import numpy as np
from jax import lax
from jax.experimental import pallas as pl
from jax.experimental.pallas import tpu as pltpu

D_MODEL = 2048
PAGE_SIZE = 128
N_HEADS = 8
HEAD_DIM = 128
ATTN_DIM = N_HEADS * HEAD_DIM
N_IDX_HEADS = 8
IDX_DIM = 64
TOPK_MAX = 256
CONV_DIM = D_MODEL // 2
CONV_WIDTH = 3
N_GROUPS = 4
EXPERTS_PER_GROUP = 8
TOPK_EXPERTS = 2
EXPERT_FF = 512
ROPE_THETA = 10000.0
EPS = 1e-6
Q_BLOCK = 128
IN_SPLITS = (ATTN_DIM, ATTN_DIM, ATTN_DIM, N_IDX_HEADS * IDX_DIM, IDX_DIM, N_IDX_HEADS,
             CONV_DIM, CONV_DIM, CONV_DIM, D_MODEL, D_MODEL)

VMEM_LIMIT_BYTES = 56 * 1024 * 1024


def _mm_body(a_ref, b_ref, o_ref):
    a = a_ref[...].astype(jnp.bfloat16)
    b = b_ref[...].astype(jnp.bfloat16)
    o_ref[...] = jnp.dot(a, b, preferred_element_type=jnp.float32).astype(o_ref.dtype)


def _mm(a, b, *, tm=512, tn=512, out_dtype=jnp.float32):
    M, K = a.shape
    _, N = b.shape
    tm = min(tm, M)
    tn = min(tn, N)
    return pl.pallas_call(
        _mm_body,
        grid=(pl.cdiv(M, tm), pl.cdiv(N, tn)),
        in_specs=[pl.BlockSpec((tm, K), lambda i, j: (i, 0)),
                  pl.BlockSpec((K, tn), lambda i, j: (0, j))],
        out_specs=pl.BlockSpec((tm, tn), lambda i, j: (i, j)),
        out_shape=jax.ShapeDtypeStruct((M, N), out_dtype),
        compiler_params=pltpu.CompilerParams(
            dimension_semantics=("arbitrary", "arbitrary"),
            vmem_limit_bytes=VMEM_LIMIT_BYTES),
        name="mm",
    )(a, b)


def _hi_lo(x):
    hi = x.astype(jnp.bfloat16)
    lo = (x - hi.astype(jnp.float32)).astype(jnp.bfloat16)
    return hi, lo


def _mm_split_body(a_ref, b_ref, o_ref):
    a_hi, a_lo = _hi_lo(a_ref[...])
    b_hi, b_lo = _hi_lo(b_ref[...])
    dot = functools.partial(jnp.dot, preferred_element_type=jnp.float32)
    o_ref[...] = dot(a_hi, b_hi) + (dot(a_lo, b_hi) + dot(a_hi, b_lo))


def _mm_split(a, b, *, tm=512):
    M, K = a.shape
    _, N = b.shape
    tm = min(tm, M)
    return pl.pallas_call(
        _mm_split_body,
        grid=(pl.cdiv(M, tm),),
        in_specs=[pl.BlockSpec((tm, K), lambda i: (i, 0)),
                  pl.BlockSpec((K, N), lambda i: (0, 0))],
        out_specs=pl.BlockSpec((tm, N), lambda i: (i, 0)),
        out_shape=jax.ShapeDtypeStruct((M, N), jnp.float32),
        compiler_params=pltpu.CompilerParams(
            dimension_semantics=("arbitrary",),
            vmem_limit_bytes=VMEM_LIMIT_BYTES),
        name="mm_split",
    )(a, b)


def _bmm_up_body(x_ref, w_ref, o_ref):
    o_ref[...] = jnp.dot(x_ref[...], w_ref[...].astype(jnp.bfloat16),
                         preferred_element_type=jnp.float32)


def _bmm_up(x, w, *, tm=1024):
    N, D = x.shape
    E, _, F = w.shape
    tm = min(tm, N)
    return pl.pallas_call(
        _bmm_up_body,
        grid=(E, pl.cdiv(N, tm)),
        in_specs=[pl.BlockSpec((tm, D), lambda e, i: (i, 0)),
                  pl.BlockSpec((None, D, F), lambda e, i: (e, 0, 0))],
        out_specs=pl.BlockSpec((None, tm, F), lambda e, i: (e, i, 0)),
        out_shape=jax.ShapeDtypeStruct((E, N, F), jnp.float32),
        compiler_params=pltpu.CompilerParams(
            dimension_semantics=("arbitrary", "arbitrary"),
            vmem_limit_bytes=VMEM_LIMIT_BYTES),
        name="bmm_up",
    )(x, w)


def _bmm_down_body(h_ref, w_ref, o_ref):
    @pl.when(pl.program_id(1) == 0)
    def _():
        o_ref[...] = jnp.zeros_like(o_ref)

    o_ref[...] += jnp.dot(h_ref[...].astype(jnp.bfloat16), w_ref[...].astype(jnp.bfloat16),
                          preferred_element_type=jnp.float32)


def _bmm_down(hid, w, *, tm=1024):
    E, N, F = hid.shape
    D = w.shape[2]
    tm = min(tm, N)
    return pl.pallas_call(
        _bmm_down_body,
        grid=(pl.cdiv(N, tm), E),
        in_specs=[pl.BlockSpec((None, tm, F), lambda i, e: (e, i, 0)),
                  pl.BlockSpec((None, F, D), lambda i, e: (e, 0, 0))],
        out_specs=pl.BlockSpec((tm, D), lambda i, e: (i, 0)),
        out_shape=jax.ShapeDtypeStruct((N, D), jnp.float32),
        compiler_params=pltpu.CompilerParams(
            dimension_semantics=("arbitrary", "arbitrary"),
            vmem_limit_bytes=VMEM_LIMIT_BYTES),
        name="bmm_down",
    )(hid, w)


ATTN_TQ = 256
ATTN_SEG = 1024
ATTN_SUB = 256
MASK_NEG = -1e30
KEY_NEG_INF = -2139095041
INT32_MIN = -2 ** 31


def _slab_sum(x):
    parts = [x[r * 8:(r + 1) * 8] for r in range(x.shape[0] // 8)]
    while len(parts) > 1:
        parts = [parts[a] + parts[a + 1] for a in range(0, len(parts), 2)]
    return parts[0]


def _prompt_attn_body(qi_ref, ki_ref, wit_ref, q_ref, k_ref, vt_ref, o_ref, key_ref, bias_ref, *, topk, idx_bits):
    tq = ATTN_TQ
    i = pl.program_id(1)
    q0 = i * tq
    nseg = (q0 + tq + ATTN_SEG - 1) // ATTN_SEG
    ntile = nseg * (ATTN_SEG // ATTN_SUB)
    qpos = q0 + lax.broadcasted_iota(jnp.int32, (1, tq), 1)
    nt_dims = (((1,), (1,)), ((), ()))

    def score_tile(t, c):
        r0 = pl.multiple_of(t * ATTN_SUB, ATTN_SUB)
        ki = ki_ref[pl.ds(r0, ATTN_SUB), :]
        acc = jnp.zeros((ATTN_SUB, tq), jnp.float32)
        for h in range(N_IDX_HEADS):
            s = lax.dot_general(ki, qi_ref[h], nt_dims, preferred_element_type=jnp.float32)
            acc = acc + jnp.maximum(s, 0.0) * wit_ref[h:h + 1, :]
        kpos = r0 + lax.broadcasted_iota(jnp.int32, (ATTN_SUB, 1), 0)
        bits = lax.bitcast_convert_type(acc, jnp.int32)
        skey = jnp.where(bits < 0, bits ^ jnp.int32(0x7FFFFFFF), bits)
        skey = jnp.where(acc == 0.0, 0, skey)
        key_ref[pl.ds(r0, ATTN_SUB), :] = jnp.where(kpos <= qpos, skey, KEY_NEG_INF)
        return c

    lax.fori_loop(0, ntile, score_tile, 0)

    def count(pred):
        def tile(t, cnt):
            r0 = pl.multiple_of(t * ATTN_SUB, ATTN_SUB)
            kk = key_ref[pl.ds(r0, ATTN_SUB), :]
            kpos = r0 + lax.broadcasted_iota(jnp.int32, (ATTN_SUB, 1), 0)
            return cnt + _slab_sum(pred(kk, kpos).astype(jnp.int32))
        cnt = lax.fori_loop(0, ntile, tile, jnp.zeros((8, tq), jnp.int32))
        return jnp.sum(cnt, axis=0, keepdims=True)

    def bit_pass(p, t_u):
        cand_u = t_u | lax.shift_left(jnp.int32(1), 31 - p)
        cand_s = cand_u ^ jnp.int32(INT32_MIN)
        c = count(lambda kk, kpos: kk >= cand_s)
        return jnp.where(c >= topk, cand_u, t_u)

    t_u = lax.fori_loop(0, 32, bit_pass, jnp.zeros((1, tq), jnp.int32))
    t_s = t_u ^ jnp.int32(INT32_MIN)

    need = topk - count(lambda kk, kpos: kk > t_s)

    def idx_pass(p, lo):
        cand = lo | lax.shift_left(jnp.int32(1), idx_bits - 1 - p)
        c = count(lambda kk, kpos: (kk == t_s) & (kpos < cand))
        return jnp.where(c < need, cand, lo)

    last_tie = lax.fori_loop(0, idx_bits, idx_pass, jnp.zeros((1, tq), jnp.int32))

    def bias_tile(t, c):
        r0 = pl.multiple_of(t * ATTN_SUB, ATTN_SUB)
        kk = key_ref[pl.ds(r0, ATTN_SUB), :]
        kpos = r0 + lax.broadcasted_iota(jnp.int32, (ATTN_SUB, 1), 0)
        sel = (kk > t_s) | ((kk == t_s) & (kpos <= last_tie))
        sel = sel & (kk > KEY_NEG_INF)
        bias_ref[pl.ds(r0, ATTN_SUB), :] = jnp.where(sel, 0.0, MASK_NEG)
        return c

    lax.fori_loop(0, ntile, bias_tile, 0)

    scale = HEAD_DIM ** -0.5

    def head_body(h, c):
        qh = q_ref[h]

        def seg_body(s, carry):
            m, l, acc = carry
            r0 = pl.multiple_of(s * ATTN_SEG, ATTN_SEG)
            kh = k_ref[h, pl.ds(r0, ATTN_SEG), :]
            logits = lax.dot_general(kh, qh, nt_dims, preferred_element_type=jnp.float32) * scale
            logits = logits + bias_ref[pl.ds(r0, ATTN_SEG), :]
            m_new = jnp.maximum(m, jnp.max(logits, axis=0, keepdims=True))
            alpha = jnp.exp(m - m_new)
            p = jnp.exp(logits - m_new)
            l = alpha * l + jnp.sum(p, axis=0, keepdims=True)
            acc = alpha * acc + jnp.dot(vt_ref[h, s], p.astype(jnp.bfloat16),
                                        preferred_element_type=jnp.float32)
            return m_new, l, acc

        init = (jnp.full((1, tq), MASK_NEG, jnp.float32), jnp.zeros((1, tq), jnp.float32),
                jnp.zeros((HEAD_DIM, tq), jnp.float32))
        _, l, acc = lax.fori_loop(0, nseg, seg_body, init)
        o_ref[h] = (acc / l).astype(o_ref.dtype)
        return c

    lax.fori_loop(0, N_HEADS, head_body, 0)


def _prompt_attention(q, k, v, qi, ki, wi):
    B, T = q.shape[:2]
    topk = min(TOPK_MAX, T // 4)
    nseg = T // ATTN_SEG
    bf = jnp.bfloat16
    q_t = q.astype(bf).transpose(0, 2, 1, 3)
    k_t = k.astype(bf).transpose(0, 2, 1, 3)
    vt = v.astype(bf).reshape(B, nseg, ATTN_SEG, N_HEADS, HEAD_DIM).transpose(0, 3, 1, 4, 2)
    qi_t = qi.astype(bf).transpose(0, 2, 1, 3)
    ki_b = ki.astype(bf)
    wit = wi.transpose(0, 2, 1)
    body = functools.partial(_prompt_attn_body, topk=topk, idx_bits=int(np.ceil(np.log2(T))))
    out = pl.pallas_call(
        body,
        grid=(B, T // ATTN_TQ),
        in_specs=[
            pl.BlockSpec((None, N_IDX_HEADS, ATTN_TQ, IDX_DIM), lambda b, i: (b, 0, i, 0)),
            pl.BlockSpec((None, T, IDX_DIM), lambda b, i: (b, 0, 0)),
            pl.BlockSpec((None, N_IDX_HEADS, ATTN_TQ), lambda b, i: (b, 0, i)),
            pl.BlockSpec((None, N_HEADS, ATTN_TQ, HEAD_DIM), lambda b, i: (b, 0, i, 0)),
            pl.BlockSpec((None, N_HEADS, T, HEAD_DIM), lambda b, i: (b, 0, 0, 0)),
            pl.BlockSpec((None, N_HEADS, nseg, HEAD_DIM, ATTN_SEG), lambda b, i: (b, 0, 0, 0, 0)),
        ],
        out_specs=pl.BlockSpec((None, N_HEADS, HEAD_DIM, ATTN_TQ), lambda b, i: (b, 0, 0, i)),
        out_shape=jax.ShapeDtypeStruct((B, N_HEADS, HEAD_DIM, T), bf),
        scratch_shapes=[pltpu.VMEM((T, ATTN_TQ), jnp.int32), pltpu.VMEM((T, ATTN_TQ), jnp.float32)],
        compiler_params=pltpu.CompilerParams(
            dimension_semantics=("arbitrary", "arbitrary"),
            vmem_limit_bytes=VMEM_LIMIT_BYTES),
        name="prompt_attn",
    )(qi_t, ki_b, wit, q_t, k_t, vt)
    return out.transpose(0, 3, 1, 2)


def _rms_norm(x, w):
    xf = x.astype(jnp.float32)
    y = xf * lax.rsqrt(jnp.mean(xf * xf, axis=-1, keepdims=True) + EPS)
    return y * w.astype(jnp.float32)


def _rotary(x, pos):
    half = x.shape[-1] // 2
    inv = 1.0 / jnp.power(ROPE_THETA, jnp.arange(half, dtype=jnp.float32) / half)
    ang = pos.astype(jnp.float32)[:, None] * inv[None, :]
    cos = jnp.cos(ang)[None, :, None, :]
    sin = jnp.sin(ang)[None, :, None, :]
    x1 = x[..., :half]
    x2 = x[..., half:]
    return jnp.concatenate([x1 * cos - x2 * sin, x2 * cos + x1 * sin], axis=-1)


def _indexer_topk(qi, wi, ki_all, q_pos, topk):
    s = jax.nn.relu(jnp.einsum('bqhd,bld->bqhl', qi, ki_all))
    score = jnp.einsum('bqhl,bqh->bql', s, wi)
    key_pos = jnp.arange(ki_all.shape[1])
    admissible = key_pos[None, :] <= q_pos[:, None]
    score = jnp.where(admissible[None], score, -jnp.inf)
    _, idx = lax.top_k(score, topk)
    valid = idx <= q_pos[None, :, None]
    return idx, valid


def _sparse_softmax_attend(q, k_sel, v_sel, valid):
    logits = jnp.einsum('bqhd,bqkhd->bqhk', q, k_sel) * (HEAD_DIM ** -0.5)
    logits = jnp.where(valid[:, :, None, :], logits, -jnp.inf)
    p = jax.nn.softmax(logits, axis=-1)
    return jnp.einsum('bqhk,bqkhd->bqhd', p, v_sel)


_row_gather = jax.vmap(lambda arr, ix: arr[ix])


def _sample_sparse_attention(q, k, v, qi, ki, wi, cache_k, cache_v, cache_kidx, page_table):
    Bd, S = q.shape[:2]
    n_pages = page_table.shape[1]
    past = n_pages * PAGE_SIZE
    topk = min(TOPK_MAX, (past + S) // 4)
    ki_past = cache_kidx[page_table].reshape(Bd, past, IDX_DIM)
    ki_all = jnp.concatenate([ki_past, ki], axis=1)
    q_pos = past + jnp.arange(S)
    idx, valid = _indexer_topk(qi, wi, ki_all, q_pos, topk)
    is_past = (idx < past)[..., None, None]
    pidx = jnp.minimum(idx, past - 1)
    phys = jnp.take_along_axis(page_table, (pidx // PAGE_SIZE).reshape(Bd, -1), axis=1).reshape(idx.shape)
    slot = pidx % PAGE_SIZE
    nidx = jnp.clip(idx - past, 0, S - 1)
    k_sel = jnp.where(is_past, cache_k[phys, slot], _row_gather(k, nidx))
    v_sel = jnp.where(is_past, cache_v[phys, slot], _row_gather(v, nidx))
    return _sparse_softmax_attend(q, k_sel, v_sel, valid)


def _short_conv(xc, bg, cg, conv_w, buf):
    u = cg * xc
    ext = jnp.concatenate([buf, u], axis=1)
    T = u.shape[1]
    y = sum(conv_w[j] * ext[:, j:j + T] for j in range(CONV_WIDTH))
    return bg * y, ext[:, -(CONV_WIDTH - 1):]


def _hier_moe(h, w_router, b_gr, b_er, w_g, w_u, w_d):
    B, T, D = h.shape
    t = h.reshape(-1, D)
    n = t.shape[0]
    r_logits = _mm_split(t, w_router)
    g_logits = r_logits[:, :N_GROUPS]
    g_prob = jax.nn.softmax(g_logits, axis=-1)
    g_sel = jnp.argmax(g_logits + b_gr, axis=-1)
    g_w = jnp.take_along_axis(g_prob, g_sel[:, None], axis=1)[:, 0]
    e_logits = r_logits[:, N_GROUPS:N_GROUPS * (1 + EXPERTS_PER_GROUP)].reshape(n, N_GROUPS, EXPERTS_PER_GROUP)
    e_in = jnp.take_along_axis(e_logits, g_sel[:, None, None], axis=1)[:, 0]
    _, e_idx = lax.top_k(e_in + b_er[g_sel], TOPK_EXPERTS)
    e_p = jax.nn.softmax(jnp.take_along_axis(e_in, e_idx, axis=1), axis=-1)
    sel_e = jnp.sum(jax.nn.one_hot(e_idx, EXPERTS_PER_GROUP, dtype=jnp.float32) * e_p[..., None], axis=1)
    gate = jax.nn.one_hot(g_sel, N_GROUPS, dtype=jnp.float32)[:, :, None] * (g_w[:, None] * sel_e)[:, None, :]
    gate = gate.reshape(n, N_GROUPS * EXPERTS_PER_GROUP)
    tb = t.astype(jnp.bfloat16)
    F = w_g.shape[-1]
    hg = _bmm_up(tb, w_g.reshape(-1, D, F))
    hu = _bmm_up(tb, w_u.reshape(-1, D, F))
    hid = jax.nn.silu(hg) * hu * gate.T[:, :, None]
    out = _bmm_down(hid, w_d.reshape(-1, F, D))
    return out.reshape(B, T, D)


def _block_forward(x, mod, pos, attend, conv_buf, lw):
    B, T, _ = x.shape
    sh1, sc1, g1, sh2, sc2, g2 = jnp.split(mod[:, None, :], 6, axis=-1)
    h = _rms_norm(x, lw['norm1_w']) * (1 + sc1) + sh1
    split_points = np.cumsum(IN_SPLITS)[:-1].tolist()
    proj = _mm(h.reshape(B * T, D_MODEL), lw['w_in']).reshape(B, T, -1)
    q, k, v, qi, ki, wi, xc, bg, cg, ga, gc = jnp.split(proj, split_points, axis=-1)
    q = _rotary(_rms_norm(q.reshape(B, T, N_HEADS, HEAD_DIM), lw['q_norm_w']), pos)
    k = _rotary(_rms_norm(k.reshape(B, T, N_HEADS, HEAD_DIM), lw['k_norm_w']), pos)
    v = v.reshape(B, T, N_HEADS, HEAD_DIM)
    qi = _rotary(qi.reshape(B, T, N_IDX_HEADS, IDX_DIM), pos)
    ki = _rotary(ki[:, :, None, :], pos)[:, :, 0]
    wi = wi * ((N_IDX_HEADS ** -0.5) * (IDX_DIM ** -0.5))
    attn = attend(q, k, v, qi, ki, wi)
    conv, new_buf = _short_conv(xc, bg, cg, lw['conv_w'], conv_buf)
    a_br = _mm(attn.reshape(B * T, ATTN_DIM), lw['w_branch_attn']).reshape(B, T, D_MODEL)
    c_br = _mm(conv.reshape(B * T, CONV_DIM), lw['w_branch_conv']).reshape(B, T, D_MODEL)
    merged = jax.nn.sigmoid(ga) * a_br + jax.nn.sigmoid(gc) * c_br
    x = x + g1 * _mm(merged.reshape(B * T, D_MODEL), lw['w_out']).reshape(B, T, D_MODEL)
    h2 = _rms_norm(x, lw['norm2_w']) * (1 + sc2) + sh2
    x = x + g2 * _hier_moe(h2, lw['w_router'], lw['b_group_router'], lw['b_expert_router'],
                           lw['w_e_gate'], lw['w_e_up'], lw['w_e_down'])
    return x, k, v, ki, new_buf


def kernel(x_prompt, x_sample, c_prompt, c_sample, cache_k, cache_v, cache_kidx, state_conv, page_table, norm1_w, norm2_w, w_ada, b_ada, w_in, q_norm_w, k_norm_w, conv_w, w_branch_attn, w_branch_conv, w_out, w_group_router, b_group_router, w_expert_router, b_expert_router, w_e_gate, w_e_up, w_e_down):
    B, T, _ = x_prompt.shape
    Bd, S, _ = x_sample.shape
    depth = norm1_w.shape[0]
    past = page_table.shape[1] * PAGE_SIZE
    pos_prompt = jnp.arange(T)
    pos_sample = past + jnp.arange(S)
    yp, ys = x_prompt, x_sample
    outs = [[] for _ in range(8)]
    c_all = jnp.concatenate([c_prompt, c_sample], axis=0)
    n_c = c_all.shape[0]
    c_pad = jnp.pad(jax.nn.silu(c_all), ((0, (-n_c) % 8), (0, 0)))
    for l in range(depth):
        w_router = jnp.pad(jnp.concatenate([w_group_router[l], w_expert_router[l].reshape(D_MODEL, -1)], axis=1),
                           ((0, 0), (0, 128 - N_GROUPS * (1 + EXPERTS_PER_GROUP))))
        lw = {
            'norm1_w': norm1_w[l], 'norm2_w': norm2_w[l],
            'w_in': w_in[l], 'q_norm_w': q_norm_w[l], 'k_norm_w': k_norm_w[l], 'conv_w': conv_w[l],
            'w_branch_attn': w_branch_attn[l], 'w_branch_conv': w_branch_conv[l], 'w_out': w_out[l],
            'w_router': w_router, 'b_group_router': b_group_router[l],
            'b_expert_router': b_expert_router[l],
            'w_e_gate': w_e_gate[l], 'w_e_up': w_e_up[l], 'w_e_down': w_e_down[l],
        }
        mod = _mm(c_pad, w_ada[l], tn=1024)[:n_c] + b_ada[l]
        zero_buf = jnp.zeros((B, CONV_WIDTH - 1, CONV_DIM), x_prompt.dtype)
        yp, kp, vp, kip, cp = _block_forward(yp, mod[:B], pos_prompt, _prompt_attention, zero_buf, lw)
        samp_attend = functools.partial(_sample_sparse_attention, cache_k=cache_k[l], cache_v=cache_v[l],
                                        cache_kidx=cache_kidx[l], page_table=page_table)
        ys, ksm, vsm, kis, cs = _block_forward(ys, mod[B:], pos_sample, samp_attend, state_conv[l], lw)
        for lst, val in zip(outs, (kp, vp, kip, cp, ksm, vsm, kis, cs)):
            lst.append(val)
    return (yp, ys) + tuple(jnp.stack(o) for o in outs)
```

```python
import functools

import jax
import jax.numpy as jnp
import numpy as np
from jax import lax
from jax.experimental import pallas as pl
from jax.experimental.pallas import tpu as pltpu

D_MODEL = 2048
PAGE_SIZE = 128
N_HEADS = 8
HEAD_DIM = 128
ATTN_DIM = N_HEADS * HEAD_DIM
N_IDX_HEADS = 8
IDX_DIM = 64
TOPK_MAX = 256
CONV_DIM = D_MODEL // 2
CONV_WIDTH = 3
N_GROUPS = 4
EXPERTS_PER_GROUP = 8
TOPK_EXPERTS = 2
EXPERT_FF = 512
ROPE_THETA = 10000.0
EPS = 1e-6
Q_BLOCK = 128
IN_SPLITS = (ATTN_DIM, ATTN_DIM, ATTN_DIM, N_IDX_HEADS * IDX_DIM, IDX_DIM, N_IDX_HEADS,
             CONV_DIM, CONV_DIM, CONV_DIM, D_MODEL, D_MODEL)

VMEM_LIMIT_BYTES = 56 * 1024 * 1024


def _mm_body(a_ref, b_ref, o_ref):
    a = a_ref[...].astype(jnp.bfloat16)
    b = b_ref[...].astype(jnp.bfloat16)
    o_ref[...] = jnp.dot(a, b, preferred_element_type=jnp.float32).astype(o_ref.dtype)


def _mm(a, b, *, tm=512, tn=512, out_dtype=jnp.float32):
    M, K = a.shape
    _, N = b.shape
    tm = min(tm, M)
    tn = min(tn, N)
    return pl.pallas_call(
        _mm_body,
        grid=(pl.cdiv(M, tm), pl.cdiv(N, tn)),
        in_specs=[pl.BlockSpec((tm, K), lambda i, j: (i, 0)),
                  pl.BlockSpec((K, tn), lambda i, j: (0, j))],
        out_specs=pl.BlockSpec((tm, tn), lambda i, j: (i, j)),
        out_shape=jax.ShapeDtypeStruct((M, N), out_dtype),
        compiler_params=pltpu.CompilerParams(
            dimension_semantics=("arbitrary", "arbitrary"),
            vmem_limit_bytes=VMEM_LIMIT_BYTES),
        name="mm",
    )(a, b)


def _hi_lo(x):
    hi = x.astype(jnp.bfloat16)
    lo = (x - hi.astype(jnp.float32)).astype(jnp.bfloat16)
    return hi, lo


def _mm_split_body(a_ref, b_ref, o_ref):
    a_hi, a_lo = _hi_lo(a_ref[...])
    b_hi, b_lo = _hi_lo(b_ref[...])
    dot = functools.partial(jnp.dot, preferred_element_type=jnp.float32)
    o_ref[...] = dot(a_hi, b_hi) + (dot(a_lo, b_hi) + dot(a_hi, b_lo))


def _mm_split(a, b, *, tm=512):
    M, K = a.shape
    _, N = b.shape
    tm = min(tm, M)
    return pl.pallas_call(
        _mm_split_body,
        grid=(pl.cdiv(M, tm),),
        in_specs=[pl.BlockSpec((tm, K), lambda i: (i, 0)),
                  pl.BlockSpec((K, N), lambda i: (0, 0))],
        out_specs=pl.BlockSpec((tm, N), lambda i: (i, 0)),
        out_shape=jax.ShapeDtypeStruct((M, N), jnp.float32),
        compiler_params=pltpu.CompilerParams(
            dimension_semantics=("arbitrary",),
            vmem_limit_bytes=VMEM_LIMIT_BYTES),
        name="mm_split",
    )(a, b)


def _bmm_up_body(x_ref, w_ref, o_ref):
    o_ref[...] = jnp.dot(x_ref[...], w_ref[...].astype(jnp.bfloat16),
                         preferred_element_type=jnp.float32)


def _bmm_up(x, w, *, tm=1024):
    N, D = x.shape
    E, _, F = w.shape
    tm = min(tm, N)
    return pl.pallas_call(
        _bmm_up_body,
        grid=(E, pl.cdiv(N, tm)),
        in_specs=[pl.BlockSpec((tm, D), lambda e, i: (i, 0)),
                  pl.BlockSpec((None, D, F), lambda e, i: (e, 0, 0))],
        out_specs=pl.BlockSpec((None, tm, F), lambda e, i: (e, i, 0)),
        out_shape=jax.ShapeDtypeStruct((E, N, F), jnp.float32),
        compiler_params=pltpu.CompilerParams(
            dimension_semantics=("arbitrary", "arbitrary"),
            vmem_limit_bytes=VMEM_LIMIT_BYTES),
        name="bmm_up",
    )(x, w)


def _bmm_down_body(h_ref, w_ref, o_ref):
    @pl.when(pl.program_id(1) == 0)
    def _():
        o_ref[...] = jnp.zeros_like(o_ref)

    o_ref[...] += jnp.dot(h_ref[...].astype(jnp.bfloat16), w_ref[...].astype(jnp.bfloat16),
                          preferred_element_type=jnp.float32)


def _bmm_down(hid, w, *, tm=1024):
    E, N, F = hid.shape
    D = w.shape[2]
    tm = min(tm, N)
    return pl.pallas_call(
        _bmm_down_body,
        grid=(pl.cdiv(N, tm), E),
        in_specs=[pl.BlockSpec((None, tm, F), lambda i, e: (e, i, 0)),
                  pl.BlockSpec((None, F, D), lambda i, e: (e, 0, 0))],
        out_specs=pl.BlockSpec((tm, D), lambda i, e: (i, 0)),
        out_shape=jax.ShapeDtypeStruct((N, D), jnp.float32),
        compiler_params=pltpu.CompilerParams(
            dimension_semantics=("arbitrary", "arbitrary"),
            vmem_limit_bytes=VMEM_LIMIT_BYTES),
        name="bmm_down",
    )(hid, w)


ATTN_TQ = 256
ATTN_SEG = 1024
ATTN_SUB = 256
MASK_NEG = -1e30
KEY_NEG_INF = -2139095041
INT32_MIN = -2 ** 31


def _slab_sum(x):
    parts = [x[r * 8:(r + 1) * 8] for r in range(x.shape[0] // 8)]
    while len(parts) > 1:
        parts = [parts[a] + parts[a + 1] for a in range(0, len(parts), 2)]
    return parts[0]


def _prompt_attn_body(qi_ref, ki_ref, wit_ref, q_ref, k_ref, vt_ref, o_ref, key_ref, bias_ref, *, topk, idx_bits):
    tq = ATTN_TQ
    i = pl.program_id(1)
    q0 = i * tq
    nseg = (q0 + tq + ATTN_SEG - 1) // ATTN_SEG
    ntile = nseg * (ATTN_SEG // ATTN_SUB)
    qpos = q0 + lax.broadcasted_iota(jnp.int32, (1, tq), 1)
    nt_dims = (((1,), (1,)), ((), ()))

    def score_tile(t, c):
        r0 = pl.multiple_of(t * ATTN_SUB, ATTN_SUB)
        ki = ki_ref[pl.ds(r0, ATTN_SUB), :]
        acc = jnp.zeros((ATTN_SUB, tq), jnp.float32)
        for h in range(N_IDX_HEADS):
            s = lax.dot_general(ki, qi_ref[h], nt_dims, preferred_element_type=jnp.float32)
            acc = acc + jnp.maximum(s, 0.0) * wit_ref[h:h + 1, :]
        kpos = r0 + lax.broadcasted_iota(jnp.int32, (ATTN_SUB, 1), 0)
        bits = lax.bitcast_convert_type(acc, jnp.int32)
        skey = jnp.where(bits < 0, bits ^ jnp.int32(0x7FFFFFFF), bits)
        skey = jnp.where(acc == 0.0, 0, skey)
        key_ref[pl.ds(r0, ATTN_SUB), :] = jnp.where(kpos <= qpos, skey, KEY_NEG_INF)
        return c

    lax.fori_loop(0, ntile, score_tile, 0)

    def count(pred):
        def tile(t, cnt):
            r0 = pl.multiple_of(t * ATTN_SUB, ATTN_SUB)
            kk = key_ref[pl.ds(r0, ATTN_SUB), :]
            kpos = r0 + lax.broadcasted_iota(jnp.int32, (ATTN_SUB, 1), 0)
            return cnt + _slab_sum(pred(kk, kpos).astype(jnp.int32))
        cnt = lax.fori_loop(0, ntile, tile, jnp.zeros((8, tq), jnp.int32))
        return jnp.sum(cnt, axis=0, keepdims=True)

    def bit_pass(p, t_u):
        cand_u = t_u | lax.shift_left(jnp.int32(1), 31 - p)
        cand_s = cand_u ^ jnp.int32(INT32_MIN)
        c = count(lambda kk, kpos: kk >= cand_s)
        return jnp.where(c >= topk, cand_u, t_u)

    t_u = lax.fori_loop(0, 32, bit_pass, jnp.zeros((1, tq), jnp.int32))
    t_s = t_u ^ jnp.int32(INT32_MIN)

    need = topk - count(lambda kk, kpos: kk > t_s)

    def idx_pass(p, lo):
        cand = lo | lax.shift_left(jnp.int32(1), idx_bits - 1 - p)
        c = count(lambda kk, kpos: (kk == t_s) & (kpos < cand))
        return jnp.where(c < need, cand, lo)

    last_tie = lax.fori_loop(0, idx_bits, idx_pass, jnp.zeros((1, tq), jnp.int32))

    def bias_tile(t, c):
        r0 = pl.multiple_of(t * ATTN_SUB, ATTN_SUB)
        kk = key_ref[pl.ds(r0, ATTN_SUB), :]
        kpos = r0 + lax.broadcasted_iota(jnp.int32, (ATTN_SUB, 1), 0)
        sel = (kk > t_s) | ((kk == t_s) & (kpos <= last_tie))
        sel = sel & (kk > KEY_NEG_INF)
        bias_ref[pl.ds(r0, ATTN_SUB), :] = jnp.where(sel, 0.0, MASK_NEG)
        return c

    lax.fori_loop(0, ntile, bias_tile, 0)

    scale = HEAD_DIM ** -0.5

    def head_body(h, c):
        qh = q_ref[h]

        def seg_body(s, carry):
            m, l, acc = carry
            r0 = pl.multiple_of(s * ATTN_SEG, ATTN_SEG)
            kh = k_ref[h, pl.ds(r0, ATTN_SEG), :]
            logits = lax.dot_general(kh, qh, nt_dims, preferred_element_type=jnp.float32) * scale
            logits = logits + bias_ref[pl.ds(r0, ATTN_SEG), :]
            m_new = jnp.maximum(m, jnp.max(logits, axis=0, keepdims=True))
            alpha = jnp.exp(m - m_new)
            p = jnp.exp(logits - m_new)
            l = alpha * l + jnp.sum(p, axis=0, keepdims=True)
            acc = alpha * acc + jnp.dot(vt_ref[h, s], p.astype(jnp.bfloat16),
                                        preferred_element_type=jnp.float32)
            return m_new, l, acc

        init = (jnp.full((1, tq), MASK_NEG, jnp.float32), jnp.zeros((1, tq), jnp.float32),
                jnp.zeros((HEAD_DIM, tq), jnp.float32))
        _, l, acc = lax.fori_loop(0, nseg, seg_body, init)
        o_ref[h] = (acc / l).astype(o_ref.dtype)
        return c

    lax.fori_loop(0, N_HEADS, head_body, 0)


def _prompt_attention(q, k, v, qi, ki, wi):
    B, T = q.shape[:2]
    topk = min(TOPK_MAX, T // 4)
    nseg = T // ATTN_SEG
    bf = jnp.bfloat16
    q_t = q.astype(bf).transpose(0, 2, 1, 3)
    k_t = k.astype(bf).transpose(0, 2, 1, 3)
    vt = v.astype(bf).reshape(B, nseg, ATTN_SEG, N_HEADS, HEAD_DIM).transpose(0, 3, 1, 4, 2)
    qi_t = qi.astype(bf).transpose(0, 2, 1, 3)
    ki_b = ki.astype(bf)
    wit = wi.transpose(0, 2, 1)
    body = functools.partial(_prompt_attn_body, topk=topk, idx_bits=int(np.ceil(np.log2(T))))
    out = pl.pallas_call(
        body,
        grid=(B, T // ATTN_TQ),
        in_specs=[
            pl.BlockSpec((None, N_IDX_HEADS, ATTN_TQ, IDX_DIM), lambda b, i: (b, 0, i, 0)),
            pl.BlockSpec((None, T, IDX_DIM), lambda b, i: (b, 0, 0)),
            pl.BlockSpec((None, N_IDX_HEADS, ATTN_TQ), lambda b, i: (b, 0, i)),
            pl.BlockSpec((None, N_HEADS, ATTN_TQ, HEAD_DIM), lambda b, i: (b, 0, i, 0)),
            pl.BlockSpec((None, N_HEADS, T, HEAD_DIM), lambda b, i: (b, 0, 0, 0)),
            pl.BlockSpec((None, N_HEADS, nseg, HEAD_DIM, ATTN_SEG), lambda b, i: (b, 0, 0, 0, 0)),
        ],
        out_specs=pl.BlockSpec((None, N_HEADS, HEAD_DIM, ATTN_TQ), lambda b, i: (b, 0, 0, i)),
        out_shape=jax.ShapeDtypeStruct((B, N_HEADS, HEAD_DIM, T), bf),
        scratch_shapes=[pltpu.VMEM((T, ATTN_TQ), jnp.int32), pltpu.VMEM((T, ATTN_TQ), jnp.float32)],
        compiler_params=pltpu.CompilerParams(
            dimension_semantics=("arbitrary", "arbitrary"),
            vmem_limit_bytes=VMEM_LIMIT_BYTES),
        name="prompt_attn",
    )(qi_t, ki_b, wit, q_t, k_t, vt)
    return out.transpose(0, 3, 1, 2)


SAMPLE_IDX_PAGES = 8
SAMPLE_KV_PAGES = 4


def _score_key(score):
    bits = lax.bitcast_convert_type(score, jnp.int32)
    key = jnp.where(bits < 0, bits ^ jnp.int32(0x7FFFFFFF), bits)
    return jnp.where(score == 0.0, 0, key)


def _sample_select_body(pt_ref, qi_ref, w_ref, kin_ref, *rest, topk, n_pages, idx_bits):
    del pt_ref
    kid_refs = rest[:SAMPLE_IDX_PAGES]
    sel_ref, key_scr = rest[SAMPLE_IDX_PAGES:]
    g = pl.program_id(1)
    n_q = key_scr.shape[1]
    past = n_pages * PAGE_SIZE
    qi = qi_ref[...]
    wcol = w_ref[...]
    qpos = past + lax.broadcasted_iota(jnp.int32, (n_q, 1), 0)
    lane = lax.broadcasted_iota(jnp.int32, (1, PAGE_SIZE), 1)

    def page_keys(ki_page, page):
        s = lax.dot_general(qi, ki_page.astype(jnp.bfloat16), (((1,), (1,)), ((), ())),
                            preferred_element_type=jnp.float32)
        score = _slab_sum_n(jnp.maximum(s, 0.0) * wcol, n_q)
        kpos = page * PAGE_SIZE + lane
        return jnp.where(kpos <= qpos, _score_key(score), KEY_NEG_INF)

    for j in range(SAMPLE_IDX_PAGES):
        page = g * SAMPLE_IDX_PAGES + j
        key_scr[page] = page_keys(kid_refs[j][...], page)

    @pl.when(g == 0)
    def _():
        key_scr[n_pages] = page_keys(kin_ref[...], n_pages)

    @pl.when(g == pl.num_programs(1) - 1)
    def _():
        kk = key_scr[...]
        kpos = (lax.broadcasted_iota(jnp.int32, kk.shape, 0) * PAGE_SIZE
                + lax.broadcasted_iota(jnp.int32, kk.shape, 2))

        def count(m):
            return jnp.sum(jnp.sum(m.astype(jnp.int32), axis=0), axis=1, keepdims=True)

        def bit_pass(p, t_u):
            cand_u = t_u | lax.shift_left(jnp.int32(1), 31 - p)
            cand_s = cand_u ^ jnp.int32(INT32_MIN)
            return jnp.where(count(kk >= cand_s[None]) >= topk, cand_u, t_u)

        t_u = lax.fori_loop(0, 32, bit_pass, jnp.zeros((n_q, 1), jnp.int32))
        t_s = (t_u ^ jnp.int32(INT32_MIN))[None]
        need = topk - count(kk > t_s)

        def idx_pass(p, lo):
            cand = lo | lax.shift_left(jnp.int32(1), idx_bits - 1 - p)
            return jnp.where(count((kk == t_s) & (kpos < cand[None])) < need, cand, lo)

        last_tie = lax.fori_loop(0, idx_bits, idx_pass, jnp.zeros((n_q, 1), jnp.int32))
        sel = (kk > t_s) | ((kk == t_s) & (kpos <= last_tie[None]))
        sel = sel & (kk > KEY_NEG_INF)
        sel_ref[...] = sel.astype(jnp.float32)


def _slab_sum_n(x, n):
    parts = [x[r * n:(r + 1) * n] for r in range(x.shape[0] // n)]
    while len(parts) > 1:
        parts = [parts[a] + parts[a + 1] for a in range(0, len(parts), 2)]
    return parts[0]


def _sample_attn_body(pt_ref, q_ref, sel_ref, seln_ref, exp_ref, hm_ref, kn_ref, vn_ref, *rest):
    del pt_ref
    k_refs = rest[:SAMPLE_KV_PAGES]
    v_refs = rest[SAMPLE_KV_PAGES:2 * SAMPLE_KV_PAGES]
    o_ref, m_scr, l_scr, acc_scr = rest[2 * SAMPLE_KV_PAGES:]
    g = pl.program_id(1)
    n_q = seln_ref.shape[1]
    bf = jnp.bfloat16
    qall = q_ref[...]
    scale = HEAD_DIM ** -0.5

    @pl.when(g == 0)
    def _():
        m_scr[...] = jnp.full(m_scr.shape, MASK_NEG, jnp.float32)
        l_scr[...] = jnp.zeros(l_scr.shape, jnp.float32)
        acc_scr[...] = jnp.zeros(acc_scr.shape, jnp.float32)

    def page_update(k_page, v_page, sel_tile, ncol):
        kp = k_page.reshape(-1, HEAD_DIM).astype(bf)
        vp = v_page.reshape(-1, HEAD_DIM).astype(bf)
        sel_cols = jnp.dot(sel_tile.astype(bf), exp_ref[:, :ncol], preferred_element_type=jnp.float32)
        sel_bias = (sel_cols - 1.0) * (-MASK_NEG)
        bias = jnp.concatenate([sel_bias] * N_HEADS, axis=0) + hm_ref[:, :ncol]
        logits = lax.dot_general(qall, kp, (((1,), (1,)), ((), ())),
                                 preferred_element_type=jnp.float32) * scale + bias
        m = m_scr[...]
        m_new = jnp.maximum(m, jnp.max(logits, axis=1, keepdims=True))
        alpha = jnp.exp(m - m_new)
        p = jnp.exp(logits - m_new)
        l_scr[...] = alpha * l_scr[...] + jnp.sum(p, axis=1, keepdims=True)
        acc_scr[...] = alpha * acc_scr[...] + jnp.dot(p.astype(bf), vp, preferred_element_type=jnp.float32)
        m_scr[...] = m_new

    @pl.when(g == 0)
    def _():
        page_update(kn_ref[...], vn_ref[...], seln_ref[0], n_q * N_HEADS)

    for j in range(SAMPLE_KV_PAGES):
        page_update(k_refs[j][...], v_refs[j][...], sel_ref[j], PAGE_SIZE * N_HEADS)

    @pl.when(g == pl.num_programs(1) - 1)
    def _():
        o_ref[...] = acc_scr[...] / l_scr[...]


def _sample_attention(q, k, v, qi, ki, wi, *, layer, cache_k, cache_v, cache_kidx, page_table):
    Bd, S = q.shape[:2]
    n_pages = page_table.shape[1]
    past = n_pages * PAGE_SIZE
    topk = min(TOPK_MAX, (past + S) // 4)
    bf = jnp.bfloat16
    rows = N_HEADS * S
    qi_r = qi.astype(bf).transpose(0, 2, 1, 3).reshape(Bd, N_IDX_HEADS * S, IDX_DIM)
    w_r = jnp.broadcast_to(wi.transpose(0, 2, 1).reshape(Bd, N_IDX_HEADS * S, 1),
                           (Bd, N_IDX_HEADS * S, PAGE_SIZE))
    ki_new = jnp.pad(ki, ((0, 0), (0, PAGE_SIZE - S), (0, 0)))
    idx_bits = int(np.ceil(np.log2(past + PAGE_SIZE)))

    def kid_spec(j):
        return pl.BlockSpec((None, None, PAGE_SIZE, IDX_DIM),
                            lambda b, g, pt: (layer, pt[b, g * SAMPLE_IDX_PAGES + j], 0, 0))

    sel = pl.pallas_call(
        functools.partial(_sample_select_body, topk=topk, n_pages=n_pages, idx_bits=idx_bits),
        grid_spec=pltpu.PrefetchScalarGridSpec(
            num_scalar_prefetch=1,
            grid=(Bd, n_pages // SAMPLE_IDX_PAGES),
            in_specs=[
                pl.BlockSpec((None, N_IDX_HEADS * S, IDX_DIM), lambda b, g, pt: (b, 0, 0)),
                pl.BlockSpec((None, N_IDX_HEADS * S, PAGE_SIZE), lambda b, g, pt: (b, 0, 0)),
                pl.BlockSpec((None, PAGE_SIZE, IDX_DIM), lambda b, g, pt: (b, 0, 0)),
            ] + [kid_spec(j) for j in range(SAMPLE_IDX_PAGES)],
            out_specs=pl.BlockSpec((None, n_pages + 1, S, PAGE_SIZE), lambda b, g, pt: (b, 0, 0, 0)),
            scratch_shapes=[pltpu.VMEM((n_pages + 1, S, PAGE_SIZE), jnp.int32)],
        ),
        out_shape=jax.ShapeDtypeStruct((Bd, n_pages + 1, S, PAGE_SIZE), jnp.float32),
        compiler_params=pltpu.CompilerParams(
            dimension_semantics=("arbitrary", "arbitrary"),
            vmem_limit_bytes=VMEM_LIMIT_BYTES),
        name="sample_select",
    )(page_table, qi_r, w_r, ki_new, *([cache_kidx] * SAMPLE_IDX_PAGES))

    q_r = q.astype(bf).transpose(0, 2, 1, 3).reshape(Bd, rows, HEAD_DIM)
    ncol = PAGE_SIZE * N_HEADS
    col = np.arange(ncol)
    expand = jnp.asarray(col[None, :] // N_HEADS == np.arange(PAGE_SIZE)[:, None], bf)
    head_mask = jnp.asarray(np.where(col[None, :] % N_HEADS == np.arange(rows)[:, None] // S, 0.0, MASK_NEG),
                            jnp.float32)

    def kv_spec(j):
        return pl.BlockSpec((None, None, PAGE_SIZE, N_HEADS, HEAD_DIM),
                            lambda b, g, pt: (layer, pt[b, g * SAMPLE_KV_PAGES + j], 0, 0, 0))

    out = pl.pallas_call(
        _sample_attn_body,
        grid_spec=pltpu.PrefetchScalarGridSpec(
            num_scalar_prefetch=1,
            grid=(Bd, n_pages // SAMPLE_KV_PAGES),
            in_specs=[
                pl.BlockSpec((None, rows, HEAD_DIM), lambda b, g, pt: (b, 0, 0)),
                pl.BlockSpec((None, SAMPLE_KV_PAGES, S, PAGE_SIZE), lambda b, g, pt: (b, g, 0, 0)),
                pl.BlockSpec((None, 1, S, PAGE_SIZE), lambda b, g, pt: (b, n_pages, 0, 0)),
                pl.BlockSpec((PAGE_SIZE, ncol), lambda b, g, pt: (0, 0)),
                pl.BlockSpec((rows, ncol), lambda b, g, pt: (0, 0)),
                pl.BlockSpec((None, S, N_HEADS, HEAD_DIM), lambda b, g, pt: (b, 0, 0, 0)),
                pl.BlockSpec((None, S, N_HEADS, HEAD_DIM), lambda b, g, pt: (b, 0, 0, 0)),
            ] + [kv_spec(j) for j in range(SAMPLE_KV_PAGES)] * 2,
            out_specs=pl.BlockSpec((None, rows, HEAD_DIM), lambda b, g, pt: (b, 0, 0)),
            scratch_shapes=[pltpu.VMEM((rows, 1), jnp.float32), pltpu.VMEM((rows, 1), jnp.float32),
                            pltpu.VMEM((rows, HEAD_DIM), jnp.float32)],
        ),
        out_shape=jax.ShapeDtypeStruct((Bd, rows, HEAD_DIM), jnp.float32),
        compiler_params=pltpu.CompilerParams(
            dimension_semantics=("arbitrary", "arbitrary"),
            vmem_limit_bytes=VMEM_LIMIT_BYTES),
        name="sample_attn",
    )(page_table, q_r, sel, sel, expand, head_mask, k, v,
      *([cache_k] * SAMPLE_KV_PAGES), *([cache_v] * SAMPLE_KV_PAGES))
    return out.reshape(Bd, N_HEADS, S, HEAD_DIM).transpose(0, 2, 1, 3)


def _rms_norm(x, w):
    xf = x.astype(jnp.float32)
    y = xf * lax.rsqrt(jnp.mean(xf * xf, axis=-1, keepdims=True) + EPS)
    return y * w.astype(jnp.float32)


def _rotary(x, pos):
    half = x.shape[-1] // 2
    inv = 1.0 / jnp.power(ROPE_THETA, jnp.arange(half, dtype=jnp.float32) / half)
    ang = pos.astype(jnp.float32)[:, None] * inv[None, :]
    cos = jnp.cos(ang)[None, :, None, :]
    sin = jnp.sin(ang)[None, :, None, :]
    x1 = x[..., :half]
    x2 = x[..., half:]
    return jnp.concatenate([x1 * cos - x2 * sin, x2 * cos + x1 * sin], axis=-1)


def _short_conv(xc, bg, cg, conv_w, buf):
    u = cg * xc
    ext = jnp.concatenate([buf, u], axis=1)
    T = u.shape[1]
    y = sum(conv_w[j] * ext[:, j:j + T] for j in range(CONV_WIDTH))
    return bg * y, ext[:, -(CONV_WIDTH - 1):]


def _hier_moe(h, w_router, b_gr, b_er, w_g, w_u, w_d):
    B, T, D = h.shape
    t = h.reshape(-1, D)
    n = t.shape[0]
    r_logits = _mm_split(t, w_router)
    g_logits = r_logits[:, :N_GROUPS]
    g_prob = jax.nn.softmax(g_logits, axis=-1)
    g_sel = jnp.argmax(g_logits + b_gr, axis=-1)
    g_w = jnp.take_along_axis(g_prob, g_sel[:, None], axis=1)[:, 0]
    e_logits = r_logits[:, N_GROUPS:N_GROUPS * (1 + EXPERTS_PER_GROUP)].reshape(n, N_GROUPS, EXPERTS_PER_GROUP)
    e_in = jnp.take_along_axis(e_logits, g_sel[:, None, None], axis=1)[:, 0]
    _, e_idx = lax.top_k(e_in + b_er[g_sel], TOPK_EXPERTS)
    e_p = jax.nn.softmax(jnp.take_along_axis(e_in, e_idx, axis=1), axis=-1)
    sel_e = jnp.sum(jax.nn.one_hot(e_idx, EXPERTS_PER_GROUP, dtype=jnp.float32) * e_p[..., None], axis=1)
    gate = jax.nn.one_hot(g_sel, N_GROUPS, dtype=jnp.float32)[:, :, None] * (g_w[:, None] * sel_e)[:, None, :]
    gate = gate.reshape(n, N_GROUPS * EXPERTS_PER_GROUP)
    tb = t.astype(jnp.bfloat16)
    F = w_g.shape[-1]
    hg = _bmm_up(tb, w_g.reshape(-1, D, F))
    hu = _bmm_up(tb, w_u.reshape(-1, D, F))
    hid = jax.nn.silu(hg) * hu * gate.T[:, :, None]
    out = _bmm_down(hid, w_d.reshape(-1, F, D))
    return out.reshape(B, T, D)


def _block_forward(x, mod, pos, attend, conv_buf, lw):
    B, T, _ = x.shape
    sh1, sc1, g1, sh2, sc2, g2 = jnp.split(mod[:, None, :], 6, axis=-1)
    h = _rms_norm(x, lw['norm1_w']) * (1 + sc1) + sh1
    split_points = np.cumsum(IN_SPLITS)[:-1].tolist()
    proj = _mm(h.reshape(B * T, D_MODEL), lw['w_in']).reshape(B, T, -1)
    q, k, v, qi, ki, wi, xc, bg, cg, ga, gc = jnp.split(proj, split_points, axis=-1)
    q = _rotary(_rms_norm(q.reshape(B, T, N_HEADS, HEAD_DIM), lw['q_norm_w']), pos)
    k = _rotary(_rms_norm(k.reshape(B, T, N_HEADS, HEAD_DIM), lw['k_norm_w']), pos)
    v = v.reshape(B, T, N_HEADS, HEAD_DIM)
    qi = _rotary(qi.reshape(B, T, N_IDX_HEADS, IDX_DIM), pos)
    ki = _rotary(ki[:, :, None, :], pos)[:, :, 0]
    wi = wi * ((N_IDX_HEADS ** -0.5) * (IDX_DIM ** -0.5))
    attn = attend(q, k, v, qi, ki, wi)
    conv, new_buf = _short_conv(xc, bg, cg, lw['conv_w'], conv_buf)
    a_br = _mm(attn.reshape(B * T, ATTN_DIM), lw['w_branch_attn']).reshape(B, T, D_MODEL)
    c_br = _mm(conv.reshape(B * T, CONV_DIM), lw['w_branch_conv']).reshape(B, T, D_MODEL)
    merged = jax.nn.sigmoid(ga) * a_br + jax.nn.sigmoid(gc) * c_br
    x = x + g1 * _mm(merged.reshape(B * T, D_MODEL), lw['w_out']).reshape(B, T, D_MODEL)
    h2 = _rms_norm(x, lw['norm2_w']) * (1 + sc2) + sh2
    x = x + g2 * _hier_moe(h2, lw['w_router'], lw['b_group_router'], lw['b_expert_router'],
                           lw['w_e_gate'], lw['w_e_up'], lw['w_e_down'])
    return x, k, v, ki, new_buf


def kernel(x_prompt, x_sample, c_prompt, c_sample, cache_k, cache_v, cache_kidx, state_conv, page_table, norm1_w, norm2_w, w_ada, b_ada, w_in, q_norm_w, k_norm_w, conv_w, w_branch_attn, w_branch_conv, w_out, w_group_router, b_group_router, w_expert_router, b_expert_router, w_e_gate, w_e_up, w_e_down):
    B, T, _ = x_prompt.shape
    Bd, S, _ = x_sample.shape
    depth = norm1_w.shape[0]
    past = page_table.shape[1] * PAGE_SIZE
    pos_prompt = jnp.arange(T)
    pos_sample = past + jnp.arange(S)
    yp, ys = x_prompt, x_sample
    outs = [[] for _ in range(8)]
    c_all = jnp.concatenate([c_prompt, c_sample], axis=0)
    n_c = c_all.shape[0]
    c_pad = jnp.pad(jax.nn.silu(c_all), ((0, (-n_c) % 8), (0, 0)))
    for l in range(depth):
        w_router = jnp.pad(jnp.concatenate([w_group_router[l], w_expert_router[l].reshape(D_MODEL, -1)], axis=1),
                           ((0, 0), (0, 128 - N_GROUPS * (1 + EXPERTS_PER_GROUP))))
        lw = {
            'norm1_w': norm1_w[l], 'norm2_w': norm2_w[l],
            'w_in': w_in[l], 'q_norm_w': q_norm_w[l], 'k_norm_w': k_norm_w[l], 'conv_w': conv_w[l],
            'w_branch_attn': w_branch_attn[l], 'w_branch_conv': w_branch_conv[l], 'w_out': w_out[l],
            'w_router': w_router, 'b_group_router': b_group_router[l],
            'b_expert_router': b_expert_router[l],
            'w_e_gate': w_e_gate[l], 'w_e_up': w_e_up[l], 'w_e_down': w_e_down[l],
        }
        mod = _mm(c_pad, w_ada[l], tn=1024)[:n_c] + b_ada[l]
        zero_buf = jnp.zeros((B, CONV_WIDTH - 1, CONV_DIM), x_prompt.dtype)
        yp, kp, vp, kip, cp = _block_forward(yp, mod[:B], pos_prompt, _prompt_attention, zero_buf, lw)
        samp_attend = functools.partial(_sample_attention, layer=l, cache_k=cache_k, cache_v=cache_v,
                                        cache_kidx=cache_kidx, page_table=page_table)
        ys, ksm, vsm, kis, cs = _block_forward(ys, mod[B:], pos_sample, samp_attend, state_conv[l], lw)
        for lst, val in zip(outs, (kp, vp, kip, cp, ksm, vsm, kis, cs)):
            lst.append(val)
    return (yp, ys) + tuple(jnp.stack(o) for o in outs)
```

```python
import functools

import jax
import jax.numpy as jnp
import numpy as np
from jax import lax
from jax.experimental import pallas as pl
from jax.experimental.pallas import tpu as pltpu

D_MODEL = 2048
PAGE_SIZE = 128
N_HEADS = 8
HEAD_DIM = 128
ATTN_DIM = N_HEADS * HEAD_DIM
N_IDX_HEADS = 8
IDX_DIM = 64
TOPK_MAX = 256
CONV_DIM = D_MODEL // 2
CONV_WIDTH = 3
N_GROUPS = 4
EXPERTS_PER_GROUP = 8
TOPK_EXPERTS = 2
EXPERT_FF = 512
ROPE_THETA = 10000.0
EPS = 1e-6
Q_BLOCK = 128
IN_SPLITS = (ATTN_DIM, ATTN_DIM, ATTN_DIM, N_IDX_HEADS * IDX_DIM, IDX_DIM, N_IDX_HEADS,
             CONV_DIM, CONV_DIM, CONV_DIM, D_MODEL, D_MODEL)

VMEM_LIMIT_BYTES = 56 * 1024 * 1024


def _mm_body(a_ref, b_ref, o_ref):
    a = a_ref[...].astype(jnp.bfloat16)
    b = b_ref[...].astype(jnp.bfloat16)
    o_ref[...] = jnp.dot(a, b, preferred_element_type=jnp.float32).astype(o_ref.dtype)


def _mm(a, b, *, tm=512, tn=512, out_dtype=jnp.float32):
    M, K = a.shape
    _, N = b.shape
    tm = min(tm, M)
    tn = min(tn, N)
    return pl.pallas_call(
        _mm_body,
        grid=(pl.cdiv(M, tm), pl.cdiv(N, tn)),
        in_specs=[pl.BlockSpec((tm, K), lambda i, j: (i, 0)),
                  pl.BlockSpec((K, tn), lambda i, j: (0, j))],
        out_specs=pl.BlockSpec((tm, tn), lambda i, j: (i, j)),
        out_shape=jax.ShapeDtypeStruct((M, N), out_dtype),
        compiler_params=pltpu.CompilerParams(
            dimension_semantics=("arbitrary", "arbitrary"),
            vmem_limit_bytes=VMEM_LIMIT_BYTES),
        name="mm",
    )(a, b)


def _hi_lo(x):
    hi = x.astype(jnp.bfloat16)
    lo = (x - hi.astype(jnp.float32)).astype(jnp.bfloat16)
    return hi, lo


def _mm_split_body(a_ref, b_ref, o_ref):
    a_hi, a_lo = _hi_lo(a_ref[...])
    b_hi, b_lo = _hi_lo(b_ref[...])
    dot = functools.partial(jnp.dot, preferred_element_type=jnp.float32)
    o_ref[...] = dot(a_hi, b_hi) + (dot(a_lo, b_hi) + dot(a_hi, b_lo))


def _mm_split(a, b, *, tm=512):
    M, K = a.shape
    _, N = b.shape
    tm = min(tm, M)
    return pl.pallas_call(
        _mm_split_body,
        grid=(pl.cdiv(M, tm),),
        in_specs=[pl.BlockSpec((tm, K), lambda i: (i, 0)),
                  pl.BlockSpec((K, N), lambda i: (0, 0))],
        out_specs=pl.BlockSpec((tm, N), lambda i: (i, 0)),
        out_shape=jax.ShapeDtypeStruct((M, N), jnp.float32),
        compiler_params=pltpu.CompilerParams(
            dimension_semantics=("arbitrary",),
            vmem_limit_bytes=VMEM_LIMIT_BYTES),
        name="mm_split",
    )(a, b)


def _bmm_up_body(x_ref, w_ref, o_ref):
    o_ref[...] = jnp.dot(x_ref[...], w_ref[...].astype(jnp.bfloat16),
                         preferred_element_type=jnp.float32)


def _bmm_up(x, w, *, tm=1024):
    N, D = x.shape
    E, _, F = w.shape
    tm = min(tm, N)
    return pl.pallas_call(
        _bmm_up_body,
        grid=(E, pl.cdiv(N, tm)),
        in_specs=[pl.BlockSpec((tm, D), lambda e, i: (i, 0)),
                  pl.BlockSpec((None, D, F), lambda e, i: (e, 0, 0))],
        out_specs=pl.BlockSpec((None, tm, F), lambda e, i: (e, i, 0)),
        out_shape=jax.ShapeDtypeStruct((E, N, F), jnp.float32),
        compiler_params=pltpu.CompilerParams(
            dimension_semantics=("arbitrary", "arbitrary"),
            vmem_limit_bytes=VMEM_LIMIT_BYTES),
        name="bmm_up",
    )(x, w)


def _bmm_down_body(h_ref, w_ref, o_ref):
    @pl.when(pl.program_id(1) == 0)
    def _():
        o_ref[...] = jnp.zeros_like(o_ref)

    o_ref[...] += jnp.dot(h_ref[...].astype(jnp.bfloat16), w_ref[...].astype(jnp.bfloat16),
                          preferred_element_type=jnp.float32)


def _bmm_down(hid, w, *, tm=1024):
    E, N, F = hid.shape
    D = w.shape[2]
    tm = min(tm, N)
    return pl.pallas_call(
        _bmm_down_body,
        grid=(pl.cdiv(N, tm), E),
        in_specs=[pl.BlockSpec((None, tm, F), lambda i, e: (e, i, 0)),
                  pl.BlockSpec((None, F, D), lambda i, e: (e, 0, 0))],
        out_specs=pl.BlockSpec((tm, D), lambda i, e: (i, 0)),
        out_shape=jax.ShapeDtypeStruct((N, D), jnp.float32),
        compiler_params=pltpu.CompilerParams(
            dimension_semantics=("arbitrary", "arbitrary"),
            vmem_limit_bytes=VMEM_LIMIT_BYTES),
        name="bmm_down",
    )(hid, w)


ATTN_TQ = 256
ATTN_SEG = 1024
ATTN_SUB = 256
MASK_NEG = -1e30
KEY_NEG_INF = -2139095041
INT32_MIN = -2 ** 31


def _slab_sum(x):
    parts = [x[r * 8:(r + 1) * 8] for r in range(x.shape[0] // 8)]
    while len(parts) > 1:
        parts = [parts[a] + parts[a + 1] for a in range(0, len(parts), 2)]
    return parts[0]


def _prompt_attn_body(qi_ref, ki_ref, wit_ref, q_ref, k_ref, vt_ref, o_ref, key_ref, bias_ref, *, topk, idx_bits):
    tq = ATTN_TQ
    i = pl.program_id(1)
    q0 = i * tq
    nseg = (q0 + tq + ATTN_SEG - 1) // ATTN_SEG
    ntile = nseg * (ATTN_SEG // ATTN_SUB)
    qpos = q0 + lax.broadcasted_iota(jnp.int32, (1, tq), 1)
    nt_dims = (((1,), (1,)), ((), ()))

    def score_tile(t, c):
        r0 = pl.multiple_of(t * ATTN_SUB, ATTN_SUB)
        ki = ki_ref[pl.ds(r0, ATTN_SUB), :]
        acc = jnp.zeros((ATTN_SUB, tq), jnp.float32)
        for h in range(N_IDX_HEADS):
            s = lax.dot_general(ki, qi_ref[h], nt_dims, preferred_element_type=jnp.float32)
            acc = acc + jnp.maximum(s, 0.0) * wit_ref[h:h + 1, :]
        kpos = r0 + lax.broadcasted_iota(jnp.int32, (ATTN_SUB, 1), 0)
        bits = lax.bitcast_convert_type(acc, jnp.int32)
        skey = jnp.where(bits < 0, bits ^ jnp.int32(0x7FFFFFFF), bits)
        skey = jnp.where(acc == 0.0, 0, skey)
        key_ref[pl.ds(r0, ATTN_SUB), :] = jnp.where(kpos <= qpos, skey, KEY_NEG_INF)
        return c

    lax.fori_loop(0, ntile, score_tile, 0)

    def count(pred):
        def tile(t, cnt):
            r0 = pl.multiple_of(t * ATTN_SUB, ATTN_SUB)
            kk = key_ref[pl.ds(r0, ATTN_SUB), :]
            kpos = r0 + lax.broadcasted_iota(jnp.int32, (ATTN_SUB, 1), 0)
            return cnt + _slab_sum(pred(kk, kpos).astype(jnp.int32))
        cnt = lax.fori_loop(0, ntile, tile, jnp.zeros((8, tq), jnp.int32))
        return jnp.sum(cnt, axis=0, keepdims=True)

    def bit_pass(p, t_u):
        cand_u = t_u | lax.shift_left(jnp.int32(1), 31 - p)
        cand_s = cand_u ^ jnp.int32(INT32_MIN)
        c = count(lambda kk, kpos: kk >= cand_s)
        return jnp.where(c >= topk, cand_u, t_u)

    t_u = lax.fori_loop(0, 32, bit_pass, jnp.zeros((1, tq), jnp.int32))
    t_s = t_u ^ jnp.int32(INT32_MIN)

    need = topk - count(lambda kk, kpos: kk > t_s)

    def idx_pass(p, lo):
        cand = lo | lax.shift_left(jnp.int32(1), idx_bits - 1 - p)
        c = count(lambda kk, kpos: (kk == t_s) & (kpos < cand))
        return jnp.where(c < need, cand, lo)

    last_tie = lax.fori_loop(0, idx_bits, idx_pass, jnp.zeros((1, tq), jnp.int32))

    def bias_tile(t, c):
        r0 = pl.multiple_of(t * ATTN_SUB, ATTN_SUB)
        kk = key_ref[pl.ds(r0, ATTN_SUB), :]
        kpos = r0 + lax.broadcasted_iota(jnp.int32, (ATTN_SUB, 1), 0)
        sel = (kk > t_s) | ((kk == t_s) & (kpos <= last_tie))
        sel = sel & (kk > KEY_NEG_INF)
        bias_ref[pl.ds(r0, ATTN_SUB), :] = jnp.where(sel, 0.0, MASK_NEG)
        return c

    lax.fori_loop(0, ntile, bias_tile, 0)

    scale = HEAD_DIM ** -0.5

    def head_body(h, c):
        qh = q_ref[h]

        def seg_body(s, carry):
            m, l, acc = carry
            r0 = pl.multiple_of(s * ATTN_SEG, ATTN_SEG)
            kh = k_ref[h, pl.ds(r0, ATTN_SEG), :]
            logits = lax.dot_general(kh, qh, nt_dims, preferred_element_type=jnp.float32) * scale
            logits = logits + bias_ref[pl.ds(r0, ATTN_SEG), :]
            m_new = jnp.maximum(m, jnp.max(logits, axis=0, keepdims=True))
            alpha = jnp.exp(m - m_new)
            p = jnp.exp(logits - m_new)
            l = alpha * l + jnp.sum(p, axis=0, keepdims=True)
            acc = alpha * acc + jnp.dot(vt_ref[h, s], p.astype(jnp.bfloat16),
                                        preferred_element_type=jnp.float32)
            return m_new, l, acc

        init = (jnp.full((1, tq), MASK_NEG, jnp.float32), jnp.zeros((1, tq), jnp.float32),
                jnp.zeros((HEAD_DIM, tq), jnp.float32))
        _, l, acc = lax.fori_loop(0, nseg, seg_body, init)
        o_ref[h] = (acc / l).astype(o_ref.dtype)
        return c

    lax.fori_loop(0, N_HEADS, head_body, 0)


def _prompt_attention(q, k, v, qi, ki, wi):
    B, T = q.shape[:2]
    topk = min(TOPK_MAX, T // 4)
    nseg = T // ATTN_SEG
    bf = jnp.bfloat16
    q_t = q.astype(bf).transpose(0, 2, 1, 3)
    k_t = k.astype(bf).transpose(0, 2, 1, 3)
    vt = v.astype(bf).reshape(B, nseg, ATTN_SEG, N_HEADS, HEAD_DIM).transpose(0, 3, 1, 4, 2)
    qi_t = qi.astype(bf).transpose(0, 2, 1, 3)
    ki_b = ki.astype(bf)
    wit = wi.transpose(0, 2, 1)
    body = functools.partial(_prompt_attn_body, topk=topk, idx_bits=int(np.ceil(np.log2(T))))
    out = pl.pallas_call(
        body,
        grid=(B, T // ATTN_TQ),
        in_specs=[
            pl.BlockSpec((None, N_IDX_HEADS, ATTN_TQ, IDX_DIM), lambda b, i: (b, 0, i, 0)),
            pl.BlockSpec((None, T, IDX_DIM), lambda b, i: (b, 0, 0)),
            pl.BlockSpec((None, N_IDX_HEADS, ATTN_TQ), lambda b, i: (b, 0, i)),
            pl.BlockSpec((None, N_HEADS, ATTN_TQ, HEAD_DIM), lambda b, i: (b, 0, i, 0)),
            pl.BlockSpec((None, N_HEADS, T, HEAD_DIM), lambda b, i: (b, 0, 0, 0)),
            pl.BlockSpec((None, N_HEADS, nseg, HEAD_DIM, ATTN_SEG), lambda b, i: (b, 0, 0, 0, 0)),
        ],
        out_specs=pl.BlockSpec((None, N_HEADS, HEAD_DIM, ATTN_TQ), lambda b, i: (b, 0, 0, i)),
        out_shape=jax.ShapeDtypeStruct((B, N_HEADS, HEAD_DIM, T), bf),
        scratch_shapes=[pltpu.VMEM((T, ATTN_TQ), jnp.int32), pltpu.VMEM((T, ATTN_TQ), jnp.float32)],
        compiler_params=pltpu.CompilerParams(
            dimension_semantics=("arbitrary", "arbitrary"),
            vmem_limit_bytes=VMEM_LIMIT_BYTES),
        name="prompt_attn",
    )(qi_t, ki_b, wit, q_t, k_t, vt)
    return out.transpose(0, 3, 1, 2)


SAMPLE_IDX_PAGES = 8
SAMPLE_KV_PAGES = 4


def _score_key(score):
    bits = lax.bitcast_convert_type(score, jnp.int32)
    key = jnp.where(bits < 0, bits ^ jnp.int32(0x7FFFFFFF), bits)
    return jnp.where(score == 0.0, 0, key)


def _sample_select_body(pt_ref, qi_ref, w_ref, kin_ref, *rest, topk, n_pages, idx_bits):
    del pt_ref
    kid_refs = rest[:SAMPLE_IDX_PAGES]
    sel_ref, key_scr = rest[SAMPLE_IDX_PAGES:]
    g = pl.program_id(1)
    n_q = key_scr.shape[1]
    past = n_pages * PAGE_SIZE
    qi = qi_ref[...]
    wcol = w_ref[...]
    qpos = past + lax.broadcasted_iota(jnp.int32, (n_q, 1), 0)
    lane = lax.broadcasted_iota(jnp.int32, (1, PAGE_SIZE), 1)

    def page_keys(ki_page, page):
        s = lax.dot_general(qi, ki_page.astype(jnp.bfloat16), (((1,), (1,)), ((), ())),
                            preferred_element_type=jnp.float32)
        score = _slab_sum_n(jnp.maximum(s, 0.0) * wcol, n_q)
        kpos = page * PAGE_SIZE + lane
        return jnp.where(kpos <= qpos, _score_key(score), KEY_NEG_INF)

    for j in range(SAMPLE_IDX_PAGES):
        page = g * SAMPLE_IDX_PAGES + j
        key_scr[page] = page_keys(kid_refs[j][...], page)

    @pl.when(g == 0)
    def _():
        key_scr[n_pages] = page_keys(kin_ref[...], n_pages)

    @pl.when(g == pl.num_programs(1) - 1)
    def _():
        kk = key_scr[...]
        kpos = (lax.broadcasted_iota(jnp.int32, kk.shape, 0) * PAGE_SIZE
                + lax.broadcasted_iota(jnp.int32, kk.shape, 2))

        def count(m):
            return jnp.sum(jnp.sum(m.astype(jnp.int32), axis=0), axis=1, keepdims=True)

        def bit_pass(p, t_u):
            cand_u = t_u | lax.shift_left(jnp.int32(1), 31 - p)
            cand_s = cand_u ^ jnp.int32(INT32_MIN)
            return jnp.where(count(kk >= cand_s[None]) >= topk, cand_u, t_u)

        t_u = lax.fori_loop(0, 32, bit_pass, jnp.zeros((n_q, 1), jnp.int32))
        t_s = (t_u ^ jnp.int32(INT32_MIN))[None]
        need = topk - count(kk > t_s)

        def idx_pass(p, lo):
            cand = lo | lax.shift_left(jnp.int32(1), idx_bits - 1 - p)
            return jnp.where(count((kk == t_s) & (kpos < cand[None])) < need, cand, lo)

        last_tie = lax.fori_loop(0, idx_bits, idx_pass, jnp.zeros((n_q, 1), jnp.int32))
        sel = (kk > t_s) | ((kk == t_s) & (kpos <= last_tie[None]))
        sel = sel & (kk > KEY_NEG_INF)
        sel_ref[...] = sel.astype(jnp.float32)


def _slab_sum_n(x, n):
    parts = [x[r * n:(r + 1) * n] for r in range(x.shape[0] // n)]
    while len(parts) > 1:
        parts = [parts[a] + parts[a + 1] for a in range(0, len(parts), 2)]
    return parts[0]


def _sample_attn_body(pt_ref, q_ref, sel_ref, seln_ref, exp_ref, hm_ref, kn_ref, vn_ref, *rest):
    del pt_ref
    k_refs = rest[:SAMPLE_KV_PAGES]
    v_refs = rest[SAMPLE_KV_PAGES:2 * SAMPLE_KV_PAGES]
    o_ref, m_scr, l_scr, acc_scr = rest[2 * SAMPLE_KV_PAGES:]
    g = pl.program_id(1)
    n_q = seln_ref.shape[1]
    bf = jnp.bfloat16
    qall = q_ref[...]
    scale = HEAD_DIM ** -0.5

    @pl.when(g == 0)
    def _():
        m_scr[...] = jnp.full(m_scr.shape, MASK_NEG, jnp.float32)
        l_scr[...] = jnp.zeros(l_scr.shape, jnp.float32)
        acc_scr[...] = jnp.zeros(acc_scr.shape, jnp.float32)

    def page_update(k_page, v_page, sel_tile, ncol):
        kp = k_page.reshape(-1, HEAD_DIM).astype(bf)
        vp = v_page.reshape(-1, HEAD_DIM).astype(bf)
        sel_cols = jnp.dot(sel_tile.astype(bf), exp_ref[:, :ncol], preferred_element_type=jnp.float32)
        sel_bias = (sel_cols - 1.0) * (-MASK_NEG)
        bias = jnp.concatenate([sel_bias] * N_HEADS, axis=0) + hm_ref[:, :ncol]
        logits = lax.dot_general(qall, kp, (((1,), (1,)), ((), ())),
                                 preferred_element_type=jnp.float32) * scale + bias
        m = m_scr[...]
        m_new = jnp.maximum(m, jnp.max(logits, axis=1, keepdims=True))
        alpha = jnp.exp(m - m_new)
        p = jnp.exp(logits - m_new)
        l_scr[...] = alpha * l_scr[...] + jnp.sum(p, axis=1, keepdims=True)
        acc_scr[...] = alpha * acc_scr[...] + jnp.dot(p.astype(bf), vp, preferred_element_type=jnp.float32)
        m_scr[...] = m_new

    @pl.when(g == 0)
    def _():
        page_update(kn_ref[...], vn_ref[...], seln_ref[0], n_q * N_HEADS)

    for j in range(SAMPLE_KV_PAGES):
        page_update(k_refs[j][...], v_refs[j][...], sel_ref[j], PAGE_SIZE * N_HEADS)

    @pl.when(g == pl.num_programs(1) - 1)
    def _():
        o_ref[...] = acc_scr[...] / l_scr[...]


def _sample_attention(q, k, v, qi, ki, wi, *, layer, cache_k, cache_v, cache_kidx, page_table):
    Bd, S = q.shape[:2]
    n_pages = page_table.shape[1]
    past = n_pages * PAGE_SIZE
    topk = min(TOPK_MAX, (past + S) // 4)
    bf = jnp.bfloat16
    rows = N_HEADS * S
    qi_r = qi.astype(bf).transpose(0, 2, 1, 3).reshape(Bd, N_IDX_HEADS * S, IDX_DIM)
    w_r = jnp.broadcast_to(wi.transpose(0, 2, 1).reshape(Bd, N_IDX_HEADS * S, 1),
                           (Bd, N_IDX_HEADS * S, PAGE_SIZE))
    ki_new = jnp.pad(ki, ((0, 0), (0, PAGE_SIZE - S), (0, 0)))
    idx_bits = int(np.ceil(np.log2(past + PAGE_SIZE)))

    def kid_spec(j):
        return pl.BlockSpec((None, None, PAGE_SIZE, IDX_DIM),
                            lambda b, g, pt: (layer, pt[b, g * SAMPLE_IDX_PAGES + j], 0, 0))

    sel = pl.pallas_call(
        functools.partial(_sample_select_body, topk=topk, n_pages=n_pages, idx_bits=idx_bits),
        grid_spec=pltpu.PrefetchScalarGridSpec(
            num_scalar_prefetch=1,
            grid=(Bd, n_pages // SAMPLE_IDX_PAGES),
            in_specs=[
                pl.BlockSpec((None, N_IDX_HEADS * S, IDX_DIM), lambda b, g, pt: (b, 0, 0)),
                pl.BlockSpec((None, N_IDX_HEADS * S, PAGE_SIZE), lambda b, g, pt: (b, 0, 0)),
                pl.BlockSpec((None, PAGE_SIZE, IDX_DIM), lambda b, g, pt: (b, 0, 0)),
            ] + [kid_spec(j) for j in range(SAMPLE_IDX_PAGES)],
            out_specs=pl.BlockSpec((None, n_pages + 1, S, PAGE_SIZE), lambda b, g, pt: (b, 0, 0, 0)),
            scratch_shapes=[pltpu.VMEM((n_pages + 1, S, PAGE_SIZE), jnp.int32)],
        ),
        out_shape=jax.ShapeDtypeStruct((Bd, n_pages + 1, S, PAGE_SIZE), jnp.float32),
        compiler_params=pltpu.CompilerParams(
            dimension_semantics=("arbitrary", "arbitrary"),
            vmem_limit_bytes=VMEM_LIMIT_BYTES),
        name="sample_select",
    )(page_table, qi_r, w_r, ki_new, *([cache_kidx] * SAMPLE_IDX_PAGES))

    q_r = q.astype(bf).transpose(0, 2, 1, 3).reshape(Bd, rows, HEAD_DIM)
    ncol = PAGE_SIZE * N_HEADS
    col = np.arange(ncol)
    expand = jnp.asarray(col[None, :] // N_HEADS == np.arange(PAGE_SIZE)[:, None], bf)
    head_mask = jnp.asarray(np.where(col[None, :] % N_HEADS == np.arange(rows)[:, None] // S, 0.0, MASK_NEG),
                            jnp.float32)

    def kv_spec(j):
        return pl.BlockSpec((None, None, PAGE_SIZE, N_HEADS, HEAD_DIM),
                            lambda b, g, pt: (layer, pt[b, g * SAMPLE_KV_PAGES + j], 0, 0, 0))

    out = pl.pallas_call(
        _sample_attn_body,
        grid_spec=pltpu.PrefetchScalarGridSpec(
            num_scalar_prefetch=1,
            grid=(Bd, n_pages // SAMPLE_KV_PAGES),
            in_specs=[
                pl.BlockSpec((None, rows, HEAD_DIM), lambda b, g, pt: (b, 0, 0)),
                pl.BlockSpec((None, SAMPLE_KV_PAGES, S, PAGE_SIZE), lambda b, g, pt: (b, g, 0, 0)),
                pl.BlockSpec((None, 1, S, PAGE_SIZE), lambda b, g, pt: (b, n_pages, 0, 0)),
                pl.BlockSpec((PAGE_SIZE, ncol), lambda b, g, pt: (0, 0)),
                pl.BlockSpec((rows, ncol), lambda b, g, pt: (0, 0)),
                pl.BlockSpec((None, S, N_HEADS, HEAD_DIM), lambda b, g, pt: (b, 0, 0, 0)),
                pl.BlockSpec((None, S, N_HEADS, HEAD_DIM), lambda b, g, pt: (b, 0, 0, 0)),
            ] + [kv_spec(j) for j in range(SAMPLE_KV_PAGES)] * 2,
            out_specs=pl.BlockSpec((None, rows, HEAD_DIM), lambda b, g, pt: (b, 0, 0)),
            scratch_shapes=[pltpu.VMEM((rows, 1), jnp.float32), pltpu.VMEM((rows, 1), jnp.float32),
                            pltpu.VMEM((rows, HEAD_DIM), jnp.float32)],
        ),
        out_shape=jax.ShapeDtypeStruct((Bd, rows, HEAD_DIM), jnp.float32),
        compiler_params=pltpu.CompilerParams(
            dimension_semantics=("arbitrary", "arbitrary"),
            vmem_limit_bytes=VMEM_LIMIT_BYTES),
        name="sample_attn",
    )(page_table, q_r, sel, sel, expand, head_mask, k, v,
      *([cache_k] * SAMPLE_KV_PAGES), *([cache_v] * SAMPLE_KV_PAGES))
    return out.reshape(Bd, N_HEADS, S, HEAD_DIM).transpose(0, 2, 1, 3)


def _rms_norm(x, w):
    xf = x.astype(jnp.float32)
    y = xf * lax.rsqrt(jnp.mean(xf * xf, axis=-1, keepdims=True) + EPS)
    return y * w.astype(jnp.float32)


def _rotary(x, pos):
    half = x.shape[-1] // 2
    inv = 1.0 / jnp.power(ROPE_THETA, jnp.arange(half, dtype=jnp.float32) / half)
    ang = pos.astype(jnp.float32)[:, None] * inv[None, :]
    cos = jnp.cos(ang)[None, :, None, :]
    sin = jnp.sin(ang)[None, :, None, :]
    x1 = x[..., :half]
    x2 = x[..., half:]
    return jnp.concatenate([x1 * cos - x2 * sin, x2 * cos + x1 * sin], axis=-1)


def _short_conv(xc, bg, cg, conv_w, buf):
    u = cg * xc
    ext = jnp.concatenate([buf, u], axis=1)
    T = u.shape[1]
    y = sum(conv_w[j] * ext[:, j:j + T] for j in range(CONV_WIDTH))
    return bg * y, ext[:, -(CONV_WIDTH - 1):]


MOE_TM = 256
COMBINE_TM = 128


def _route(r_logits, b_gr, b_er):
    n = r_logits.shape[0]
    g_logits = r_logits[:, :N_GROUPS]
    g_prob = jax.nn.softmax(g_logits, axis=-1)
    g_sel = jnp.argmax(g_logits + b_gr, axis=-1)
    g_w = jnp.take_along_axis(g_prob, g_sel[:, None], axis=1)[:, 0]
    e_logits = r_logits[:, N_GROUPS:N_GROUPS * (1 + EXPERTS_PER_GROUP)].reshape(n, N_GROUPS, EXPERTS_PER_GROUP)
    e_in = jnp.take_along_axis(e_logits, g_sel[:, None, None], axis=1)[:, 0]
    _, e_idx = lax.top_k(e_in + b_er[g_sel], TOPK_EXPERTS)
    e_p = jax.nn.softmax(jnp.take_along_axis(e_in, e_idx, axis=1), axis=-1)
    eid = (g_sel[:, None] * EXPERTS_PER_GROUP + e_idx).astype(jnp.int32)
    return eid, g_w[:, None] * e_p


def _moe_plan(eid, wgt, n_tiles):
    n_exp = N_GROUPS * EXPERTS_PER_GROUP
    a = eid.size
    e_flat = eid.reshape(a)
    order = jnp.argsort(e_flat, stable=True).astype(jnp.int32)
    e_sorted = e_flat[order]
    counts = jnp.sum(e_flat[:, None] == jnp.arange(n_exp)[None, :], axis=0).astype(jnp.int32)
    padded = (counts + MOE_TM - 1) // MOE_TM * MOE_TM
    pad_end = jnp.cumsum(padded)
    dest = (pad_end - padded)[e_sorted] + jnp.arange(a, dtype=jnp.int32) - (jnp.cumsum(counts) - counts)[e_sorted]
    rows = n_tiles * MOE_TM
    row_token = jnp.zeros((rows,), jnp.int32).at[dest].set(order // TOPK_EXPERTS)
    row_weight = jnp.zeros((rows,), jnp.float32).at[dest].set(wgt.reshape(a)[order])
    pos = jnp.zeros((a,), jnp.int32).at[order].set(dest)
    tile_expert = jnp.minimum(jnp.searchsorted(pad_end, jnp.arange(n_tiles, dtype=jnp.int32) * MOE_TM, side='right'),
                              n_exp - 1).astype(jnp.int32)
    n_used = (pad_end[-1] // MOE_TM).astype(jnp.int32).reshape(1)
    return tile_expert, n_used, row_token, row_weight.reshape(rows, 1), pos


def _moe_body(te_ref, nu_ref, tok_ref, x_hbm, rw_ref, wg_ref, wu_ref, wd_ref, y_ref,
              buf, sem, wgb, wub, wdb):
    t = pl.program_id(0)
    n_used = nu_ref[0]
    slot = lax.rem(t, 2)
    bf = jnp.bfloat16

    def row_copy(tile, slot_, r):
        tok = tok_ref[tile * MOE_TM + r]
        return pltpu.make_async_copy(x_hbm.at[pl.ds(tok, 1), :], buf.at[slot_, pl.ds(r, 1), :], sem.at[slot_])

    def gather_start(tile, slot_):
        def body(r, c):
            row_copy(tile, slot_, r).start()
            return c
        lax.fori_loop(0, MOE_TM, body, 0, unroll=8)

    def gather_wait(tile, slot_):
        def body(r, c):
            row_copy(tile, slot_, r).wait()
            return c
        lax.fori_loop(0, MOE_TM, body, 0, unroll=8)

    @pl.when(t == 0)
    def _():
        gather_start(0, 0)

    @pl.when(t + 1 < n_used)
    def _():
        gather_start(t + 1, 1 - slot)

    @pl.when(t < n_used)
    def _():
        expert_changed = jnp.logical_or(t == 0, te_ref[t] != te_ref[jnp.maximum(t - 1, 0)])

        @pl.when(expert_changed)
        def _():
            wgb[...] = wg_ref[...].astype(bf)
            wub[...] = wu_ref[...].astype(bf)
            wdb[...] = wd_ref[...].astype(bf)

        gather_wait(t, slot)
        x = buf[slot].astype(bf)
        g = jnp.dot(x, wgb[...], preferred_element_type=jnp.float32)
        u = jnp.dot(x, wub[...], preferred_element_type=jnp.float32)
        hid = jax.nn.silu(g) * u * rw_ref[...]
        y_ref[...] = jnp.dot(hid.astype(bf), wdb[...], preferred_element_type=jnp.float32)

    @pl.when(t >= n_used)
    def _():
        y_ref[...] = jnp.zeros_like(y_ref)


def _moe_experts(x_all, plan, w_g, w_u, w_d, layer, n_tiles):
    tile_expert, n_used, row_token, row_weight, _ = plan
    D = x_all.shape[1]
    F = w_g.shape[-1]

    def w_spec(k_dim, n_dim):
        return pl.BlockSpec((None, None, None, k_dim, n_dim),
                            lambda t, te, nu, tok: (layer, te[t] // EXPERTS_PER_GROUP, te[t] % EXPERTS_PER_GROUP, 0, 0))

    return pl.pallas_call(
        _moe_body,
        grid_spec=pltpu.PrefetchScalarGridSpec(
            num_scalar_prefetch=3,
            grid=(n_tiles,),
            in_specs=[
                pl.BlockSpec(memory_space=pl.ANY),
                pl.BlockSpec((MOE_TM, 1), lambda t, te, nu, tok: (t, 0)),
                w_spec(D, F), w_spec(D, F), w_spec(F, D),
            ],
            out_specs=pl.BlockSpec((MOE_TM, D), lambda t, te, nu, tok: (t, 0)),
            scratch_shapes=[
                pltpu.VMEM((2, MOE_TM, D), jnp.float32),
                pltpu.SemaphoreType.DMA((2,)),
                pltpu.VMEM((D, F), jnp.bfloat16), pltpu.VMEM((D, F), jnp.bfloat16),
                pltpu.VMEM((F, D), jnp.bfloat16),
            ],
        ),
        out_shape=jax.ShapeDtypeStruct((n_tiles * MOE_TM, D), jnp.float32),
        compiler_params=pltpu.CompilerParams(
            dimension_semantics=("arbitrary",),
            vmem_limit_bytes=VMEM_LIMIT_BYTES),
        name="moe_experts",
    )(tile_expert, n_used, row_token, x_all, row_weight, w_g, w_u, w_d)


def _combine_body(pos_ref, x_ref, g_ref, y_hbm, o_ref, buf, sem):
    i = pl.program_id(0)
    slot = lax.rem(i, 2)

    def row_copy(tile, slot_, r, j):
        row = pos_ref[(tile * COMBINE_TM + r) * TOPK_EXPERTS + j]
        return pltpu.make_async_copy(y_hbm.at[pl.ds(row, 1), :], buf.at[slot_, j, pl.ds(r, 1), :], sem.at[slot_])

    def gather(tile, slot_, start):
        def body(r, c):
            for j in range(TOPK_EXPERTS):
                cp = row_copy(tile, slot_, r, j)
                if start:
                    cp.start()
                else:
                    cp.wait()
            return c
        lax.fori_loop(0, COMBINE_TM, body, 0, unroll=8)

    @pl.when(i == 0)
    def _():
        gather(0, 0, True)

    @pl.when(i + 1 < pl.num_programs(0))
    def _():
        gather(i + 1, 1 - slot, True)

    gather(i, slot, False)
    moe = buf[slot, 0]
    for j in range(1, TOPK_EXPERTS):
        moe = moe + buf[slot, j]
    o_ref[...] = x_ref[...] + g_ref[...] * moe


def _moe_combine(x, gate, pos, y_sorted, rows_per_gate):
    N, D = x.shape
    tm = COMBINE_TM
    assert N % tm == 0 and (rows_per_gate % tm == 0 or tm % rows_per_gate == 0)
    if rows_per_gate >= tm:
        g_arr = gate.reshape(-1, 1, D)
        g_spec = pl.BlockSpec((None, 1, D), lambda i, pos: (i * tm // rows_per_gate, 0, 0))
    else:
        g_arr = jnp.repeat(gate, rows_per_gate, axis=0)
        g_spec = pl.BlockSpec((tm, D), lambda i, pos: (i, 0))
    return pl.pallas_call(
        _combine_body,
        grid_spec=pltpu.PrefetchScalarGridSpec(
            num_scalar_prefetch=1,
            grid=(N // tm,),
            in_specs=[pl.BlockSpec((tm, D), lambda i, pos: (i, 0)), g_spec, pl.BlockSpec(memory_space=pl.ANY)],
            out_specs=pl.BlockSpec((tm, D), lambda i, pos: (i, 0)),
            scratch_shapes=[pltpu.VMEM((2, TOPK_EXPERTS, tm, D), jnp.float32), pltpu.SemaphoreType.DMA((2,))],
        ),
        out_shape=jax.ShapeDtypeStruct((N, D), jnp.float32),
        compiler_params=pltpu.CompilerParams(
            dimension_semantics=("arbitrary",),
            vmem_limit_bytes=VMEM_LIMIT_BYTES),
        name="moe_combine",
    )(pos, x, g_arr, y_sorted)


def _block_forward(x, mod, pos, attend, conv_buf, lw):
    B, T, _ = x.shape
    sh1, sc1, g1, sh2, sc2, g2 = jnp.split(mod[:, None, :], 6, axis=-1)
    h = _rms_norm(x, lw['norm1_w']) * (1 + sc1) + sh1
    split_points = np.cumsum(IN_SPLITS)[:-1].tolist()
    proj = _mm(h.reshape(B * T, D_MODEL), lw['w_in']).reshape(B, T, -1)
    q, k, v, qi, ki, wi, xc, bg, cg, ga, gc = jnp.split(proj, split_points, axis=-1)
    q = _rotary(_rms_norm(q.reshape(B, T, N_HEADS, HEAD_DIM), lw['q_norm_w']), pos)
    k = _rotary(_rms_norm(k.reshape(B, T, N_HEADS, HEAD_DIM), lw['k_norm_w']), pos)
    v = v.reshape(B, T, N_HEADS, HEAD_DIM)
    qi = _rotary(qi.reshape(B, T, N_IDX_HEADS, IDX_DIM), pos)
    ki = _rotary(ki[:, :, None, :], pos)[:, :, 0]
    wi = wi * ((N_IDX_HEADS ** -0.5) * (IDX_DIM ** -0.5))
    attn = attend(q, k, v, qi, ki, wi)
    conv, new_buf = _short_conv(xc, bg, cg, lw['conv_w'], conv_buf)
    a_br = _mm(attn.reshape(B * T, ATTN_DIM), lw['w_branch_attn']).reshape(B, T, D_MODEL)
    c_br = _mm(conv.reshape(B * T, CONV_DIM), lw['w_branch_conv']).reshape(B, T, D_MODEL)
    merged = jax.nn.sigmoid(ga) * a_br + jax.nn.sigmoid(gc) * c_br
    x = x + g1 * _mm(merged.reshape(B * T, D_MODEL), lw['w_out']).reshape(B, T, D_MODEL)
    h2 = _rms_norm(x, lw['norm2_w']) * (1 + sc2) + sh2
    return x.reshape(B * T, D_MODEL), h2.reshape(B * T, D_MODEL), g2[:, 0, :], k, v, ki, new_buf


def kernel(x_prompt, x_sample, c_prompt, c_sample, cache_k, cache_v, cache_kidx, state_conv, page_table, norm1_w, norm2_w, w_ada, b_ada, w_in, q_norm_w, k_norm_w, conv_w, w_branch_attn, w_branch_conv, w_out, w_group_router, b_group_router, w_expert_router, b_expert_router, w_e_gate, w_e_up, w_e_down):
    B, T, _ = x_prompt.shape
    Bd, S, _ = x_sample.shape
    depth = norm1_w.shape[0]
    past = page_table.shape[1] * PAGE_SIZE
    pos_prompt = jnp.arange(T)
    pos_sample = past + jnp.arange(S)
    yp, ys = x_prompt, x_sample
    outs = [[] for _ in range(8)]
    c_all = jnp.concatenate([c_prompt, c_sample], axis=0)
    n_c = c_all.shape[0]
    c_pad = jnp.pad(jax.nn.silu(c_all), ((0, (-n_c) % 8), (0, 0)))
    for l in range(depth):
        w_router = jnp.pad(jnp.concatenate([w_group_router[l], w_expert_router[l].reshape(D_MODEL, -1)], axis=1),
                           ((0, 0), (0, 128 - N_GROUPS * (1 + EXPERTS_PER_GROUP))))
        lw = {
            'norm1_w': norm1_w[l], 'norm2_w': norm2_w[l],
            'w_in': w_in[l], 'q_norm_w': q_norm_w[l], 'k_norm_w': k_norm_w[l], 'conv_w': conv_w[l],
            'w_branch_attn': w_branch_attn[l], 'w_branch_conv': w_branch_conv[l], 'w_out': w_out[l],
            'w_router': w_router, 'b_group_router': b_group_router[l],
            'b_expert_router': b_expert_router[l],
            'w_e_gate': w_e_gate[l], 'w_e_up': w_e_up[l], 'w_e_down': w_e_down[l],
        }
        mod = _mm(c_pad, w_ada[l], tn=1024)[:n_c] + b_ada[l]
        zero_buf = jnp.zeros((B, CONV_WIDTH - 1, CONV_DIM), x_prompt.dtype)
        xp, hp, gp, kp, vp, kip, cp = _block_forward(yp, mod[:B], pos_prompt, _prompt_attention, zero_buf, lw)
        samp_attend = functools.partial(_sample_attention, layer=l, cache_k=cache_k, cache_v=cache_v,
                                        cache_kidx=cache_kidx, page_table=page_table)
        xs, hs, gs, ksm, vsm, kis, cs = _block_forward(ys, mod[B:], pos_sample, samp_attend, state_conv[l], lw)
        for lst, val in zip(outs, (kp, vp, kip, cp, ksm, vsm, kis, cs)):
            lst.append(val)
        h_all = jnp.concatenate([hp, hs], axis=0)
        n_p = hp.shape[0]
        eid, wgt = _route(_mm_split(h_all, w_router), b_group_router[l], b_expert_router[l])
        n_assign = eid.size
        n_tiles = (n_assign + N_GROUPS * EXPERTS_PER_GROUP * (MOE_TM - 1)) // MOE_TM
        plan = _moe_plan(eid, wgt, n_tiles)
        y_sorted = _moe_experts(h_all, plan, w_e_gate, w_e_up, w_e_down, l, n_tiles)
        pos = plan[4]
        yp = _moe_combine(xp, gp, pos[:n_p * TOPK_EXPERTS], y_sorted, T).reshape(B, T, D_MODEL)
        ys = _moe_combine(xs, gs, pos[n_p * TOPK_EXPERTS:], y_sorted, S).reshape(Bd, S, D_MODEL)
    return (yp, ys) + tuple(jnp.stack(o) for o in outs)
```

```python
import functools

import jax
import jax.numpy as jnp
import numpy as np
from jax import lax
from jax.experimental import pallas as pl
from jax.experimental.pallas import tpu as pltpu

D_MODEL = 2048
PAGE_SIZE = 128
N_HEADS = 8
HEAD_DIM = 128
ATTN_DIM = N_HEADS * HEAD_DIM
N_IDX_HEADS = 8
IDX_DIM = 64
TOPK_MAX = 256
CONV_DIM = D_MODEL // 2
CONV_WIDTH = 3
N_GROUPS = 4
EXPERTS_PER_GROUP = 8
TOPK_EXPERTS = 2
EXPERT_FF = 512
ROPE_THETA = 10000.0
EPS = 1e-6
Q_BLOCK = 128
IN_SPLITS = (ATTN_DIM, ATTN_DIM, ATTN_DIM, N_IDX_HEADS * IDX_DIM, IDX_DIM, N_IDX_HEADS,
             CONV_DIM, CONV_DIM, CONV_DIM, D_MODEL, D_MODEL)

VMEM_LIMIT_BYTES = 56 * 1024 * 1024
MM_TM = 2048


def _mm_body(a_ref, b_ref, o_ref):
    a = a_ref[...].astype(jnp.bfloat16)
    b = b_ref[...].astype(jnp.bfloat16)
    o_ref[...] = jnp.dot(a, b, preferred_element_type=jnp.float32).astype(o_ref.dtype)


def _mm(a, b, *, tm=512, tn=512, out_dtype=jnp.float32):
    M, K = a.shape
    _, N = b.shape
    tm = min(tm, M)
    tn = min(tn, N)
    return pl.pallas_call(
        _mm_body,
        grid=(pl.cdiv(M, tm), pl.cdiv(N, tn)),
        in_specs=[pl.BlockSpec((tm, K), lambda i, j: (i, 0)),
                  pl.BlockSpec((K, tn), lambda i, j: (0, j))],
        out_specs=pl.BlockSpec((tm, tn), lambda i, j: (i, j)),
        out_shape=jax.ShapeDtypeStruct((M, N), out_dtype),
        compiler_params=pltpu.CompilerParams(
            dimension_semantics=("arbitrary", "arbitrary"),
            vmem_limit_bytes=VMEM_LIMIT_BYTES),
        name="mm",
    )(a, b)


def _hi_lo(x):
    hi = x.astype(jnp.bfloat16)
    lo = (x - hi.astype(jnp.float32)).astype(jnp.bfloat16)
    return hi, lo


def _mm_split_body(a_ref, b_ref, o_ref):
    a_hi, a_lo = _hi_lo(a_ref[...])
    b_hi, b_lo = _hi_lo(b_ref[...])
    dot = functools.partial(jnp.dot, preferred_element_type=jnp.float32)
    o_ref[...] = dot(a_hi, b_hi) + (dot(a_lo, b_hi) + dot(a_hi, b_lo))


def _mm_split(a, b, *, tm=512):
    M, K = a.shape
    _, N = b.shape
    tm = min(tm, M)
    return pl.pallas_call(
        _mm_split_body,
        grid=(pl.cdiv(M, tm),),
        in_specs=[pl.BlockSpec((tm, K), lambda i: (i, 0)),
                  pl.BlockSpec((K, N), lambda i: (0, 0))],
        out_specs=pl.BlockSpec((tm, N), lambda i: (i, 0)),
        out_shape=jax.ShapeDtypeStruct((M, N), jnp.float32),
        compiler_params=pltpu.CompilerParams(
            dimension_semantics=("arbitrary",),
            vmem_limit_bytes=VMEM_LIMIT_BYTES),
        name="mm_split",
    )(a, b)


ATTN_TQ = 256
ATTN_SEG = 1024
ATTN_SUB = 256
ATTN_HEADS_PER_STEP = 1
MASK_NEG = -1e30
KEY_NEG_INF = -2139095041
INT32_MIN = -2 ** 31


def _slab_sum(x):
    parts = [x[r * 8:(r + 1) * 8] for r in range(x.shape[0] // 8)]
    while len(parts) > 1:
        parts = [parts[a] + parts[a + 1] for a in range(0, len(parts), 2)]
    return parts[0]


def _prompt_attn_body(qi_ref, ki_ref, wit_ref, q_ref, k_ref, vt_ref, o_ref, key_ref, bias_ref, tie_ref,
                      *, topk, idx_bits):
    tq = ATTN_TQ
    i = pl.program_id(1)
    q0 = i * tq
    nseg = (q0 + tq + ATTN_SEG - 1) // ATTN_SEG
    ntile = nseg * (ATTN_SEG // ATTN_SUB)
    qpos = q0 + lax.broadcasted_iota(jnp.int32, (1, tq), 1)
    nt_dims = (((1,), (1,)), ((), ()))

    def score_tile(t, c):
        r0 = pl.multiple_of(t * ATTN_SUB, ATTN_SUB)
        ki = ki_ref[pl.ds(r0, ATTN_SUB), :]
        acc = jnp.zeros((ATTN_SUB, tq), jnp.float32)
        for h in range(N_IDX_HEADS):
            s = lax.dot_general(ki, qi_ref[h], nt_dims, preferred_element_type=jnp.float32)
            acc = acc + jnp.maximum(s, 0.0) * wit_ref[h:h + 1, :]
        kpos = r0 + lax.broadcasted_iota(jnp.int32, (ATTN_SUB, 1), 0)
        bits = lax.bitcast_convert_type(acc, jnp.int32)
        skey = jnp.where(bits < 0, bits ^ jnp.int32(0x7FFFFFFF), bits)
        skey = jnp.where(acc == 0.0, 0, skey)
        key_ref[pl.ds(r0, ATTN_SUB), :] = jnp.where(kpos <= qpos, skey, KEY_NEG_INF)
        return c

    lax.fori_loop(0, ntile, score_tile, 0)

    def count(pred):
        def tile(t, cnt):
            r0 = pl.multiple_of(t * ATTN_SUB, ATTN_SUB)
            kk = key_ref[pl.ds(r0, ATTN_SUB), :]
            kpos = r0 + lax.broadcasted_iota(jnp.int32, (ATTN_SUB, 1), 0)
            return cnt + _slab_sum(pred(kk, kpos).astype(jnp.int32))
        cnt = lax.fori_loop(0, ntile, tile, jnp.zeros((8, tq), jnp.int32))
        return jnp.sum(cnt, axis=0, keepdims=True)

    def bit_pass(p, t_u):
        cand_u = t_u | lax.shift_left(jnp.int32(1), 31 - p)
        cand_s = cand_u ^ jnp.int32(INT32_MIN)
        c = count(lambda kk, kpos: kk >= cand_s)
        return jnp.where(c >= topk, cand_u, t_u)

    t_u = lax.fori_loop(0, 32, bit_pass, jnp.zeros((1, tq), jnp.int32))
    t_s = t_u ^ jnp.int32(INT32_MIN)

    need = topk - count(lambda kk, kpos: kk > t_s)
    surplus = (count(lambda kk, kpos: kk >= t_s) > topk) & (t_s > KEY_NEG_INF)
    tie_ref[...] = jnp.full((1, tq), 2 ** idx_bits, jnp.int32)

    @pl.when(jnp.max(surplus.astype(jnp.int32)) > 0)
    def _():
        def idx_pass(p, lo):
            cand = lo | lax.shift_left(jnp.int32(1), idx_bits - 1 - p)
            c = count(lambda kk, kpos: (kk == t_s) & (kpos < cand))
            return jnp.where(c < need, cand, lo)

        tie_ref[...] = lax.fori_loop(0, idx_bits, idx_pass, jnp.zeros((1, tq), jnp.int32))

    last_tie = tie_ref[...]

    def bias_tile(t, c):
        r0 = pl.multiple_of(t * ATTN_SUB, ATTN_SUB)
        kk = key_ref[pl.ds(r0, ATTN_SUB), :]
        kpos = r0 + lax.broadcasted_iota(jnp.int32, (ATTN_SUB, 1), 0)
        sel = (kk > t_s) | ((kk == t_s) & (kpos <= last_tie))
        sel = sel & (kk > KEY_NEG_INF)
        bias_ref[pl.ds(r0, ATTN_SUB), :] = jnp.where(sel, 0.0, MASK_NEG)
        return c

    lax.fori_loop(0, ntile, bias_tile, 0)

    scale = HEAD_DIM ** -0.5

    def head_group_body(hg, c):
        heads = [hg * ATTN_HEADS_PER_STEP + a for a in range(ATTN_HEADS_PER_STEP)]

        def seg_body(s, carry):
            r0 = pl.multiple_of(s * ATTN_SEG, ATTN_SEG)
            bias = bias_ref[pl.ds(r0, ATTN_SEG), :]
            out = []
            for h, (m, l, acc) in zip(heads, carry):
                kh = k_ref[h, pl.ds(r0, ATTN_SEG), :]
                logits = lax.dot_general(kh, q_ref[h], nt_dims, preferred_element_type=jnp.float32) * scale
                logits = logits + bias
                m_new = jnp.maximum(m, jnp.max(logits, axis=0, keepdims=True))
                alpha = jnp.exp(m - m_new)
                p = jnp.exp(logits - m_new)
                l = alpha * l + jnp.sum(p, axis=0, keepdims=True)
                acc = alpha * acc + jnp.dot(vt_ref[h, s], p.astype(jnp.bfloat16),
                                            preferred_element_type=jnp.float32)
                out.append((m_new, l, acc))
            return tuple(out)

        init = tuple((jnp.full((1, tq), MASK_NEG, jnp.float32), jnp.zeros((1, tq), jnp.float32),
                      jnp.zeros((HEAD_DIM, tq), jnp.float32)) for _ in heads)
        for h, (_, l, acc) in zip(heads, lax.fori_loop(0, nseg, seg_body, init)):
            o_ref[h] = (acc / l).astype(o_ref.dtype)
        return c

    lax.fori_loop(0, N_HEADS // ATTN_HEADS_PER_STEP, head_group_body, 0)


def _prompt_attention(q, k, v, qi, ki, wi):
    B, T = q.shape[:2]
    topk = min(TOPK_MAX, T // 4)
    nseg = T // ATTN_SEG
    bf = jnp.bfloat16
    q_t = q.astype(bf).transpose(0, 2, 1, 3)
    k_t = k.astype(bf).transpose(0, 2, 1, 3)
    vt = v.astype(bf).reshape(B, nseg, ATTN_SEG, N_HEADS, HEAD_DIM).transpose(0, 3, 1, 4, 2)
    qi_t = qi.astype(bf).transpose(0, 2, 1, 3)
    ki_b = ki.astype(bf)
    wit = wi.transpose(0, 2, 1)
    body = functools.partial(_prompt_attn_body, topk=topk, idx_bits=int(np.ceil(np.log2(T))))
    out = pl.pallas_call(
        body,
        grid=(B, T // ATTN_TQ),
        in_specs=[
            pl.BlockSpec((None, N_IDX_HEADS, ATTN_TQ, IDX_DIM), lambda b, i: (b, 0, i, 0)),
            pl.BlockSpec((None, T, IDX_DIM), lambda b, i: (b, 0, 0)),
            pl.BlockSpec((None, N_IDX_HEADS, ATTN_TQ), lambda b, i: (b, 0, i)),
            pl.BlockSpec((None, N_HEADS, ATTN_TQ, HEAD_DIM), lambda b, i: (b, 0, i, 0)),
            pl.BlockSpec((None, N_HEADS, T, HEAD_DIM), lambda b, i: (b, 0, 0, 0)),
            pl.BlockSpec((None, N_HEADS, nseg, HEAD_DIM, ATTN_SEG), lambda b, i: (b, 0, 0, 0, 0)),
        ],
        out_specs=pl.BlockSpec((None, N_HEADS, HEAD_DIM, ATTN_TQ), lambda b, i: (b, 0, 0, i)),
        out_shape=jax.ShapeDtypeStruct((B, N_HEADS, HEAD_DIM, T), bf),
        scratch_shapes=[pltpu.VMEM((T, ATTN_TQ), jnp.int32), pltpu.VMEM((T, ATTN_TQ), jnp.float32),
                        pltpu.VMEM((1, ATTN_TQ), jnp.int32)],
        compiler_params=pltpu.CompilerParams(
            dimension_semantics=("arbitrary", "arbitrary"),
            vmem_limit_bytes=VMEM_LIMIT_BYTES),
        name="prompt_attn",
    )(qi_t, ki_b, wit, q_t, k_t, vt)
    return out.transpose(0, 3, 1, 2)


SAMPLE_IDX_PAGES = 8
SAMPLE_KV_PAGES = 4


def _score_key(score):
    bits = lax.bitcast_convert_type(score, jnp.int32)
    key = jnp.where(bits < 0, bits ^ jnp.int32(0x7FFFFFFF), bits)
    return jnp.where(score == 0.0, 0, key)


def _sample_select_body(pt_ref, qi_ref, w_ref, kin_ref, *rest, topk, n_pages, idx_bits):
    del pt_ref
    kid_refs = rest[:SAMPLE_IDX_PAGES]
    sel_ref, key_scr, tie_scr = rest[SAMPLE_IDX_PAGES:]
    g = pl.program_id(1)
    n_q = key_scr.shape[1]
    past = n_pages * PAGE_SIZE
    qi = qi_ref[...]
    wcol = w_ref[...]
    qpos = past + lax.broadcasted_iota(jnp.int32, (n_q, 1), 0)
    lane = lax.broadcasted_iota(jnp.int32, (1, PAGE_SIZE), 1)

    def page_keys(ki_page, page):
        s = lax.dot_general(qi, ki_page.astype(jnp.bfloat16), (((1,), (1,)), ((), ())),
                            preferred_element_type=jnp.float32)
        score = _slab_sum_n(jnp.maximum(s, 0.0) * wcol, n_q)
        kpos = page * PAGE_SIZE + lane
        return jnp.where(kpos <= qpos, _score_key(score), KEY_NEG_INF)

    for j in range(SAMPLE_IDX_PAGES):
        page = g * SAMPLE_IDX_PAGES + j
        key_scr[page] = page_keys(kid_refs[j][...], page)

    @pl.when(g == 0)
    def _():
        key_scr[n_pages] = page_keys(kin_ref[...], n_pages)

    @pl.when(g == pl.num_programs(1) - 1)
    def _():
        kk = key_scr[...]
        kpos = (lax.broadcasted_iota(jnp.int32, kk.shape, 0) * PAGE_SIZE
                + lax.broadcasted_iota(jnp.int32, kk.shape, 2))

        def count(m):
            return jnp.sum(jnp.sum(m.astype(jnp.int32), axis=0), axis=1, keepdims=True)

        def bit_pass(p, t_u):
            cand_u = t_u | lax.shift_left(jnp.int32(1), 31 - p)
            cand_s = cand_u ^ jnp.int32(INT32_MIN)
            return jnp.where(count(kk >= cand_s[None]) >= topk, cand_u, t_u)

        t_u = lax.fori_loop(0, 32, bit_pass, jnp.zeros((n_q, 1), jnp.int32))
        t_s = (t_u ^ jnp.int32(INT32_MIN))[None]
        need = topk - count(kk > t_s)
        surplus = (count(kk >= t_s) > topk) & (t_s[0] > KEY_NEG_INF)
        tie_scr[...] = jnp.full((n_q, 1), 2 ** idx_bits, jnp.int32)

        @pl.when(jnp.max(surplus.astype(jnp.int32)) > 0)
        def _():
            def idx_pass(p, lo):
                cand = lo | lax.shift_left(jnp.int32(1), idx_bits - 1 - p)
                return jnp.where(count((kk == t_s) & (kpos < cand[None])) < need, cand, lo)

            tie_scr[...] = lax.fori_loop(0, idx_bits, idx_pass, jnp.zeros((n_q, 1), jnp.int32))

        last_tie = tie_scr[...]
        sel = (kk > t_s) | ((kk == t_s) & (kpos <= last_tie[None]))
        sel = sel & (kk > KEY_NEG_INF)
        sel_ref[...] = sel.astype(jnp.float32)


def _slab_sum_n(x, n):
    parts = [x[r * n:(r + 1) * n] for r in range(x.shape[0] // n)]
    while len(parts) > 1:
        parts = [parts[a] + parts[a + 1] for a in range(0, len(parts), 2)]
    return parts[0]


def _sample_attn_body(pt_ref, q_ref, sel_ref, seln_ref, exp_ref, hm_ref, kn_ref, vn_ref, *rest):
    del pt_ref
    k_refs = rest[:SAMPLE_KV_PAGES]
    v_refs = rest[SAMPLE_KV_PAGES:2 * SAMPLE_KV_PAGES]
    o_ref, m_scr, l_scr, acc_scr = rest[2 * SAMPLE_KV_PAGES:]
    g = pl.program_id(1)
    n_q = seln_ref.shape[1]
    bf = jnp.bfloat16
    qall = q_ref[...]
    scale = HEAD_DIM ** -0.5

    @pl.when(g == 0)
    def _():
        m_scr[...] = jnp.full(m_scr.shape, MASK_NEG, jnp.float32)
        l_scr[...] = jnp.zeros(l_scr.shape, jnp.float32)
        acc_scr[...] = jnp.zeros(acc_scr.shape, jnp.float32)

    def pages_update(pages):
        logits = []
        for k_page, _, sel_tile, ncol in pages:
            kp = k_page.reshape(-1, HEAD_DIM).astype(bf)
            sel_cols = jnp.dot(sel_tile.astype(bf), exp_ref[:, :ncol], preferred_element_type=jnp.float32)
            sel_bias = (sel_cols - 1.0) * (-MASK_NEG)
            bias = jnp.concatenate([sel_bias] * N_HEADS, axis=0) + hm_ref[:, :ncol]
            logits.append(lax.dot_general(qall, kp, (((1,), (1,)), ((), ())),
                                          preferred_element_type=jnp.float32) * scale + bias)
        m = m_scr[...]
        m_new = m
        for lg in logits:
            m_new = jnp.maximum(m_new, jnp.max(lg, axis=1, keepdims=True))
        alpha = jnp.exp(m - m_new)
        l_new = alpha * l_scr[...]
        acc = alpha * acc_scr[...]
        for lg, (_, v_page, _, _) in zip(logits, pages):
            p = jnp.exp(lg - m_new)
            l_new = l_new + jnp.sum(p, axis=1, keepdims=True)
            acc = acc + jnp.dot(p.astype(bf), v_page.reshape(-1, HEAD_DIM).astype(bf),
                                preferred_element_type=jnp.float32)
        l_scr[...] = l_new
        acc_scr[...] = acc
        m_scr[...] = m_new

    @pl.when(g == 0)
    def _():
        pages_update([(kn_ref[...], vn_ref[...], seln_ref[0], n_q * N_HEADS)])

    pages_update([(k_refs[j][...], v_refs[j][...], sel_ref[j], PAGE_SIZE * N_HEADS)
                  for j in range(SAMPLE_KV_PAGES)])

    @pl.when(g == pl.num_programs(1) - 1)
    def _():
        o_ref[...] = acc_scr[...] / l_scr[...]


def _sample_attention(q, k, v, qi, ki, wi, *, layer, cache_k, cache_v, cache_kidx, page_table):
    Bd, S = q.shape[:2]
    n_pages = page_table.shape[1]
    past = n_pages * PAGE_SIZE
    topk = min(TOPK_MAX, (past + S) // 4)
    bf = jnp.bfloat16
    rows = N_HEADS * S
    qi_r = qi.astype(bf).transpose(0, 2, 1, 3).reshape(Bd, N_IDX_HEADS * S, IDX_DIM)
    w_r = jnp.broadcast_to(wi.transpose(0, 2, 1).reshape(Bd, N_IDX_HEADS * S, 1),
                           (Bd, N_IDX_HEADS * S, PAGE_SIZE))
    ki_new = jnp.pad(ki, ((0, 0), (0, PAGE_SIZE - S), (0, 0)))
    idx_bits = int(np.ceil(np.log2(past + PAGE_SIZE)))

    def kid_spec(j):
        return pl.BlockSpec((None, None, PAGE_SIZE, IDX_DIM),
                            lambda b, g, pt: (layer, pt[b, g * SAMPLE_IDX_PAGES + j], 0, 0))

    sel = pl.pallas_call(
        functools.partial(_sample_select_body, topk=topk, n_pages=n_pages, idx_bits=idx_bits),
        grid_spec=pltpu.PrefetchScalarGridSpec(
            num_scalar_prefetch=1,
            grid=(Bd, n_pages // SAMPLE_IDX_PAGES),
            in_specs=[
                pl.BlockSpec((None, N_IDX_HEADS * S, IDX_DIM), lambda b, g, pt: (b, 0, 0)),
                pl.BlockSpec((None, N_IDX_HEADS * S, PAGE_SIZE), lambda b, g, pt: (b, 0, 0)),
                pl.BlockSpec((None, PAGE_SIZE, IDX_DIM), lambda b, g, pt: (b, 0, 0)),
            ] + [kid_spec(j) for j in range(SAMPLE_IDX_PAGES)],
            out_specs=pl.BlockSpec((None, n_pages + 1, S, PAGE_SIZE), lambda b, g, pt: (b, 0, 0, 0)),
            scratch_shapes=[pltpu.VMEM((n_pages + 1, S, PAGE_SIZE), jnp.int32), pltpu.VMEM((S, 1), jnp.int32)],
        ),
        out_shape=jax.ShapeDtypeStruct((Bd, n_pages + 1, S, PAGE_SIZE), jnp.float32),
        compiler_params=pltpu.CompilerParams(
            dimension_semantics=("arbitrary", "arbitrary"),
            vmem_limit_bytes=VMEM_LIMIT_BYTES),
        name="sample_select",
    )(page_table, qi_r, w_r, ki_new, *([cache_kidx] * SAMPLE_IDX_PAGES))

    q_r = q.astype(bf).transpose(0, 2, 1, 3).reshape(Bd, rows, HEAD_DIM)
    ncol = PAGE_SIZE * N_HEADS
    col = np.arange(ncol)
    expand = jnp.asarray(col[None, :] // N_HEADS == np.arange(PAGE_SIZE)[:, None], bf)
    head_mask = jnp.asarray(np.where(col[None, :] % N_HEADS == np.arange(rows)[:, None] // S, 0.0, MASK_NEG),
                            jnp.float32)

    def kv_spec(j):
        return pl.BlockSpec((None, None, PAGE_SIZE, N_HEADS, HEAD_DIM),
                            lambda b, g, pt: (layer, pt[b, g * SAMPLE_KV_PAGES + j], 0, 0, 0))

    out = pl.pallas_call(
        _sample_attn_body,
        grid_spec=pltpu.PrefetchScalarGridSpec(
            num_scalar_prefetch=1,
            grid=(Bd, n_pages // SAMPLE_KV_PAGES),
            in_specs=[
                pl.BlockSpec((None, rows, HEAD_DIM), lambda b, g, pt: (b, 0, 0)),
                pl.BlockSpec((None, SAMPLE_KV_PAGES, S, PAGE_SIZE), lambda b, g, pt: (b, g, 0, 0)),
                pl.BlockSpec((None, 1, S, PAGE_SIZE), lambda b, g, pt: (b, n_pages, 0, 0)),
                pl.BlockSpec((PAGE_SIZE, ncol), lambda b, g, pt: (0, 0)),
                pl.BlockSpec((rows, ncol), lambda b, g, pt: (0, 0)),
                pl.BlockSpec((None, S, N_HEADS, HEAD_DIM), lambda b, g, pt: (b, 0, 0, 0)),
                pl.BlockSpec((None, S, N_HEADS, HEAD_DIM), lambda b, g, pt: (b, 0, 0, 0)),
            ] + [kv_spec(j) for j in range(SAMPLE_KV_PAGES)] * 2,
            out_specs=pl.BlockSpec((None, rows, HEAD_DIM), lambda b, g, pt: (b, 0, 0)),
            scratch_shapes=[pltpu.VMEM((rows, 1), jnp.float32), pltpu.VMEM((rows, 1), jnp.float32),
                            pltpu.VMEM((rows, HEAD_DIM), jnp.float32)],
        ),
        out_shape=jax.ShapeDtypeStruct((Bd, rows, HEAD_DIM), jnp.float32),
        compiler_params=pltpu.CompilerParams(
            dimension_semantics=("arbitrary", "arbitrary"),
            vmem_limit_bytes=VMEM_LIMIT_BYTES),
        name="sample_attn",
    )(page_table, q_r, sel, sel, expand, head_mask, k, v,
      *([cache_k] * SAMPLE_KV_PAGES), *([cache_v] * SAMPLE_KV_PAGES))
    return out.reshape(Bd, N_HEADS, S, HEAD_DIM).transpose(0, 2, 1, 3)


def _rms_norm(x, w):
    xf = x.astype(jnp.float32)
    y = xf * lax.rsqrt(jnp.mean(xf * xf, axis=-1, keepdims=True) + EPS)
    return y * w.astype(jnp.float32)


def _rotary(x, pos):
    half = x.shape[-1] // 2
    inv = 1.0 / jnp.power(ROPE_THETA, jnp.arange(half, dtype=jnp.float32) / half)
    ang = pos.astype(jnp.float32)[:, None] * inv[None, :]
    cos = jnp.cos(ang)[None, :, None, :]
    sin = jnp.sin(ang)[None, :, None, :]
    x1 = x[..., :half]
    x2 = x[..., half:]
    return jnp.concatenate([x1 * cos - x2 * sin, x2 * cos + x1 * sin], axis=-1)


def _short_conv(xc, bg, cg, conv_w, buf):
    u = cg * xc
    ext = jnp.concatenate([buf, u], axis=1)
    T = u.shape[1]
    y = sum(conv_w[j] * ext[:, j:j + T] for j in range(CONV_WIDTH))
    return bg * y, ext[:, -(CONV_WIDTH - 1):]


MOE_TM = 256
COMBINE_TM = 128


def _route(r_logits, b_gr, b_er):
    n = r_logits.shape[0]
    g_logits = r_logits[:, :N_GROUPS]
    g_prob = jax.nn.softmax(g_logits, axis=-1)
    g_sel = jnp.argmax(g_logits + b_gr, axis=-1)
    g_w = jnp.take_along_axis(g_prob, g_sel[:, None], axis=1)[:, 0]
    e_logits = r_logits[:, N_GROUPS:N_GROUPS * (1 + EXPERTS_PER_GROUP)].reshape(n, N_GROUPS, EXPERTS_PER_GROUP)
    e_in = jnp.take_along_axis(e_logits, g_sel[:, None, None], axis=1)[:, 0]
    _, e_idx = lax.top_k(e_in + b_er[g_sel], TOPK_EXPERTS)
    e_p = jax.nn.softmax(jnp.take_along_axis(e_in, e_idx, axis=1), axis=-1)
    eid = (g_sel[:, None] * EXPERTS_PER_GROUP + e_idx).astype(jnp.int32)
    return eid, g_w[:, None] * e_p


def _moe_plan(eid, wgt, n_tiles):
    n_exp = N_GROUPS * EXPERTS_PER_GROUP
    a = eid.size
    e_flat = eid.reshape(a)
    onehot = (e_flat[:, None] == jnp.arange(n_exp, dtype=jnp.int32)[None, :]).astype(jnp.int32)
    running = jnp.cumsum(onehot, axis=0)
    counts = running[-1]
    padded = (counts + MOE_TM - 1) // MOE_TM * MOE_TM
    pad_end = jnp.cumsum(padded)
    pad_start = pad_end - padded
    cnt_start = jnp.cumsum(counts) - counts
    pos = pad_start[e_flat] + jnp.sum(running * onehot, axis=1) - 1
    order = jnp.argsort(e_flat, stable=True).astype(jnp.int32)
    rows = n_tiles * MOE_TM
    row = jnp.arange(rows, dtype=jnp.int32)
    tile_expert = jnp.minimum(jnp.searchsorted(pad_end, jnp.arange(n_tiles, dtype=jnp.int32) * MOE_TM, side='right'),
                              n_exp - 1).astype(jnp.int32)
    row_expert = jnp.repeat(tile_expert, MOE_TM)
    rank = row - pad_start[row_expert]
    row_valid = (rank < counts[row_expert]) & (row < pad_end[-1])
    assign = order[jnp.clip(cnt_start[row_expert] + rank, 0, a - 1)]
    row_token = jnp.where(row_valid, assign // TOPK_EXPERTS, 0)
    row_weight = jnp.where(row_valid, wgt.reshape(a)[assign], 0.0)
    n_used = (pad_end[-1] // MOE_TM).astype(jnp.int32).reshape(1)
    return tile_expert, n_used, row_token, row_weight.reshape(rows, 1), pos.astype(jnp.int32)


def _moe_body(te_ref, nu_ref, tok_ref, x_hbm, rw_ref, wg_ref, wu_ref, wd_ref, y_ref,
              buf, sem, wgb, wub, wdb):
    t = pl.program_id(0)
    n_used = nu_ref[0]
    slot = lax.rem(t, 2)
    bf = jnp.bfloat16

    def row_copy(tile, slot_, r):
        tok = tok_ref[tile * MOE_TM + r]
        return pltpu.make_async_copy(x_hbm.at[pl.ds(tok, 1), :], buf.at[slot_, pl.ds(r, 1), :], sem.at[slot_])

    def gather_start(tile, slot_):
        def body(r, c):
            row_copy(tile, slot_, r).start()
            return c
        lax.fori_loop(0, MOE_TM, body, 0, unroll=8)

    def gather_wait(tile, slot_):
        def body(r, c):
            row_copy(tile, slot_, r).wait()
            return c
        lax.fori_loop(0, MOE_TM, body, 0, unroll=8)

    @pl.when(t == 0)
    def _():
        gather_start(0, 0)

    @pl.when(t + 1 < n_used)
    def _():
        gather_start(t + 1, 1 - slot)

    @pl.when(t < n_used)
    def _():
        expert_changed = jnp.logical_or(t == 0, te_ref[t] != te_ref[jnp.maximum(t - 1, 0)])

        @pl.when(expert_changed)
        def _():
            wgb[...] = wg_ref[...].astype(bf)
            wub[...] = wu_ref[...].astype(bf)
            wdb[...] = wd_ref[...].astype(bf)

        gather_wait(t, slot)
        x = buf[slot].astype(bf)
        g = jnp.dot(x, wgb[...], preferred_element_type=jnp.float32)
        u = jnp.dot(x, wub[...], preferred_element_type=jnp.float32)
        hid = jax.nn.silu(g) * u * rw_ref[...]
        y_ref[...] = jnp.dot(hid.astype(bf), wdb[...], preferred_element_type=jnp.float32)

    @pl.when(t >= n_used)
    def _():
        y_ref[...] = jnp.zeros_like(y_ref)


def _moe_experts(x_all, plan, w_g, w_u, w_d, layer, n_tiles):
    tile_expert, n_used, row_token, row_weight, _ = plan
    D = x_all.shape[1]
    F = w_g.shape[-1]

    def w_spec(k_dim, n_dim):
        return pl.BlockSpec((None, None, None, k_dim, n_dim),
                            lambda t, te, nu, tok: (layer, te[t] // EXPERTS_PER_GROUP, te[t] % EXPERTS_PER_GROUP, 0, 0))

    return pl.pallas_call(
        _moe_body,
        grid_spec=pltpu.PrefetchScalarGridSpec(
            num_scalar_prefetch=3,
            grid=(n_tiles,),
            in_specs=[
                pl.BlockSpec(memory_space=pl.ANY),
                pl.BlockSpec((MOE_TM, 1), lambda t, te, nu, tok: (t, 0)),
                w_spec(D, F), w_spec(D, F), w_spec(F, D),
            ],
            out_specs=pl.BlockSpec((MOE_TM, D), lambda t, te, nu, tok: (t, 0)),
            scratch_shapes=[
                pltpu.VMEM((2, MOE_TM, D), jnp.float32),
                pltpu.SemaphoreType.DMA((2,)),
                pltpu.VMEM((D, F), jnp.bfloat16), pltpu.VMEM((D, F), jnp.bfloat16),
                pltpu.VMEM((F, D), jnp.bfloat16),
            ],
        ),
        out_shape=jax.ShapeDtypeStruct((n_tiles * MOE_TM, D), jnp.float32),
        compiler_params=pltpu.CompilerParams(
            dimension_semantics=("arbitrary",),
            vmem_limit_bytes=VMEM_LIMIT_BYTES),
        name="moe_experts",
    )(tile_expert, n_used, row_token, x_all, row_weight, w_g, w_u, w_d)


def _combine_body(pos_ref, x_ref, g_ref, y_hbm, o_ref, buf, sem):
    i = pl.program_id(0)
    slot = lax.rem(i, 2)

    def row_copy(tile, slot_, r, j):
        row = pos_ref[(tile * COMBINE_TM + r) * TOPK_EXPERTS + j]
        return pltpu.make_async_copy(y_hbm.at[pl.ds(row, 1), :], buf.at[slot_, j, pl.ds(r, 1), :], sem.at[slot_])

    def gather(tile, slot_, start):
        def body(r, c):
            for j in range(TOPK_EXPERTS):
                cp = row_copy(tile, slot_, r, j)
                if start:
                    cp.start()
                else:
                    cp.wait()
            return c
        lax.fori_loop(0, COMBINE_TM, body, 0, unroll=8)

    @pl.when(i == 0)
    def _():
        gather(0, 0, True)

    @pl.when(i + 1 < pl.num_programs(0))
    def _():
        gather(i + 1, 1 - slot, True)

    gather(i, slot, False)
    moe = buf[slot, 0]
    for j in range(1, TOPK_EXPERTS):
        moe = moe + buf[slot, j]
    o_ref[...] = x_ref[...] + g_ref[...] * moe


def _moe_combine(x, gate, pos, y_sorted, rows_per_gate):
    N, D = x.shape
    tm = COMBINE_TM
    assert N % tm == 0 and (rows_per_gate % tm == 0 or tm % rows_per_gate == 0)
    if rows_per_gate >= tm:
        g_arr = gate.reshape(-1, 1, D)
        g_spec = pl.BlockSpec((None, 1, D), lambda i, pos: (i * tm // rows_per_gate, 0, 0))
    else:
        g_arr = jnp.repeat(gate, rows_per_gate, axis=0)
        g_spec = pl.BlockSpec((tm, D), lambda i, pos: (i, 0))
    return pl.pallas_call(
        _combine_body,
        grid_spec=pltpu.PrefetchScalarGridSpec(
            num_scalar_prefetch=1,
            grid=(N // tm,),
            in_specs=[pl.BlockSpec((tm, D), lambda i, pos: (i, 0)), g_spec, pl.BlockSpec(memory_space=pl.ANY)],
            out_specs=pl.BlockSpec((tm, D), lambda i, pos: (i, 0)),
            scratch_shapes=[pltpu.VMEM((2, TOPK_EXPERTS, tm, D), jnp.float32), pltpu.SemaphoreType.DMA((2,))],
        ),
        out_shape=jax.ShapeDtypeStruct((N, D), jnp.float32),
        compiler_params=pltpu.CompilerParams(
            dimension_semantics=("arbitrary",),
            vmem_limit_bytes=VMEM_LIMIT_BYTES),
        name="moe_combine",
    )(pos, x, g_arr, y_sorted)


def _block_forward(x, mod, pos, attend, conv_buf, lw):
    B, T, _ = x.shape
    sh1, sc1, g1, sh2, sc2, g2 = jnp.split(mod[:, None, :], 6, axis=-1)
    bf = jnp.bfloat16
    h = (_rms_norm(x, lw['norm1_w']) * (1 + sc1) + sh1).astype(bf)
    split_points = np.cumsum(IN_SPLITS)[:-1].tolist()
    proj = _mm(h.reshape(B * T, D_MODEL), lw['w_in'], tm=MM_TM).reshape(B, T, -1)
    q, k, v, qi, ki, wi, xc, bg, cg, ga, gc = jnp.split(proj, split_points, axis=-1)
    q = _rotary(_rms_norm(q.reshape(B, T, N_HEADS, HEAD_DIM), lw['q_norm_w']), pos)
    k = _rotary(_rms_norm(k.reshape(B, T, N_HEADS, HEAD_DIM), lw['k_norm_w']), pos)
    v = v.reshape(B, T, N_HEADS, HEAD_DIM)
    qi = _rotary(qi.reshape(B, T, N_IDX_HEADS, IDX_DIM), pos)
    ki = _rotary(ki[:, :, None, :], pos)[:, :, 0]
    wi = wi * ((N_IDX_HEADS ** -0.5) * (IDX_DIM ** -0.5))
    attn = attend(q, k, v, qi, ki, wi)
    conv, new_buf = _short_conv(xc, bg, cg, lw['conv_w'], conv_buf)
    a_br = _mm(attn.astype(bf).reshape(B * T, ATTN_DIM), lw['w_branch_attn'], tm=MM_TM).reshape(B, T, D_MODEL)
    c_br = _mm(conv.astype(bf).reshape(B * T, CONV_DIM), lw['w_branch_conv'], tm=MM_TM).reshape(B, T, D_MODEL)
    merged = (jax.nn.sigmoid(ga) * a_br + jax.nn.sigmoid(gc) * c_br).astype(bf)
    x = x + g1 * _mm(merged.reshape(B * T, D_MODEL), lw['w_out'], tm=MM_TM).reshape(B, T, D_MODEL)
    h2 = _rms_norm(x, lw['norm2_w']) * (1 + sc2) + sh2
    return x.reshape(B * T, D_MODEL), h2.reshape(B * T, D_MODEL), g2[:, 0, :], k, v, ki, new_buf


def kernel(x_prompt, x_sample, c_prompt, c_sample, cache_k, cache_v, cache_kidx, state_conv, page_table, norm1_w, norm2_w, w_ada, b_ada, w_in, q_norm_w, k_norm_w, conv_w, w_branch_attn, w_branch_conv, w_out, w_group_router, b_group_router, w_expert_router, b_expert_router, w_e_gate, w_e_up, w_e_down):
    B, T, _ = x_prompt.shape
    Bd, S, _ = x_sample.shape
    depth = norm1_w.shape[0]
    past = page_table.shape[1] * PAGE_SIZE
    pos_prompt = jnp.arange(T)
    pos_sample = past + jnp.arange(S)
    yp, ys = x_prompt, x_sample
    outs = [[] for _ in range(8)]
    c_all = jnp.concatenate([c_prompt, c_sample], axis=0)
    n_c = c_all.shape[0]
    c_pad = jnp.pad(jax.nn.silu(c_all), ((0, (-n_c) % 8), (0, 0)))
    for l in range(depth):
        w_router = jnp.pad(jnp.concatenate([w_group_router[l], w_expert_router[l].reshape(D_MODEL, -1)], axis=1),
                           ((0, 0), (0, 128 - N_GROUPS * (1 + EXPERTS_PER_GROUP))))
        lw = {
            'norm1_w': norm1_w[l], 'norm2_w': norm2_w[l],
            'w_in': w_in[l], 'q_norm_w': q_norm_w[l], 'k_norm_w': k_norm_w[l], 'conv_w': conv_w[l],
            'w_branch_attn': w_branch_attn[l], 'w_branch_conv': w_branch_conv[l], 'w_out': w_out[l],
            'w_router': w_router, 'b_group_router': b_group_router[l],
            'b_expert_router': b_expert_router[l],
            'w_e_gate': w_e_gate[l], 'w_e_up': w_e_up[l], 'w_e_down': w_e_down[l],
        }
        mod = _mm(c_pad, w_ada[l], tn=1024)[:n_c] + b_ada[l]
        zero_buf = jnp.zeros((B, CONV_WIDTH - 1, CONV_DIM), x_prompt.dtype)
        xp, hp, gp, kp, vp, kip, cp = _block_forward(yp, mod[:B], pos_prompt, _prompt_attention, zero_buf, lw)
        samp_attend = functools.partial(_sample_attention, layer=l, cache_k=cache_k, cache_v=cache_v,
                                        cache_kidx=cache_kidx, page_table=page_table)
        xs, hs, gs, ksm, vsm, kis, cs = _block_forward(ys, mod[B:], pos_sample, samp_attend, state_conv[l], lw)
        for lst, val in zip(outs, (kp, vp, kip, cp, ksm, vsm, kis, cs)):
            lst.append(val)
        h_all = jnp.concatenate([hp, hs], axis=0)
        n_p = hp.shape[0]
        eid, wgt = _route(_mm_split(h_all, w_router), b_group_router[l], b_expert_router[l])
        n_assign = eid.size
        n_tiles = (n_assign + N_GROUPS * EXPERTS_PER_GROUP * (MOE_TM - 1)) // MOE_TM
        plan = _moe_plan(eid, wgt, n_tiles)
        y_sorted = _moe_experts(h_all, plan, w_e_gate, w_e_up, w_e_down, l, n_tiles)
        pos = plan[4]
        yp = _moe_combine(xp, gp, pos[:n_p * TOPK_EXPERTS], y_sorted, T).reshape(B, T, D_MODEL)
        ys = _moe_combine(xs, gs, pos[n_p * TOPK_EXPERTS:], y_sorted, S).reshape(Bd, S, D_MODEL)
    return (yp, ys) + tuple(jnp.stack(o) for o in outs)
```

```python
import functools

import jax
import jax.numpy as jnp
import numpy as np
from jax import lax
from jax.experimental import pallas as pl
from jax.experimental.pallas import tpu as pltpu

D_MODEL = 2048
PAGE_SIZE = 128
N_HEADS = 8
HEAD_DIM = 128
ATTN_DIM = N_HEADS * HEAD_DIM
N_IDX_HEADS = 8
IDX_DIM = 64
TOPK_MAX = 256
CONV_DIM = D_MODEL // 2
CONV_WIDTH = 3
N_GROUPS = 4
EXPERTS_PER_GROUP = 8
TOPK_EXPERTS = 2
EXPERT_FF = 512
ROPE_THETA = 10000.0
EPS = 1e-6

VMEM_LIMIT_BYTES = 56 * 1024 * 1024


def _mm_body(a_ref, b_ref, o_ref):
    a = a_ref[...].astype(jnp.bfloat16)
    b = b_ref[...].astype(jnp.bfloat16)
    o_ref[...] = jnp.dot(a, b, preferred_element_type=jnp.float32).astype(o_ref.dtype)


def _mm(a, b, *, tm=512, tn=512, out_dtype=jnp.float32):
    M, K = a.shape
    _, N = b.shape
    tm = min(tm, M)
    tn = min(tn, N)
    return pl.pallas_call(
        _mm_body,
        grid=(pl.cdiv(M, tm), pl.cdiv(N, tn)),
        in_specs=[pl.BlockSpec((tm, K), lambda i, j: (i, 0)),
                  pl.BlockSpec((K, tn), lambda i, j: (0, j))],
        out_specs=pl.BlockSpec((tm, tn), lambda i, j: (i, j)),
        out_shape=jax.ShapeDtypeStruct((M, N), out_dtype),
        compiler_params=pltpu.CompilerParams(
            dimension_semantics=("arbitrary", "arbitrary"),
            vmem_limit_bytes=VMEM_LIMIT_BYTES),
        name="mm",
    )(a, b)


def _hi_lo(x):
    hi = x.astype(jnp.bfloat16)
    lo = (x - hi.astype(jnp.float32)).astype(jnp.bfloat16)
    return hi, lo


ATTN_TQ = 256
ATTN_SEG = 1024
ATTN_SUB = 256
ATTN_HEADS_PER_STEP = 1
MASK_NEG = -1e30
KEY_NEG_INF = -2139095041
INT32_MIN = -2 ** 31


def _slab_sum(x):
    parts = [x[r * 8:(r + 1) * 8] for r in range(x.shape[0] // 8)]
    while len(parts) > 1:
        parts = [parts[a] + parts[a + 1] for a in range(0, len(parts), 2)]
    return parts[0]


def _prompt_attn_body(qi_ref, ki_ref, wit_ref, q_ref, k_ref, vt_ref, o_ref, key_ref, bias_ref, tie_ref,
                      *, topk, idx_bits):
    tq = ATTN_TQ
    i = pl.program_id(1)
    q0 = i * tq
    nseg = (q0 + tq + ATTN_SEG - 1) // ATTN_SEG
    ntile = nseg * (ATTN_SEG // ATTN_SUB)
    qpos = q0 + lax.broadcasted_iota(jnp.int32, (1, tq), 1)
    nt_dims = (((1,), (1,)), ((), ()))

    def score_tile(t, c):
        r0 = pl.multiple_of(t * ATTN_SUB, ATTN_SUB)
        ki = ki_ref[pl.ds(r0, ATTN_SUB), :]
        acc = jnp.zeros((ATTN_SUB, tq), jnp.float32)
        for h in range(N_IDX_HEADS):
            s = lax.dot_general(ki, qi_ref[h], nt_dims, preferred_element_type=jnp.float32)
            acc = acc + jnp.maximum(s, 0.0) * wit_ref[h:h + 1, :]
        kpos = r0 + lax.broadcasted_iota(jnp.int32, (ATTN_SUB, 1), 0)
        bits = lax.bitcast_convert_type(acc, jnp.int32)
        skey = jnp.where(bits < 0, bits ^ jnp.int32(0x7FFFFFFF), bits)
        skey = jnp.where(acc == 0.0, 0, skey)
        key_ref[pl.ds(r0, ATTN_SUB), :] = jnp.where(kpos <= qpos, skey, KEY_NEG_INF)
        return c

    lax.fori_loop(0, ntile, score_tile, 0)

    def count(pred):
        def tile(t, cnt):
            r0 = pl.multiple_of(t * ATTN_SUB, ATTN_SUB)
            kk = key_ref[pl.ds(r0, ATTN_SUB), :]
            kpos = r0 + lax.broadcasted_iota(jnp.int32, (ATTN_SUB, 1), 0)
            return cnt + _slab_sum(pred(kk, kpos).astype(jnp.int32))
        cnt = lax.fori_loop(0, ntile, tile, jnp.zeros((8, tq), jnp.int32))
        return jnp.sum(cnt, axis=0, keepdims=True)

    def bit_pass(p, t_u):
        cand_u = t_u | lax.shift_left(jnp.int32(1), 31 - p)
        cand_s = cand_u ^ jnp.int32(INT32_MIN)
        c = count(lambda kk, kpos: kk >= cand_s)
        return jnp.where(c >= topk, cand_u, t_u)

    t_u = lax.fori_loop(0, 32, bit_pass, jnp.zeros((1, tq), jnp.int32))
    t_s = t_u ^ jnp.int32(INT32_MIN)

    need = topk - count(lambda kk, kpos: kk > t_s)
    surplus = (count(lambda kk, kpos: kk >= t_s) > topk) & (t_s > KEY_NEG_INF)
    tie_ref[...] = jnp.full((1, tq), 2 ** idx_bits, jnp.int32)

    @pl.when(jnp.max(surplus.astype(jnp.int32)) > 0)
    def _():
        def idx_pass(p, lo):
            cand = lo | lax.shift_left(jnp.int32(1), idx_bits - 1 - p)
            c = count(lambda kk, kpos: (kk == t_s) & (kpos < cand))
            return jnp.where(c < need, cand, lo)

        tie_ref[...] = lax.fori_loop(0, idx_bits, idx_pass, jnp.zeros((1, tq), jnp.int32))

    last_tie = tie_ref[...]

    def bias_tile(t, c):
        r0 = pl.multiple_of(t * ATTN_SUB, ATTN_SUB)
        kk = key_ref[pl.ds(r0, ATTN_SUB), :]
        kpos = r0 + lax.broadcasted_iota(jnp.int32, (ATTN_SUB, 1), 0)
        sel = (kk > t_s) | ((kk == t_s) & (kpos <= last_tie))
        sel = sel & (kk > KEY_NEG_INF)
        bias_ref[pl.ds(r0, ATTN_SUB), :] = jnp.where(sel, 0.0, MASK_NEG)
        return c

    lax.fori_loop(0, ntile, bias_tile, 0)

    scale = HEAD_DIM ** -0.5

    def head_group_body(hg, c):
        heads = [hg * ATTN_HEADS_PER_STEP + a for a in range(ATTN_HEADS_PER_STEP)]

        def seg_body(s, carry):
            r0 = pl.multiple_of(s * ATTN_SEG, ATTN_SEG)
            bias = bias_ref[pl.ds(r0, ATTN_SEG), :]
            out = []
            for h, (m, l, acc) in zip(heads, carry):
                kh = k_ref[h, pl.ds(r0, ATTN_SEG), :]
                logits = lax.dot_general(kh, q_ref[h], nt_dims, preferred_element_type=jnp.float32) * scale
                logits = logits + bias
                m_new = jnp.maximum(m, jnp.max(logits, axis=0, keepdims=True))
                alpha = jnp.exp(m - m_new)
                p = jnp.exp(logits - m_new)
                l = alpha * l + jnp.sum(p, axis=0, keepdims=True)
                acc = alpha * acc + jnp.dot(vt_ref[h, s], p.astype(jnp.bfloat16),
                                            preferred_element_type=jnp.float32)
                out.append((m_new, l, acc))
            return tuple(out)

        init = tuple((jnp.full((1, tq), MASK_NEG, jnp.float32), jnp.zeros((1, tq), jnp.float32),
                      jnp.zeros((HEAD_DIM, tq), jnp.float32)) for _ in heads)
        for h, (_, l, acc) in zip(heads, lax.fori_loop(0, nseg, seg_body, init)):
            o_ref[h] = (acc / l).astype(o_ref.dtype)
        return c

    lax.fori_loop(0, N_HEADS // ATTN_HEADS_PER_STEP, head_group_body, 0)


def _prompt_attention(q_t, k_t, v_b, qi_t, ki_b, wi):
    B, _, T, _ = q_t.shape
    topk = min(TOPK_MAX, T // 4)
    nseg = T // ATTN_SEG
    bf = jnp.bfloat16
    vt = v_b.reshape(B, nseg, ATTN_SEG, N_HEADS, HEAD_DIM).transpose(0, 3, 1, 4, 2)
    wit = wi.transpose(0, 2, 1)
    body = functools.partial(_prompt_attn_body, topk=topk, idx_bits=int(np.ceil(np.log2(T))))
    out = pl.pallas_call(
        body,
        grid=(B, T // ATTN_TQ),
        in_specs=[
            pl.BlockSpec((None, N_IDX_HEADS, ATTN_TQ, IDX_DIM), lambda b, i: (b, 0, i, 0)),
            pl.BlockSpec((None, T, IDX_DIM), lambda b, i: (b, 0, 0)),
            pl.BlockSpec((None, N_IDX_HEADS, ATTN_TQ), lambda b, i: (b, 0, i)),
            pl.BlockSpec((None, N_HEADS, ATTN_TQ, HEAD_DIM), lambda b, i: (b, 0, i, 0)),
            pl.BlockSpec((None, N_HEADS, T, HEAD_DIM), lambda b, i: (b, 0, 0, 0)),
            pl.BlockSpec((None, N_HEADS, nseg, HEAD_DIM, ATTN_SEG), lambda b, i: (b, 0, 0, 0, 0)),
        ],
        out_specs=pl.BlockSpec((None, N_HEADS, HEAD_DIM, ATTN_TQ), lambda b, i: (b, 0, 0, i)),
        out_shape=jax.ShapeDtypeStruct((B, N_HEADS, HEAD_DIM, T), bf),
        scratch_shapes=[pltpu.VMEM((T, ATTN_TQ), jnp.int32), pltpu.VMEM((T, ATTN_TQ), jnp.float32),
                        pltpu.VMEM((1, ATTN_TQ), jnp.int32)],
        compiler_params=pltpu.CompilerParams(
            dimension_semantics=("arbitrary", "arbitrary"),
            vmem_limit_bytes=VMEM_LIMIT_BYTES),
        name="prompt_attn",
    )(qi_t, ki_b, wit, q_t, k_t, vt)
    return out.transpose(0, 3, 1, 2)


SAMPLE_IDX_PAGES = 8
SAMPLE_KV_PAGES = 4


def _score_key(score):
    bits = lax.bitcast_convert_type(score, jnp.int32)
    key = jnp.where(bits < 0, bits ^ jnp.int32(0x7FFFFFFF), bits)
    return jnp.where(score == 0.0, 0, key)


def _sample_select_body(pt_ref, qi_ref, w_ref, kin_ref, *rest, topk, n_pages, idx_bits):
    del pt_ref
    kid_refs = rest[:SAMPLE_IDX_PAGES]
    sel_ref, key_scr, tie_scr = rest[SAMPLE_IDX_PAGES:]
    g = pl.program_id(1)
    n_q = key_scr.shape[1]
    past = n_pages * PAGE_SIZE
    qi = qi_ref[...]
    wcol = w_ref[...]
    qpos = past + lax.broadcasted_iota(jnp.int32, (n_q, 1), 0)
    lane = lax.broadcasted_iota(jnp.int32, (1, PAGE_SIZE), 1)

    def page_keys(ki_page, page):
        s = lax.dot_general(qi, ki_page.astype(jnp.bfloat16), (((1,), (1,)), ((), ())),
                            preferred_element_type=jnp.float32)
        score = _slab_sum_n(jnp.maximum(s, 0.0) * wcol, n_q)
        kpos = page * PAGE_SIZE + lane
        return jnp.where(kpos <= qpos, _score_key(score), KEY_NEG_INF)

    for j in range(SAMPLE_IDX_PAGES):
        page = g * SAMPLE_IDX_PAGES + j
        key_scr[page] = page_keys(kid_refs[j][...], page)

    @pl.when(g == 0)
    def _():
        key_scr[n_pages] = page_keys(kin_ref[...], n_pages)

    @pl.when(g == pl.num_programs(1) - 1)
    def _():
        kk = key_scr[...]
        kpos = (lax.broadcasted_iota(jnp.int32, kk.shape, 0) * PAGE_SIZE
                + lax.broadcasted_iota(jnp.int32, kk.shape, 2))

        def count(m):
            return jnp.sum(jnp.sum(m.astype(jnp.int32), axis=0), axis=1, keepdims=True)

        def bit_pass(p, t_u):
            cand_u = t_u | lax.shift_left(jnp.int32(1), 31 - p)
            cand_s = cand_u ^ jnp.int32(INT32_MIN)
            return jnp.where(count(kk >= cand_s[None]) >= topk, cand_u, t_u)

        t_u = lax.fori_loop(0, 32, bit_pass, jnp.zeros((n_q, 1), jnp.int32))
        t_s = (t_u ^ jnp.int32(INT32_MIN))[None]
        need = topk - count(kk > t_s)
        surplus = (count(kk >= t_s) > topk) & (t_s[0] > KEY_NEG_INF)
        tie_scr[...] = jnp.full((n_q, 1), 2 ** idx_bits, jnp.int32)

        @pl.when(jnp.max(surplus.astype(jnp.int32)) > 0)
        def _():
            def idx_pass(p, lo):
                cand = lo | lax.shift_left(jnp.int32(1), idx_bits - 1 - p)
                return jnp.where(count((kk == t_s) & (kpos < cand[None])) < need, cand, lo)

            tie_scr[...] = lax.fori_loop(0, idx_bits, idx_pass, jnp.zeros((n_q, 1), jnp.int32))

        last_tie = tie_scr[...]
        sel = (kk > t_s) | ((kk == t_s) & (kpos <= last_tie[None]))
        sel = sel & (kk > KEY_NEG_INF)
        sel_ref[...] = sel.astype(jnp.float32)


def _slab_sum_n(x, n):
    parts = [x[r * n:(r + 1) * n] for r in range(x.shape[0] // n)]
    while len(parts) > 1:
        parts = [parts[a] + parts[a + 1] for a in range(0, len(parts), 2)]
    return parts[0]


def _sample_attn_body(pt_ref, q_ref, sel_ref, seln_ref, exp_ref, hm_ref, kn_ref, vn_ref, *rest):
    del pt_ref
    k_refs = rest[:SAMPLE_KV_PAGES]
    v_refs = rest[SAMPLE_KV_PAGES:2 * SAMPLE_KV_PAGES]
    o_ref, m_scr, l_scr, acc_scr = rest[2 * SAMPLE_KV_PAGES:]
    g = pl.program_id(1)
    n_q = seln_ref.shape[1]
    bf = jnp.bfloat16
    qall = q_ref[...]
    scale = HEAD_DIM ** -0.5

    @pl.when(g == 0)
    def _():
        m_scr[...] = jnp.full(m_scr.shape, MASK_NEG, jnp.float32)
        l_scr[...] = jnp.zeros(l_scr.shape, jnp.float32)
        acc_scr[...] = jnp.zeros(acc_scr.shape, jnp.float32)

    def pages_update(pages):
        logits = []
        for k_page, _, sel_tile, ncol in pages:
            kp = k_page.reshape(-1, HEAD_DIM).astype(bf)
            sel_cols = jnp.dot(sel_tile.astype(bf), exp_ref[:, :ncol], preferred_element_type=jnp.float32)
            sel_bias = (sel_cols - 1.0) * (-MASK_NEG)
            bias = jnp.concatenate([sel_bias] * N_HEADS, axis=0) + hm_ref[:, :ncol]
            logits.append(lax.dot_general(qall, kp, (((1,), (1,)), ((), ())),
                                          preferred_element_type=jnp.float32) * scale + bias)
        m = m_scr[...]
        m_new = m
        for lg in logits:
            m_new = jnp.maximum(m_new, jnp.max(lg, axis=1, keepdims=True))
        alpha = jnp.exp(m - m_new)
        l_new = alpha * l_scr[...]
        acc = alpha * acc_scr[...]
        for lg, (_, v_page, _, _) in zip(logits, pages):
            p = jnp.exp(lg - m_new)
            l_new = l_new + jnp.sum(p, axis=1, keepdims=True)
            acc = acc + jnp.dot(p.astype(bf), v_page.reshape(-1, HEAD_DIM).astype(bf),
                                preferred_element_type=jnp.float32)
        l_scr[...] = l_new
        acc_scr[...] = acc
        m_scr[...] = m_new

    @pl.when(g == 0)
    def _():
        pages_update([(kn_ref[...], vn_ref[...], seln_ref[0], n_q * N_HEADS)])

    pages_update([(k_refs[j][...], v_refs[j][...], sel_ref[j], PAGE_SIZE * N_HEADS)
                  for j in range(SAMPLE_KV_PAGES)])

    @pl.when(g == pl.num_programs(1) - 1)
    def _():
        o_ref[...] = acc_scr[...] / l_scr[...]


def _sample_attention(q, k, v, qi, ki, wi, *, layer, cache_k, cache_v, cache_kidx, page_table):
    Bd, S = q.shape[:2]
    n_pages = page_table.shape[1]
    past = n_pages * PAGE_SIZE
    topk = min(TOPK_MAX, (past + S) // 4)
    bf = jnp.bfloat16
    rows = N_HEADS * S
    qi_r = qi.astype(bf).transpose(0, 2, 1, 3).reshape(Bd, N_IDX_HEADS * S, IDX_DIM)
    w_r = jnp.broadcast_to(wi.transpose(0, 2, 1).reshape(Bd, N_IDX_HEADS * S, 1),
                           (Bd, N_IDX_HEADS * S, PAGE_SIZE))
    ki_new = jnp.pad(ki, ((0, 0), (0, PAGE_SIZE - S), (0, 0)))
    idx_bits = int(np.ceil(np.log2(past + PAGE_SIZE)))

    def kid_spec(j):
        return pl.BlockSpec((None, None, PAGE_SIZE, IDX_DIM),
                            lambda b, g, pt: (layer, pt[b, g * SAMPLE_IDX_PAGES + j], 0, 0))

    sel = pl.pallas_call(
        functools.partial(_sample_select_body, topk=topk, n_pages=n_pages, idx_bits=idx_bits),
        grid_spec=pltpu.PrefetchScalarGridSpec(
            num_scalar_prefetch=1,
            grid=(Bd, n_pages // SAMPLE_IDX_PAGES),
            in_specs=[
                pl.BlockSpec((None, N_IDX_HEADS * S, IDX_DIM), lambda b, g, pt: (b, 0, 0)),
                pl.BlockSpec((None, N_IDX_HEADS * S, PAGE_SIZE), lambda b, g, pt: (b, 0, 0)),
                pl.BlockSpec((None, PAGE_SIZE, IDX_DIM), lambda b, g, pt: (b, 0, 0)),
            ] + [kid_spec(j) for j in range(SAMPLE_IDX_PAGES)],
            out_specs=pl.BlockSpec((None, n_pages + 1, S, PAGE_SIZE), lambda b, g, pt: (b, 0, 0, 0)),
            scratch_shapes=[pltpu.VMEM((n_pages + 1, S, PAGE_SIZE), jnp.int32), pltpu.VMEM((S, 1), jnp.int32)],
        ),
        out_shape=jax.ShapeDtypeStruct((Bd, n_pages + 1, S, PAGE_SIZE), jnp.float32),
        compiler_params=pltpu.CompilerParams(
            dimension_semantics=("arbitrary", "arbitrary"),
            vmem_limit_bytes=VMEM_LIMIT_BYTES),
        name="sample_select",
    )(page_table, qi_r, w_r, ki_new, *([cache_kidx] * SAMPLE_IDX_PAGES))

    q_r = q.astype(bf).transpose(0, 2, 1, 3).reshape(Bd, rows, HEAD_DIM)
    ncol = PAGE_SIZE * N_HEADS
    col = np.arange(ncol)
    expand = jnp.asarray(col[None, :] // N_HEADS == np.arange(PAGE_SIZE)[:, None], bf)
    head_mask = jnp.asarray(np.where(col[None, :] % N_HEADS == np.arange(rows)[:, None] // S, 0.0, MASK_NEG),
                            jnp.float32)

    def kv_spec(j):
        return pl.BlockSpec((None, None, PAGE_SIZE, N_HEADS, HEAD_DIM),
                            lambda b, g, pt: (layer, pt[b, g * SAMPLE_KV_PAGES + j], 0, 0, 0))

    out = pl.pallas_call(
        _sample_attn_body,
        grid_spec=pltpu.PrefetchScalarGridSpec(
            num_scalar_prefetch=1,
            grid=(Bd, n_pages // SAMPLE_KV_PAGES),
            in_specs=[
                pl.BlockSpec((None, rows, HEAD_DIM), lambda b, g, pt: (b, 0, 0)),
                pl.BlockSpec((None, SAMPLE_KV_PAGES, S, PAGE_SIZE), lambda b, g, pt: (b, g, 0, 0)),
                pl.BlockSpec((None, 1, S, PAGE_SIZE), lambda b, g, pt: (b, n_pages, 0, 0)),
                pl.BlockSpec((PAGE_SIZE, ncol), lambda b, g, pt: (0, 0)),
                pl.BlockSpec((rows, ncol), lambda b, g, pt: (0, 0)),
                pl.BlockSpec((None, S, N_HEADS, HEAD_DIM), lambda b, g, pt: (b, 0, 0, 0)),
                pl.BlockSpec((None, S, N_HEADS, HEAD_DIM), lambda b, g, pt: (b, 0, 0, 0)),
            ] + [kv_spec(j) for j in range(SAMPLE_KV_PAGES)] * 2,
            out_specs=pl.BlockSpec((None, rows, HEAD_DIM), lambda b, g, pt: (b, 0, 0)),
            scratch_shapes=[pltpu.VMEM((rows, 1), jnp.float32), pltpu.VMEM((rows, 1), jnp.float32),
                            pltpu.VMEM((rows, HEAD_DIM), jnp.float32)],
        ),
        out_shape=jax.ShapeDtypeStruct((Bd, rows, HEAD_DIM), jnp.float32),
        compiler_params=pltpu.CompilerParams(
            dimension_semantics=("arbitrary", "arbitrary"),
            vmem_limit_bytes=VMEM_LIMIT_BYTES),
        name="sample_attn",
    )(page_table, q_r, sel, sel, expand, head_mask, k, v,
      *([cache_k] * SAMPLE_KV_PAGES), *([cache_v] * SAMPLE_KV_PAGES))
    return out.reshape(Bd, N_HEADS, S, HEAD_DIM).transpose(0, 2, 1, 3)


W_GROUP = 1024
COL_Q, COL_K, COL_V, COL_IDX = 0, 1024, 2048, 3072
COL_XC, COL_BG, COL_CG = 4096, 5120, 6144
COL_GA, COL_GC, COL_END = 7168, 9216, 11264
IDX_COLS = N_IDX_HEADS * IDX_DIM + IDX_DIM + N_IDX_HEADS
PROJ_TM = 1024
OUT_TM = 256


def _repack_w_in(w_in):
    n_qkv = 3 * ATTN_DIM
    pad = jnp.zeros((w_in.shape[0], W_GROUP - IDX_COLS), w_in.dtype)
    return jnp.concatenate([w_in[:, :n_qkv + IDX_COLS], pad, w_in[:, n_qkv + IDX_COLS:]],
                           axis=1).astype(jnp.bfloat16)


def _rope_tables(pos, half, width):
    inv = 1.0 / jnp.power(ROPE_THETA, jnp.arange(half, dtype=jnp.float32) / half)
    ang = pos.astype(jnp.float32)[:, None] * inv[None, :]
    cos, sin = jnp.cos(ang), jnp.sin(ang)
    reps = width // (2 * half)
    return (jnp.tile(jnp.concatenate([cos, cos], axis=1), (1, reps)),
            jnp.tile(jnp.concatenate([-sin, sin], axis=1), (1, reps)))


def _group_param(param, rows_per_group, tm):
    D = param.shape[-1]
    if rows_per_group % tm == 0:
        return param[:, None, :], pl.BlockSpec((None, 1, D), lambda i, *_: (i * tm // rows_per_group, 0, 0))
    return jnp.repeat(param, rows_per_group, axis=0), pl.BlockSpec((tm, D), lambda i, *_: (i, 0))


def _cparams(n_axes):
    return pltpu.CompilerParams(dimension_semantics=("arbitrary",) * n_axes, vmem_limit_bytes=VMEM_LIMIT_BYTES)


def _modnorm(x, w, scale, shift):
    y = x * lax.rsqrt(jnp.mean(x * x, axis=-1, keepdims=True) + EPS) * w
    return y * (1.0 + scale) + shift


def _norm1_body(x_ref, sc_ref, sh_ref, w_ref, o_ref):
    o_ref[...] = _modnorm(x_ref[...], w_ref[...], sc_ref[...], sh_ref[...]).astype(o_ref.dtype)


def _norm1(x, norm_w, sc, sh, rows_per_group):
    N, D = x.shape
    tm = min(512, N)
    sc_arr, sc_spec = _group_param(sc, rows_per_group, tm)
    sh_arr, sh_spec = _group_param(sh, rows_per_group, tm)
    return pl.pallas_call(
        _norm1_body,
        grid=(N // tm,),
        in_specs=[pl.BlockSpec((tm, D), lambda i: (i, 0)), sc_spec, sh_spec,
                  pl.BlockSpec((1, D), lambda i: (0, 0))],
        out_specs=pl.BlockSpec((tm, D), lambda i: (i, 0)),
        out_shape=jax.ShapeDtypeStruct((N, D), jnp.bfloat16),
        compiler_params=_cparams(1),
        name="norm1",
    )(x, sc_arr, sh_arr, norm_w[None, :])


def _qkv_body(h_ref, w_ref, nw_ref, cos_ref, sin_ref, qt_ref, k_ref, kt_ref, v_ref, vb_ref):
    j = pl.program_id(1)
    acc = jnp.dot(h_ref[...], w_ref[...], preferred_element_type=jnp.float32)

    def norm_rope(nw):
        heads = []
        for hd in range(N_HEADS):
            xh = acc[:, hd * HEAD_DIM:(hd + 1) * HEAD_DIM]
            y = xh * lax.rsqrt(jnp.mean(xh * xh, axis=-1, keepdims=True) + EPS) * nw
            heads.append(y * cos_ref[...] + pltpu.roll(y, HEAD_DIM // 2, 1) * sin_ref[...])
        return heads

    @pl.when(j == 0)
    def _():
        for hd, y in enumerate(norm_rope(nw_ref[0:1, :])):
            qt_ref[hd] = y.astype(qt_ref.dtype)

    @pl.when(j == 1)
    def _():
        for hd, y in enumerate(norm_rope(nw_ref[1:2, :])):
            k_ref[:, hd * HEAD_DIM:(hd + 1) * HEAD_DIM] = y
            kt_ref[hd] = y.astype(kt_ref.dtype)

    @pl.when(j == 2)
    def _():
        v_ref[...] = acc
        vb_ref[...] = acc.astype(vb_ref.dtype)


def _qkv_proj(h, w_b, q_norm_w, k_norm_w, cos, sin, n_seq, seq_len):
    N, D = h.shape
    tm = min(PROJ_TM, seq_len)
    tiles = seq_len // tm
    bf = jnp.bfloat16
    nw = jnp.zeros((8, HEAD_DIM), jnp.float32).at[0].set(q_norm_w).at[1].set(k_norm_w)
    head_major = pl.BlockSpec((None, N_HEADS, tm, HEAD_DIM), lambda i, j: (i // tiles, 0, i % tiles, 0))
    rows = pl.BlockSpec((tm, ATTN_DIM), lambda i, j: (i, 0))
    table = pl.BlockSpec((tm, HEAD_DIM), lambda i, j: (i % tiles, 0))
    return pl.pallas_call(
        _qkv_body,
        grid=(N // tm, 3),
        in_specs=[pl.BlockSpec((tm, D), lambda i, j: (i, 0)),
                  pl.BlockSpec((D, W_GROUP), lambda i, j: (0, j)),
                  pl.BlockSpec((8, HEAD_DIM), lambda i, j: (0, 0)), table, table],
        out_specs=[head_major, rows, head_major, rows, rows],
        out_shape=[jax.ShapeDtypeStruct((n_seq, N_HEADS, seq_len, HEAD_DIM), bf),
                   jax.ShapeDtypeStruct((N, ATTN_DIM), jnp.float32),
                   jax.ShapeDtypeStruct((n_seq, N_HEADS, seq_len, HEAD_DIM), bf),
                   jax.ShapeDtypeStruct((N, ATTN_DIM), jnp.float32),
                   jax.ShapeDtypeStruct((N, ATTN_DIM), bf)],
        compiler_params=_cparams(2),
        name="qkv_proj",
    )(h, w_b, nw, cos, sin)


def _idx_body(h_ref, w_ref, cos_ref, sin_ref, qit_ref, ki_ref, kib_ref, wi_ref):
    acc = jnp.dot(h_ref[...], w_ref[...], preferred_element_type=jnp.float32)
    first_half = lax.broadcasted_iota(jnp.int32, (1, 2 * IDX_DIM), 1) % IDX_DIM < IDX_DIM // 2

    def rope_pair(x):
        swapped = jnp.where(first_half, pltpu.roll(x, 2 * IDX_DIM - IDX_DIM // 2, 1),
                            pltpu.roll(x, IDX_DIM // 2, 1))
        return x * cos_ref[...] + swapped * sin_ref[...]

    for pair in range(N_IDX_HEADS // 2):
        y = rope_pair(acc[:, pair * 2 * IDX_DIM:(pair + 1) * 2 * IDX_DIM])
        qit_ref[2 * pair] = y[:, :IDX_DIM].astype(qit_ref.dtype)
        qit_ref[2 * pair + 1] = y[:, IDX_DIM:].astype(qit_ref.dtype)
    tail = acc[:, N_IDX_HEADS * IDX_DIM:N_IDX_HEADS * IDX_DIM + 2 * IDX_DIM]
    ki = rope_pair(tail)[:, :IDX_DIM]
    ki_ref[...] = ki
    kib_ref[...] = ki.astype(kib_ref.dtype)
    wi_ref[...] = tail[:, IDX_DIM:IDX_DIM + N_IDX_HEADS] * ((N_IDX_HEADS ** -0.5) * (IDX_DIM ** -0.5))


def _idx_proj(h, w_b, cos, sin, n_seq, seq_len):
    N, D = h.shape
    tm = min(PROJ_TM, seq_len)
    tiles = seq_len // tm
    bf = jnp.bfloat16
    table = pl.BlockSpec((tm, 2 * IDX_DIM), lambda i: (i % tiles, 0))
    return pl.pallas_call(
        _idx_body,
        grid=(N // tm,),
        in_specs=[pl.BlockSpec((tm, D), lambda i: (i, 0)),
                  pl.BlockSpec((D, W_GROUP), lambda i: (0, COL_IDX // W_GROUP)), table, table],
        out_specs=[pl.BlockSpec((None, N_IDX_HEADS, tm, IDX_DIM), lambda i: (i // tiles, 0, i % tiles, 0)),
                   pl.BlockSpec((tm, IDX_DIM), lambda i: (i, 0)),
                   pl.BlockSpec((tm, IDX_DIM), lambda i: (i, 0)),
                   pl.BlockSpec((tm, N_IDX_HEADS), lambda i: (i, 0))],
        out_shape=[jax.ShapeDtypeStruct((n_seq, N_IDX_HEADS, seq_len, IDX_DIM), bf),
                   jax.ShapeDtypeStruct((N, IDX_DIM), jnp.float32),
                   jax.ShapeDtypeStruct((N, IDX_DIM), bf),
                   jax.ShapeDtypeStruct((N, N_IDX_HEADS), jnp.float32)],
        compiler_params=_cparams(1),
        name="idx_proj",
    )(h, w_b, cos, sin)


CONV_TN = 512


def _conv_body(h_ref, wx_ref, wb_ref, wc_ref, cw_ref, st_ref, y_ref, tail_ref, carry, *, tiles):
    i = pl.program_id(0)
    j = pl.program_id(1)
    h = h_ref[...]
    dot = functools.partial(jnp.dot, preferred_element_type=jnp.float32)
    u = dot(h, wc_ref[...]) * dot(h, wx_ref[...])
    bg = dot(h, wb_ref[...])

    @pl.when(i % tiles == 0)
    def _():
        carry[j] = st_ref[...]

    prev = carry[j]
    row = lax.broadcasted_iota(jnp.int32, (u.shape[0], 1), 0)
    u1 = jnp.where(row == 0, prev[1:2, :], pltpu.roll(u, 1, 0))
    u2 = jnp.where(row == 0, prev[0:1, :], jnp.where(row == 1, prev[1:2, :], pltpu.roll(u, 2, 0)))
    y_ref[...] = (bg * (cw_ref[0:1, :] * u2 + cw_ref[1:2, :] * u1 + cw_ref[2:3, :] * u)).astype(y_ref.dtype)
    last = u[u.shape[0] - (CONV_WIDTH - 1):, :]
    carry[j] = last
    tail_ref[...] = last


def _conv_proj(h, w_b, conv_w, state, n_seq, seq_len):
    N, D = h.shape
    tm = min(PROJ_TM, seq_len)
    tiles = seq_len // tm
    n_ct = CONV_DIM // CONV_TN

    def w_spec(col):
        return pl.BlockSpec((D, CONV_TN), lambda i, j: (0, col // CONV_TN + j))

    conv, tails = pl.pallas_call(
        functools.partial(_conv_body, tiles=tiles),
        grid=(N // tm, n_ct),
        in_specs=[pl.BlockSpec((tm, D), lambda i, j: (i, 0)),
                  w_spec(COL_XC), w_spec(COL_BG), w_spec(COL_CG),
                  pl.BlockSpec((CONV_WIDTH, CONV_TN), lambda i, j: (0, j)),
                  pl.BlockSpec((None, CONV_WIDTH - 1, CONV_TN), lambda i, j: (i // tiles, 0, j))],
        out_specs=[pl.BlockSpec((tm, CONV_TN), lambda i, j: (i, j)),
                   pl.BlockSpec((None, CONV_WIDTH - 1, CONV_TN), lambda i, j: (i, 0, j))],
        out_shape=[jax.ShapeDtypeStruct((N, CONV_DIM), jnp.bfloat16),
                   jax.ShapeDtypeStruct((N // tm, CONV_WIDTH - 1, CONV_DIM), jnp.float32)],
        scratch_shapes=[pltpu.VMEM((n_ct, CONV_WIDTH - 1, CONV_TN), jnp.float32)],
        compiler_params=_cparams(2),
        name="conv_proj",
    )(h, w_b, w_b, w_b, conv_w, state)
    return conv, tails[tiles - 1::tiles]


MERGE_TN = 512


def _merge_body(h_ref, a_ref, c_ref, wga_ref, wgc_ref, wba_ref, wbc_ref, o_ref):
    bf = jnp.bfloat16
    dot = functools.partial(jnp.dot, preferred_element_type=jnp.float32)
    h = h_ref[...]
    a_br = dot(a_ref[...], wba_ref[...].astype(bf))
    c_br = dot(c_ref[...], wbc_ref[...].astype(bf))
    merged = jax.nn.sigmoid(dot(h, wga_ref[...])) * a_br + jax.nn.sigmoid(dot(h, wgc_ref[...])) * c_br
    o_ref[...] = merged.astype(o_ref.dtype)


def _merge_proj(h, attn, conv, w_b, w_branch_attn, w_branch_conv):
    N, D = h.shape
    tm = min(PROJ_TM, N)
    return pl.pallas_call(
        _merge_body,
        grid=(N // tm, D // MERGE_TN),
        in_specs=[pl.BlockSpec((tm, D), lambda i, j: (i, 0)),
                  pl.BlockSpec((tm, ATTN_DIM), lambda i, j: (i, 0)),
                  pl.BlockSpec((tm, CONV_DIM), lambda i, j: (i, 0)),
                  pl.BlockSpec((D, MERGE_TN), lambda i, j: (0, COL_GA // MERGE_TN + j)),
                  pl.BlockSpec((D, MERGE_TN), lambda i, j: (0, COL_GC // MERGE_TN + j)),
                  pl.BlockSpec((ATTN_DIM, MERGE_TN), lambda i, j: (0, j)),
                  pl.BlockSpec((CONV_DIM, MERGE_TN), lambda i, j: (0, j))],
        out_specs=pl.BlockSpec((tm, MERGE_TN), lambda i, j: (i, j)),
        out_shape=jax.ShapeDtypeStruct((N, D), jnp.bfloat16),
        compiler_params=_cparams(2),
        name="merge_proj",
    )(h, attn, conv, w_b, w_b, w_branch_attn, w_branch_conv)


def _out_body(m_ref, x_ref, g1_ref, sc_ref, sh_ref, nw_ref, wo_ref, rh_ref, rl_ref, x1_ref, h2_ref, lg_ref):
    dot = functools.partial(jnp.dot, preferred_element_type=jnp.float32)
    x1 = x_ref[...] + g1_ref[...] * dot(m_ref[...], wo_ref[...])
    x1_ref[...] = x1
    h2 = _modnorm(x1, nw_ref[...], sc_ref[...], sh_ref[...])
    h2_ref[...] = h2
    h_hi, h_lo = _hi_lo(h2)
    lg_ref[...] = dot(h_hi, rh_ref[...]) + (dot(h_lo, rh_ref[...]) + dot(h_hi, rl_ref[...]))


def _out_proj(merged, x, g1, sc2, sh2, norm2_w, w_out_b, w_router, rows_per_group):
    N, D = x.shape
    tm = min(OUT_TM, N)
    g1_arr, g1_spec = _group_param(g1, rows_per_group, tm)
    sc_arr, sc_spec = _group_param(sc2, rows_per_group, tm)
    sh_arr, sh_spec = _group_param(sh2, rows_per_group, tm)
    r_hi, r_lo = _hi_lo(w_router)
    n_r = w_router.shape[1]
    rows = pl.BlockSpec((tm, D), lambda i: (i, 0))
    return pl.pallas_call(
        _out_body,
        grid=(N // tm,),
        in_specs=[rows, rows, g1_spec, sc_spec, sh_spec,
                  pl.BlockSpec((1, D), lambda i: (0, 0)),
                  pl.BlockSpec((D, D), lambda i: (0, 0)),
                  pl.BlockSpec((D, n_r), lambda i: (0, 0)),
                  pl.BlockSpec((D, n_r), lambda i: (0, 0))],
        out_specs=[rows, rows, pl.BlockSpec((tm, n_r), lambda i: (i, 0))],
        out_shape=[jax.ShapeDtypeStruct((N, D), jnp.float32), jax.ShapeDtypeStruct((N, D), jnp.float32),
                   jax.ShapeDtypeStruct((N, n_r), jnp.float32)],
        compiler_params=_cparams(1),
        name="out_proj",
    )(merged, x, g1_arr, sc_arr, sh_arr, norm2_w[None, :], w_out_b, r_hi, r_lo)


def _short_conv(xc, bg, cg, conv_w, buf):
    u = cg * xc
    ext = jnp.concatenate([buf, u], axis=1)
    T = u.shape[1]
    y = sum(conv_w[j] * ext[:, j:j + T] for j in range(CONV_WIDTH))
    return bg * y, ext[:, -(CONV_WIDTH - 1):]


MOE_TM = 256
COMBINE_TM = 128


def _route(r_logits, b_gr, b_er):
    n = r_logits.shape[0]
    g_logits = r_logits[:, :N_GROUPS]
    g_prob = jax.nn.softmax(g_logits, axis=-1)
    g_sel = jnp.argmax(g_logits + b_gr, axis=-1)
    g_w = jnp.take_along_axis(g_prob, g_sel[:, None], axis=1)[:, 0]
    e_logits = r_logits[:, N_GROUPS:N_GROUPS * (1 + EXPERTS_PER_GROUP)].reshape(n, N_GROUPS, EXPERTS_PER_GROUP)
    e_in = jnp.take_along_axis(e_logits, g_sel[:, None, None], axis=1)[:, 0]
    _, e_idx = lax.top_k(e_in + b_er[g_sel], TOPK_EXPERTS)
    e_p = jax.nn.softmax(jnp.take_along_axis(e_in, e_idx, axis=1), axis=-1)
    eid = (g_sel[:, None] * EXPERTS_PER_GROUP + e_idx).astype(jnp.int32)
    return eid, g_w[:, None] * e_p


def _moe_plan(eid, wgt, n_tiles):
    n_exp = N_GROUPS * EXPERTS_PER_GROUP
    a = eid.size
    e_flat = eid.reshape(a)
    onehot = (e_flat[:, None] == jnp.arange(n_exp, dtype=jnp.int32)[None, :]).astype(jnp.int32)
    running = jnp.cumsum(onehot, axis=0)
    counts = running[-1]
    padded = (counts + MOE_TM - 1) // MOE_TM * MOE_TM
    pad_end = jnp.cumsum(padded)
    pad_start = pad_end - padded
    cnt_start = jnp.cumsum(counts) - counts
    pos = pad_start[e_flat] + jnp.sum(running * onehot, axis=1) - 1
    order = jnp.argsort(e_flat, stable=True).astype(jnp.int32)
    rows = n_tiles * MOE_TM
    row = jnp.arange(rows, dtype=jnp.int32)
    tile_expert = jnp.minimum(jnp.searchsorted(pad_end, jnp.arange(n_tiles, dtype=jnp.int32) * MOE_TM, side='right'),
                              n_exp - 1).astype(jnp.int32)
    row_expert = jnp.repeat(tile_expert, MOE_TM)
    rank = row - pad_start[row_expert]
    row_valid = (rank < counts[row_expert]) & (row < pad_end[-1])
    assign = order[jnp.clip(cnt_start[row_expert] + rank, 0, a - 1)]
    row_token = jnp.where(row_valid, assign // TOPK_EXPERTS, 0)
    row_weight = jnp.where(row_valid, wgt.reshape(a)[assign], 0.0)
    n_used = (pad_end[-1] // MOE_TM).astype(jnp.int32).reshape(1)
    return tile_expert, n_used, row_token, row_weight.reshape(rows, 1), pos.astype(jnp.int32)


def _moe_body(te_ref, nu_ref, tok_ref, x_hbm, rw_ref, wg_ref, wu_ref, wd_ref, y_ref,
              buf, sem, wgb, wub, wdb):
    t = pl.program_id(0)
    n_used = nu_ref[0]
    slot = lax.rem(t, 2)
    bf = jnp.bfloat16

    def row_copy(tile, slot_, r):
        tok = tok_ref[tile * MOE_TM + r]
        return pltpu.make_async_copy(x_hbm.at[pl.ds(tok, 1), :], buf.at[slot_, pl.ds(r, 1), :], sem.at[slot_])

    def gather_start(tile, slot_):
        def body(r, c):
            row_copy(tile, slot_, r).start()
            return c
        lax.fori_loop(0, MOE_TM, body, 0, unroll=8)

    def gather_wait(tile, slot_):
        def body(r, c):
            row_copy(tile, slot_, r).wait()
            return c
        lax.fori_loop(0, MOE_TM, body, 0, unroll=8)

    @pl.when(t == 0)
    def _():
        gather_start(0, 0)

    @pl.when(t + 1 < n_used)
    def _():
        gather_start(t + 1, 1 - slot)

    @pl.when(t < n_used)
    def _():
        expert_changed = jnp.logical_or(t == 0, te_ref[t] != te_ref[jnp.maximum(t - 1, 0)])

        @pl.when(expert_changed)
        def _():
            wgb[...] = wg_ref[...].astype(bf)
            wub[...] = wu_ref[...].astype(bf)
            wdb[...] = wd_ref[...].astype(bf)

        gather_wait(t, slot)
        x = buf[slot].astype(bf)
        g = jnp.dot(x, wgb[...], preferred_element_type=jnp.float32)
        u = jnp.dot(x, wub[...], preferred_element_type=jnp.float32)
        hid = jax.nn.silu(g) * u * rw_ref[...]
        y_ref[...] = jnp.dot(hid.astype(bf), wdb[...], preferred_element_type=jnp.float32)

    @pl.when(t >= n_used)
    def _():
        y_ref[...] = jnp.zeros_like(y_ref)


def _moe_experts(x_all, plan, w_g, w_u, w_d, layer, n_tiles):
    tile_expert, n_used, row_token, row_weight, _ = plan
    D = x_all.shape[1]
    F = w_g.shape[-1]

    def w_spec(k_dim, n_dim):
        return pl.BlockSpec((None, None, None, k_dim, n_dim),
                            lambda t, te, nu, tok: (layer, te[t] // EXPERTS_PER_GROUP, te[t] % EXPERTS_PER_GROUP, 0, 0))

    return pl.pallas_call(
        _moe_body,
        grid_spec=pltpu.PrefetchScalarGridSpec(
            num_scalar_prefetch=3,
            grid=(n_tiles,),
            in_specs=[
                pl.BlockSpec(memory_space=pl.ANY),
                pl.BlockSpec((MOE_TM, 1), lambda t, te, nu, tok: (t, 0)),
                w_spec(D, F), w_spec(D, F), w_spec(F, D),
            ],
            out_specs=pl.BlockSpec((MOE_TM, D), lambda t, te, nu, tok: (t, 0)),
            scratch_shapes=[
                pltpu.VMEM((2, MOE_TM, D), jnp.float32),
                pltpu.SemaphoreType.DMA((2,)),
                pltpu.VMEM((D, F), jnp.bfloat16), pltpu.VMEM((D, F), jnp.bfloat16),
                pltpu.VMEM((F, D), jnp.bfloat16),
            ],
        ),
        out_shape=jax.ShapeDtypeStruct((n_tiles * MOE_TM, D), jnp.float32),
        compiler_params=pltpu.CompilerParams(
            dimension_semantics=("arbitrary",),
            vmem_limit_bytes=VMEM_LIMIT_BYTES),
        name="moe_experts",
    )(tile_expert, n_used, row_token, x_all, row_weight, w_g, w_u, w_d)


def _combine_body(pos_ref, x_ref, g_ref, y_hbm, o_ref, buf, sem):
    i = pl.program_id(0)
    slot = lax.rem(i, 2)

    def row_copy(tile, slot_, r, j):
        row = pos_ref[(tile * COMBINE_TM + r) * TOPK_EXPERTS + j]
        return pltpu.make_async_copy(y_hbm.at[pl.ds(row, 1), :], buf.at[slot_, j, pl.ds(r, 1), :], sem.at[slot_])

    def gather(tile, slot_, start):
        def body(r, c):
            for j in range(TOPK_EXPERTS):
                cp = row_copy(tile, slot_, r, j)
                if start:
                    cp.start()
                else:
                    cp.wait()
            return c
        lax.fori_loop(0, COMBINE_TM, body, 0, unroll=8)

    @pl.when(i == 0)
    def _():
        gather(0, 0, True)

    @pl.when(i + 1 < pl.num_programs(0))
    def _():
        gather(i + 1, 1 - slot, True)

    gather(i, slot, False)
    moe = buf[slot, 0]
    for j in range(1, TOPK_EXPERTS):
        moe = moe + buf[slot, j]
    o_ref[...] = x_ref[...] + g_ref[...] * moe


def _moe_combine(x, gate, pos, y_sorted, rows_per_gate):
    N, D = x.shape
    tm = COMBINE_TM
    assert N % tm == 0 and (rows_per_gate % tm == 0 or tm % rows_per_gate == 0)
    if rows_per_gate >= tm:
        g_arr = gate.reshape(-1, 1, D)
        g_spec = pl.BlockSpec((None, 1, D), lambda i, pos: (i * tm // rows_per_gate, 0, 0))
    else:
        g_arr = jnp.repeat(gate, rows_per_gate, axis=0)
        g_spec = pl.BlockSpec((tm, D), lambda i, pos: (i, 0))
    return pl.pallas_call(
        _combine_body,
        grid_spec=pltpu.PrefetchScalarGridSpec(
            num_scalar_prefetch=1,
            grid=(N // tm,),
            in_specs=[pl.BlockSpec((tm, D), lambda i, pos: (i, 0)), g_spec, pl.BlockSpec(memory_space=pl.ANY)],
            out_specs=pl.BlockSpec((tm, D), lambda i, pos: (i, 0)),
            scratch_shapes=[pltpu.VMEM((2, TOPK_EXPERTS, tm, D), jnp.float32), pltpu.SemaphoreType.DMA((2,))],
        ),
        out_shape=jax.ShapeDtypeStruct((N, D), jnp.float32),
        compiler_params=pltpu.CompilerParams(
            dimension_semantics=("arbitrary",),
            vmem_limit_bytes=VMEM_LIMIT_BYTES),
        name="moe_combine",
    )(pos, x, g_arr, y_sorted)


def _dense_block(x, mod, pos, mod_rows, layout, attend, conv_fn, lw):
    n_seq, seq_len = layout
    sh1, sc1, g1, sh2, sc2, g2 = jnp.split(mod, 6, axis=-1)
    h = _norm1(x, lw['norm1_w'], sc1, sh1, mod_rows)
    cos, sin = _rope_tables(pos, HEAD_DIM // 2, HEAD_DIM)
    q_t, k, k_t, v, v_b = _qkv_proj(h, lw['w_b'], lw['q_norm_w'], lw['k_norm_w'], cos, sin, n_seq, seq_len)
    cos_i, sin_i = _rope_tables(pos, IDX_DIM // 2, 2 * IDX_DIM)
    qi_t, ki, ki_b, wi = _idx_proj(h, lw['w_b'], cos_i, sin_i, n_seq, seq_len)
    attn = attend(q_t, k_t, k, v, v_b, qi_t, ki, ki_b, wi)
    conv, new_state = conv_fn(h)
    merged = _merge_proj(h, attn, conv, lw['w_b'], lw['w_branch_attn'], lw['w_branch_conv'])
    x1, h2, logits = _out_proj(merged, x, g1, sc2, sh2, lw['norm2_w'], lw['w_out_b'], lw['w_router'], mod_rows)
    return x1, h2, logits, g2, k, v, ki, new_state


def kernel(x_prompt, x_sample, c_prompt, c_sample, cache_k, cache_v, cache_kidx, state_conv, page_table, norm1_w, norm2_w, w_ada, b_ada, w_in, q_norm_w, k_norm_w, conv_w, w_branch_attn, w_branch_conv, w_out, w_group_router, b_group_router, w_expert_router, b_expert_router, w_e_gate, w_e_up, w_e_down):
    B, T, _ = x_prompt.shape
    Bd, S, _ = x_sample.shape
    depth = norm1_w.shape[0]
    past = page_table.shape[1] * PAGE_SIZE
    n_p, n_s = B * T, Bd * S
    pos_prompt = jnp.arange(T)
    pos_sample = jnp.tile(past + jnp.arange(S), Bd)
    yp, ys = x_prompt, x_sample
    outs = [[] for _ in range(8)]
    c_all = jnp.concatenate([c_prompt, c_sample], axis=0)
    n_c = c_all.shape[0]
    c_pad = jnp.pad(jax.nn.silu(c_all), ((0, (-n_c) % 8), (0, 0)))
    bf = jnp.bfloat16
    for l in range(depth):
        w_router = jnp.pad(jnp.concatenate([w_group_router[l], w_expert_router[l].reshape(D_MODEL, -1)], axis=1),
                           ((0, 0), (0, 128 - N_GROUPS * (1 + EXPERTS_PER_GROUP))))
        lw = {
            'norm1_w': norm1_w[l], 'norm2_w': norm2_w[l], 'w_b': _repack_w_in(w_in[l]),
            'q_norm_w': q_norm_w[l], 'k_norm_w': k_norm_w[l],
            'w_branch_attn': w_branch_attn[l], 'w_branch_conv': w_branch_conv[l],
            'w_out_b': w_out[l].astype(bf), 'w_router': w_router,
        }
        mod = _mm(c_pad, w_ada[l], tn=1024)[:n_c] + b_ada[l]

        def prompt_attend(q_t, k_t, k, v, v_b, qi_t, ki, ki_b, wi):
            out = _prompt_attention(q_t, k_t, v_b.reshape(B, T, ATTN_DIM), qi_t, ki_b.reshape(B, T, IDX_DIM),
                                    wi.reshape(B, T, N_IDX_HEADS))
            return out.reshape(n_p, ATTN_DIM)

        def prompt_conv(h):
            zero_state = jnp.zeros((B, CONV_WIDTH - 1, CONV_DIM), jnp.float32)
            return _conv_proj(h, lw['w_b'], conv_w[l], zero_state, B, T)

        def sample_attend(q_t, k_t, k, v, v_b, qi_t, ki, ki_b, wi):
            def per_token(a):
                return a[0].transpose(1, 0, 2).reshape(Bd, S, a.shape[1], a.shape[3])
            out = _sample_attention(per_token(q_t), k.reshape(Bd, S, N_HEADS, HEAD_DIM),
                                    v.reshape(Bd, S, N_HEADS, HEAD_DIM), per_token(qi_t),
                                    ki.reshape(Bd, S, IDX_DIM), wi.reshape(Bd, S, N_IDX_HEADS),
                                    layer=l, cache_k=cache_k, cache_v=cache_v, cache_kidx=cache_kidx,
                                    page_table=page_table)
            return out.reshape(n_s, ATTN_DIM).astype(bf)

        def sample_conv(h):
            xbc = _mm(h, lw['w_b'][:, COL_XC:COL_GA]).reshape(Bd, S, 3 * CONV_DIM)
            xc, bg, cg = jnp.split(xbc, 3, axis=-1)
            conv, state = _short_conv(xc, bg, cg, conv_w[l], state_conv[l])
            return conv.reshape(n_s, CONV_DIM).astype(bf), state

        xp, hp, lgp, gp, kp, vp, kip, cp = _dense_block(
            yp.reshape(n_p, D_MODEL), mod[:B], pos_prompt, T, (B, T), prompt_attend, prompt_conv, lw)
        xs, hs, lgs, gs, ksm, vsm, kis, cs = _dense_block(
            ys.reshape(n_s, D_MODEL), mod[B:], pos_sample, S, (1, n_s), sample_attend, sample_conv, lw)
        for lst, val in zip(outs, (kp.reshape(B, T, N_HEADS, HEAD_DIM), vp.reshape(B, T, N_HEADS, HEAD_DIM),
                                   kip.reshape(B, T, IDX_DIM), cp,
                                   ksm.reshape(Bd, S, N_HEADS, HEAD_DIM), vsm.reshape(Bd, S, N_HEADS, HEAD_DIM),
                                   kis.reshape(Bd, S, IDX_DIM), cs)):
            lst.append(val)
        h_all = jnp.concatenate([hp, hs], axis=0)
        eid, wgt = _route(jnp.concatenate([lgp, lgs], axis=0), b_group_router[l], b_expert_router[l])
        n_assign = eid.size
        n_tiles = (n_assign + N_GROUPS * EXPERTS_PER_GROUP * (MOE_TM - 1)) // MOE_TM
        plan = _moe_plan(eid, wgt, n_tiles)
        y_sorted = _moe_experts(h_all, plan, w_e_gate, w_e_up, w_e_down, l, n_tiles)
        pos = plan[4]
        yp = _moe_combine(xp, gp, pos[:n_p * TOPK_EXPERTS], y_sorted, T).reshape(B, T, D_MODEL)
        ys = _moe_combine(xs, gs, pos[n_p * TOPK_EXPERTS:], y_sorted, S).reshape(Bd, S, D_MODEL)
    return (yp, ys) + tuple(jnp.stack(o) for o in outs)
```

```python
import functools

import jax
import jax.numpy as jnp
import numpy as np
from jax import lax
from jax.experimental import pallas as pl
from jax.experimental.pallas import tpu as pltpu

D_MODEL = 2048
PAGE_SIZE = 128
N_HEADS = 8
HEAD_DIM = 128
ATTN_DIM = N_HEADS * HEAD_DIM
N_IDX_HEADS = 8
IDX_DIM = 64
TOPK_MAX = 256
CONV_DIM = D_MODEL // 2
CONV_WIDTH = 3
N_GROUPS = 4
EXPERTS_PER_GROUP = 8
TOPK_EXPERTS = 2
EXPERT_FF = 512
ROPE_THETA = 10000.0
EPS = 1e-6

VMEM_LIMIT_BYTES = 56 * 1024 * 1024


def _mm_body(a_ref, b_ref, o_ref):
    a = a_ref[...].astype(jnp.bfloat16)
    b = b_ref[...].astype(jnp.bfloat16)
    o_ref[...] = jnp.dot(a, b, preferred_element_type=jnp.float32).astype(o_ref.dtype)


def _mm(a, b, *, tm=512, tn=512, out_dtype=jnp.float32):
    M, K = a.shape
    _, N = b.shape
    tm = min(tm, M)
    tn = min(tn, N)
    return pl.pallas_call(
        _mm_body,
        grid=(pl.cdiv(M, tm), pl.cdiv(N, tn)),
        in_specs=[pl.BlockSpec((tm, K), lambda i, j: (i, 0)),
                  pl.BlockSpec((K, tn), lambda i, j: (0, j))],
        out_specs=pl.BlockSpec((tm, tn), lambda i, j: (i, j)),
        out_shape=jax.ShapeDtypeStruct((M, N), out_dtype),
        compiler_params=pltpu.CompilerParams(
            dimension_semantics=("arbitrary", "arbitrary"),
            vmem_limit_bytes=VMEM_LIMIT_BYTES),
        name="mm",
    )(a, b)


def _hi_lo(x):
    hi = x.astype(jnp.bfloat16)
    lo = (x - hi.astype(jnp.float32)).astype(jnp.bfloat16)
    return hi, lo


ATTN_TQ = 256
ATTN_SEG = 1024
ATTN_SUB = 256
ATTN_HEADS_PER_STEP = 1
MASK_NEG = -1e30
KEY_NEG_INF = -2139095041
INT32_MIN = -2 ** 31


def _slab_sum(x):
    parts = [x[r * 8:(r + 1) * 8] for r in range(x.shape[0] // 8)]
    while len(parts) > 1:
        parts = [parts[a] + parts[a + 1] for a in range(0, len(parts), 2)]
    return parts[0]


def _prompt_attn_body(qi_ref, ki_ref, wit_ref, q_ref, k_ref, vt_ref, o_ref, key_ref, bias_ref, tie_ref,
                      *, topk, idx_bits):
    tq = ATTN_TQ
    i = pl.program_id(1)
    q0 = i * tq
    nseg = (q0 + tq + ATTN_SEG - 1) // ATTN_SEG
    ntile = nseg * (ATTN_SEG // ATTN_SUB)
    qpos = q0 + lax.broadcasted_iota(jnp.int32, (1, tq), 1)
    nt_dims = (((1,), (1,)), ((), ()))

    def score_tile(t, c):
        r0 = pl.multiple_of(t * ATTN_SUB, ATTN_SUB)
        ki = ki_ref[pl.ds(r0, ATTN_SUB), :]
        acc = jnp.zeros((ATTN_SUB, tq), jnp.float32)
        for h in range(N_IDX_HEADS):
            s = lax.dot_general(ki, qi_ref[h], nt_dims, preferred_element_type=jnp.float32)
            acc = acc + jnp.maximum(s, 0.0) * wit_ref[h:h + 1, :]
        kpos = r0 + lax.broadcasted_iota(jnp.int32, (ATTN_SUB, 1), 0)
        bits = lax.bitcast_convert_type(acc, jnp.int32)
        skey = jnp.where(bits < 0, bits ^ jnp.int32(0x7FFFFFFF), bits)
        skey = jnp.where(acc == 0.0, 0, skey)
        key_ref[pl.ds(r0, ATTN_SUB), :] = jnp.where(kpos <= qpos, skey, KEY_NEG_INF)
        return c

    lax.fori_loop(0, ntile, score_tile, 0)

    def count(pred):
        def tile(t, cnt):
            r0 = pl.multiple_of(t * ATTN_SUB, ATTN_SUB)
            kk = key_ref[pl.ds(r0, ATTN_SUB), :]
            kpos = r0 + lax.broadcasted_iota(jnp.int32, (ATTN_SUB, 1), 0)
            return cnt + _slab_sum(pred(kk, kpos).astype(jnp.int32))
        cnt = lax.fori_loop(0, ntile, tile, jnp.zeros((8, tq), jnp.int32))
        return jnp.sum(cnt, axis=0, keepdims=True)

    def bit_pass(p, t_u):
        cand_u = t_u | lax.shift_left(jnp.int32(1), 31 - p)
        cand_s = cand_u ^ jnp.int32(INT32_MIN)
        c = count(lambda kk, kpos: kk >= cand_s)
        return jnp.where(c >= topk, cand_u, t_u)

    t_u = lax.fori_loop(0, 32, bit_pass, jnp.zeros((1, tq), jnp.int32))
    t_s = t_u ^ jnp.int32(INT32_MIN)

    need = topk - count(lambda kk, kpos: kk > t_s)
    surplus = (count(lambda kk, kpos: kk >= t_s) > topk) & (t_s > KEY_NEG_INF)
    tie_ref[...] = jnp.full((1, tq), 2 ** idx_bits, jnp.int32)

    @pl.when(jnp.max(surplus.astype(jnp.int32)) > 0)
    def _():
        def idx_pass(p, lo):
            cand = lo | lax.shift_left(jnp.int32(1), idx_bits - 1 - p)
            c = count(lambda kk, kpos: (kk == t_s) & (kpos < cand))
            return jnp.where(c < need, cand, lo)

        tie_ref[...] = lax.fori_loop(0, idx_bits, idx_pass, jnp.zeros((1, tq), jnp.int32))

    last_tie = tie_ref[...]

    def bias_tile(t, c):
        r0 = pl.multiple_of(t * ATTN_SUB, ATTN_SUB)
        kk = key_ref[pl.ds(r0, ATTN_SUB), :]
        kpos = r0 + lax.broadcasted_iota(jnp.int32, (ATTN_SUB, 1), 0)
        sel = (kk > t_s) | ((kk == t_s) & (kpos <= last_tie))
        sel = sel & (kk > KEY_NEG_INF)
        bias_ref[pl.ds(r0, ATTN_SUB), :] = jnp.where(sel, 0.0, MASK_NEG)
        return c

    lax.fori_loop(0, ntile, bias_tile, 0)

    scale = HEAD_DIM ** -0.5

    def head_group_body(hg, c):
        heads = [hg * ATTN_HEADS_PER_STEP + a for a in range(ATTN_HEADS_PER_STEP)]

        def seg_body(s, carry):
            r0 = pl.multiple_of(s * ATTN_SEG, ATTN_SEG)
            bias = bias_ref[pl.ds(r0, ATTN_SEG), :]
            out = []
            for h, (m, l, acc) in zip(heads, carry):
                kh = k_ref[h, pl.ds(r0, ATTN_SEG), :]
                logits = lax.dot_general(kh, q_ref[h], nt_dims, preferred_element_type=jnp.float32) * scale
                logits = logits + bias
                m_new = jnp.maximum(m, jnp.max(logits, axis=0, keepdims=True))
                alpha = jnp.exp(m - m_new)
                p = jnp.exp(logits - m_new)
                l = alpha * l + jnp.sum(p, axis=0, keepdims=True)
                acc = alpha * acc + jnp.dot(vt_ref[h, s], p.astype(jnp.bfloat16),
                                            preferred_element_type=jnp.float32)
                out.append((m_new, l, acc))
            return tuple(out)

        init = tuple((jnp.full((1, tq), MASK_NEG, jnp.float32), jnp.zeros((1, tq), jnp.float32),
                      jnp.zeros((HEAD_DIM, tq), jnp.float32)) for _ in heads)
        for h, (_, l, acc) in zip(heads, lax.fori_loop(0, nseg, seg_body, init)):
            o_ref[h] = (acc / l).T.astype(o_ref.dtype)
        return c

    lax.fori_loop(0, N_HEADS // ATTN_HEADS_PER_STEP, head_group_body, 0)


def _prompt_attention(q_t, k_t, v_b, qi_t, ki_b, wi):
    B, _, T, _ = q_t.shape
    topk = min(TOPK_MAX, T // 4)
    nseg = T // ATTN_SEG
    bf = jnp.bfloat16
    vt = v_b.reshape(B, nseg, ATTN_SEG, N_HEADS, HEAD_DIM).transpose(0, 3, 1, 4, 2)
    wit = wi.transpose(0, 2, 1)
    body = functools.partial(_prompt_attn_body, topk=topk, idx_bits=int(np.ceil(np.log2(T))))
    return pl.pallas_call(
        body,
        grid=(B, T // ATTN_TQ),
        in_specs=[
            pl.BlockSpec((None, N_IDX_HEADS, ATTN_TQ, IDX_DIM), lambda b, i: (b, 0, i, 0)),
            pl.BlockSpec((None, T, IDX_DIM), lambda b, i: (b, 0, 0)),
            pl.BlockSpec((None, N_IDX_HEADS, ATTN_TQ), lambda b, i: (b, 0, i)),
            pl.BlockSpec((None, N_HEADS, ATTN_TQ, HEAD_DIM), lambda b, i: (b, 0, i, 0)),
            pl.BlockSpec((None, N_HEADS, T, HEAD_DIM), lambda b, i: (b, 0, 0, 0)),
            pl.BlockSpec((None, N_HEADS, nseg, HEAD_DIM, ATTN_SEG), lambda b, i: (b, 0, 0, 0, 0)),
        ],
        out_specs=pl.BlockSpec((None, N_HEADS, ATTN_TQ, HEAD_DIM), lambda b, i: (b, 0, i, 0)),
        out_shape=jax.ShapeDtypeStruct((B, N_HEADS, T, HEAD_DIM), bf),
        scratch_shapes=[pltpu.VMEM((T, ATTN_TQ), jnp.int32), pltpu.VMEM((T, ATTN_TQ), jnp.float32),
                        pltpu.VMEM((1, ATTN_TQ), jnp.int32)],
        compiler_params=pltpu.CompilerParams(
            dimension_semantics=("arbitrary", "arbitrary"),
            vmem_limit_bytes=VMEM_LIMIT_BYTES),
        name="prompt_attn",
    )(qi_t, ki_b, wit, q_t, k_t, vt)


SAMPLE_IDX_PAGES = 8
SAMPLE_KV_PAGES = 4
SAMPLE_NB = 2


def _score_key(score):
    bits = lax.bitcast_convert_type(score, jnp.int32)
    key = jnp.where(bits < 0, bits ^ jnp.int32(0x7FFFFFFF), bits)
    return jnp.where(score == 0.0, 0, key)


def _sample_select_body(pt_ref, qi_ref, w_ref, kin_ref, exp_ref, *rest, topk, n_pages, idx_bits):
    del pt_ref
    kid_refs = rest[:SAMPLE_IDX_PAGES]
    sel_ref, key_scr, tie_scr = rest[SAMPLE_IDX_PAGES:]
    g = pl.program_id(1)
    n_q = key_scr.shape[1]
    past = n_pages * PAGE_SIZE
    qi = qi_ref[...]
    wcol = w_ref[...]
    qpos = past + lax.broadcasted_iota(jnp.int32, (n_q, 1), 0)
    lane = lax.broadcasted_iota(jnp.int32, (1, PAGE_SIZE), 1)

    def page_keys(ki_page, page):
        s = lax.dot_general(qi, ki_page.astype(jnp.bfloat16), (((1,), (1,)), ((), ())),
                            preferred_element_type=jnp.float32)
        score = _slab_sum_n(jnp.maximum(s, 0.0) * wcol, n_q)
        kpos = page * PAGE_SIZE + lane
        return jnp.where(kpos <= qpos, _score_key(score), KEY_NEG_INF)

    for j in range(SAMPLE_IDX_PAGES):
        page = g * SAMPLE_IDX_PAGES + j
        key_scr[page] = page_keys(kid_refs[j][...], page)

    @pl.when(g == 0)
    def _():
        key_scr[n_pages] = page_keys(kin_ref[...], n_pages)

    @pl.when(g == pl.num_programs(1) - 1)
    def _():
        kk = key_scr[...]
        kpos = (lax.broadcasted_iota(jnp.int32, kk.shape, 0) * PAGE_SIZE
                + lax.broadcasted_iota(jnp.int32, kk.shape, 2))

        def count(m):
            return jnp.sum(jnp.sum(m.astype(jnp.int32), axis=0), axis=1, keepdims=True)

        def bit_pass(p, t_u):
            cand_u = t_u | lax.shift_left(jnp.int32(1), 31 - p)
            cand_s = cand_u ^ jnp.int32(INT32_MIN)
            return jnp.where(count(kk >= cand_s[None]) >= topk, cand_u, t_u)

        t_u = lax.fori_loop(0, 32, bit_pass, jnp.zeros((n_q, 1), jnp.int32))
        t_s = (t_u ^ jnp.int32(INT32_MIN))[None]
        need = topk - count(kk > t_s)
        surplus = (count(kk >= t_s) > topk) & (t_s[0] > KEY_NEG_INF)
        tie_scr[...] = jnp.full((n_q, 1), 2 ** idx_bits, jnp.int32)

        @pl.when(jnp.max(surplus.astype(jnp.int32)) > 0)
        def _():
            def idx_pass(p, lo):
                cand = lo | lax.shift_left(jnp.int32(1), idx_bits - 1 - p)
                return jnp.where(count((kk == t_s) & (kpos < cand[None])) < need, cand, lo)

            tie_scr[...] = lax.fori_loop(0, idx_bits, idx_pass, jnp.zeros((n_q, 1), jnp.int32))

        last_tie = tie_scr[...]
        sel = (kk > t_s) | ((kk == t_s) & (kpos <= last_tie[None]))
        sel = sel & (kk > KEY_NEG_INF)
        key_scr[...] = sel.astype(jnp.int32)

        def expand_page(p, c):
            tile = key_scr[p].astype(jnp.float32).astype(jnp.bfloat16)
            cols = jnp.dot(tile, exp_ref[...], preferred_element_type=jnp.float32)
            sel_ref[p] = (cols - 1.0) * (-MASK_NEG)
            return c

        lax.fori_loop(0, n_pages + 1, expand_page, 0)


def _slab_sum_n(x, n):
    parts = [x[r * n:(r + 1) * n] for r in range(x.shape[0] // n)]
    while len(parts) > 1:
        parts = [parts[a] + parts[a + 1] for a in range(0, len(parts), 2)]
    return parts[0]


def _sample_attn_body(pt_ref, q_ref, sel_ref, seln_ref, hm_ref, kn_ref, vn_ref, *rest):
    del pt_ref
    n_pg = SAMPLE_NB * SAMPLE_KV_PAGES
    k_refs = rest[:n_pg]
    v_refs = rest[n_pg:2 * n_pg]
    o_ref, m_scr, l_scr, acc_scr = rest[2 * n_pg:]
    g = pl.program_id(1)
    n_q = seln_ref.shape[2]
    bf = jnp.bfloat16
    scale = HEAD_DIM ** -0.5

    @pl.when(g == 0)
    def _():
        m_scr[...] = jnp.full(m_scr.shape, MASK_NEG, jnp.float32)
        l_scr[...] = jnp.zeros(l_scr.shape, jnp.float32)
        acc_scr[...] = jnp.zeros(acc_scr.shape, jnp.float32)

    def pages_update(nb, pages):
        qall = q_ref[nb]
        logits = []
        for k_page, _, sel_tile, ncol in pages:
            kp = k_page.reshape(-1, HEAD_DIM).astype(bf)
            bias = jnp.concatenate([sel_tile[:, :ncol]] * N_HEADS, axis=0) + hm_ref[:, :ncol]
            logits.append(lax.dot_general(qall, kp, (((1,), (1,)), ((), ())),
                                          preferred_element_type=jnp.float32) * scale + bias)
        m = m_scr[nb]
        m_new = m
        for lg in logits:
            m_new = jnp.maximum(m_new, jnp.max(lg, axis=1, keepdims=True))
        alpha = jnp.exp(m - m_new)
        l_new = alpha * l_scr[nb]
        acc = alpha * acc_scr[nb]
        for lg, (_, v_page, _, _) in zip(logits, pages):
            p = jnp.exp(lg - m_new)
            l_new = l_new + jnp.sum(p, axis=1, keepdims=True)
            acc = acc + jnp.dot(p.astype(bf), v_page.reshape(-1, HEAD_DIM).astype(bf),
                                preferred_element_type=jnp.float32)
        l_scr[nb] = l_new
        acc_scr[nb] = acc
        m_scr[nb] = m_new

    @pl.when(g == 0)
    def _():
        for nb in range(SAMPLE_NB):
            pages_update(nb, [(kn_ref[nb], vn_ref[nb], seln_ref[nb, 0], n_q * N_HEADS)])

    for nb in range(SAMPLE_NB):
        pages_update(nb, [(k_refs[nb * SAMPLE_KV_PAGES + j][...], v_refs[nb * SAMPLE_KV_PAGES + j][...],
                           sel_ref[nb, j], PAGE_SIZE * N_HEADS) for j in range(SAMPLE_KV_PAGES)])

    @pl.when(g == pl.num_programs(1) - 1)
    def _():
        o_ref[...] = acc_scr[...] / l_scr[...]


def _sample_attention(q, k, v, qi, ki, wi, *, layer, cache_k, cache_v, cache_kidx, page_table):
    Bd, S = q.shape[:2]
    n_pages = page_table.shape[1]
    past = n_pages * PAGE_SIZE
    topk = min(TOPK_MAX, (past + S) // 4)
    bf = jnp.bfloat16
    rows = N_HEADS * S
    qi_r = qi.astype(bf).transpose(0, 2, 1, 3).reshape(Bd, N_IDX_HEADS * S, IDX_DIM)
    w_r = jnp.broadcast_to(wi.transpose(0, 2, 1).reshape(Bd, N_IDX_HEADS * S, 1),
                           (Bd, N_IDX_HEADS * S, PAGE_SIZE))
    ki_new = jnp.pad(ki, ((0, 0), (0, PAGE_SIZE - S), (0, 0)))
    idx_bits = int(np.ceil(np.log2(past + PAGE_SIZE)))
    ncol = PAGE_SIZE * N_HEADS
    col = np.arange(ncol)
    expand = jnp.asarray(col[None, :] // N_HEADS == np.arange(PAGE_SIZE)[:, None], bf)

    def kid_spec(j):
        return pl.BlockSpec((None, None, PAGE_SIZE, IDX_DIM),
                            lambda b, g, pt: (layer, pt[b, g * SAMPLE_IDX_PAGES + j], 0, 0))

    sel = pl.pallas_call(
        functools.partial(_sample_select_body, topk=topk, n_pages=n_pages, idx_bits=idx_bits),
        grid_spec=pltpu.PrefetchScalarGridSpec(
            num_scalar_prefetch=1,
            grid=(Bd, n_pages // SAMPLE_IDX_PAGES),
            in_specs=[
                pl.BlockSpec((None, N_IDX_HEADS * S, IDX_DIM), lambda b, g, pt: (b, 0, 0)),
                pl.BlockSpec((None, N_IDX_HEADS * S, PAGE_SIZE), lambda b, g, pt: (b, 0, 0)),
                pl.BlockSpec((None, PAGE_SIZE, IDX_DIM), lambda b, g, pt: (b, 0, 0)),
                pl.BlockSpec((PAGE_SIZE, ncol), lambda b, g, pt: (0, 0)),
            ] + [kid_spec(j) for j in range(SAMPLE_IDX_PAGES)],
            out_specs=pl.BlockSpec((None, n_pages + 1, S, ncol), lambda b, g, pt: (b, 0, 0, 0)),
            scratch_shapes=[pltpu.VMEM((n_pages + 1, S, PAGE_SIZE), jnp.int32), pltpu.VMEM((S, 1), jnp.int32)],
        ),
        out_shape=jax.ShapeDtypeStruct((Bd, n_pages + 1, S, ncol), jnp.float32),
        compiler_params=pltpu.CompilerParams(
            dimension_semantics=("arbitrary", "arbitrary"),
            vmem_limit_bytes=VMEM_LIMIT_BYTES),
        name="sample_select",
    )(page_table, qi_r, w_r, ki_new, expand, *([cache_kidx] * SAMPLE_IDX_PAGES))

    q_r = q.astype(bf).transpose(0, 2, 1, 3).reshape(Bd, rows, HEAD_DIM)
    head_mask = jnp.asarray(np.where(col[None, :] % N_HEADS == np.arange(rows)[:, None] // S, 0.0, MASK_NEG),
                            jnp.float32)

    nb = SAMPLE_NB
    assert Bd % nb == 0

    def kv_spec(a, j):
        return pl.BlockSpec((None, None, PAGE_SIZE, N_HEADS, HEAD_DIM),
                            lambda b, g, pt: (layer, pt[b * nb + a, g * SAMPLE_KV_PAGES + j], 0, 0, 0))

    kv_specs = [kv_spec(a, j) for a in range(nb) for j in range(SAMPLE_KV_PAGES)]
    out = pl.pallas_call(
        _sample_attn_body,
        grid_spec=pltpu.PrefetchScalarGridSpec(
            num_scalar_prefetch=1,
            grid=(Bd // nb, n_pages // SAMPLE_KV_PAGES),
            in_specs=[
                pl.BlockSpec((nb, rows, HEAD_DIM), lambda b, g, pt: (b, 0, 0)),
                pl.BlockSpec((nb, SAMPLE_KV_PAGES, S, ncol), lambda b, g, pt: (b, g, 0, 0)),
                pl.BlockSpec((nb, 1, S, ncol), lambda b, g, pt: (b, n_pages, 0, 0)),
                pl.BlockSpec((rows, ncol), lambda b, g, pt: (0, 0)),
                pl.BlockSpec((nb, S, N_HEADS, HEAD_DIM), lambda b, g, pt: (b, 0, 0, 0)),
                pl.BlockSpec((nb, S, N_HEADS, HEAD_DIM), lambda b, g, pt: (b, 0, 0, 0)),
            ] + kv_specs * 2,
            out_specs=pl.BlockSpec((nb, rows, HEAD_DIM), lambda b, g, pt: (b, 0, 0)),
            scratch_shapes=[pltpu.VMEM((nb, rows, 1), jnp.float32), pltpu.VMEM((nb, rows, 1), jnp.float32),
                            pltpu.VMEM((nb, rows, HEAD_DIM), jnp.float32)],
        ),
        out_shape=jax.ShapeDtypeStruct((Bd, rows, HEAD_DIM), jnp.float32),
        compiler_params=pltpu.CompilerParams(
            dimension_semantics=("arbitrary", "arbitrary"),
            vmem_limit_bytes=VMEM_LIMIT_BYTES),
        name="sample_attn",
    )(page_table, q_r, sel, sel, head_mask, k, v,
      *([cache_k] * (nb * SAMPLE_KV_PAGES)), *([cache_v] * (nb * SAMPLE_KV_PAGES)))
    return out.reshape(Bd, N_HEADS, S, HEAD_DIM).transpose(0, 2, 1, 3)


W_GROUP = 1024
COL_Q, COL_K, COL_V, COL_IDX = 0, 1024, 2048, 3072
COL_XC, COL_BG, COL_CG = 4096, 5120, 6144
COL_GA, COL_GC, COL_END = 7168, 9216, 11264
IDX_COLS = N_IDX_HEADS * IDX_DIM + IDX_DIM + N_IDX_HEADS
PROJ_TM = 1024
OUT_TM = 256


def _repack_w_in(w_in):
    n_qkv = 3 * ATTN_DIM
    pad = jnp.zeros((w_in.shape[0], W_GROUP - IDX_COLS), w_in.dtype)
    return jnp.concatenate([w_in[:, :n_qkv + IDX_COLS], pad, w_in[:, n_qkv + IDX_COLS:]],
                           axis=1).astype(jnp.bfloat16)


def _rope_tables(pos, half, width):
    inv = 1.0 / jnp.power(ROPE_THETA, jnp.arange(half, dtype=jnp.float32) / half)
    ang = pos.astype(jnp.float32)[:, None] * inv[None, :]
    cos, sin = jnp.cos(ang), jnp.sin(ang)
    reps = width // (2 * half)
    return (jnp.tile(jnp.concatenate([cos, cos], axis=1), (1, reps)),
            jnp.tile(jnp.concatenate([-sin, sin], axis=1), (1, reps)))


def _group_param(param, rows_per_group, tm):
    D = param.shape[-1]
    if rows_per_group % tm == 0:
        return param[:, None, :], pl.BlockSpec((None, 1, D), lambda i, *_: (i * tm // rows_per_group, 0, 0))
    return jnp.repeat(param, rows_per_group, axis=0), pl.BlockSpec((tm, D), lambda i, *_: (i, 0))


def _cparams(n_axes):
    return pltpu.CompilerParams(dimension_semantics=("arbitrary",) * n_axes, vmem_limit_bytes=VMEM_LIMIT_BYTES)


def _modnorm(x, w, scale, shift):
    y = x * lax.rsqrt(jnp.mean(x * x, axis=-1, keepdims=True) + EPS) * w
    return y * (1.0 + scale) + shift


def _norm1_body(x_ref, sc_ref, sh_ref, w_ref, o_ref):
    o_ref[...] = _modnorm(x_ref[...], w_ref[...], sc_ref[...], sh_ref[...]).astype(o_ref.dtype)


def _norm1(x, norm_w, sc, sh, rows_per_group):
    N, D = x.shape
    tm = min(512, N)
    sc_arr, sc_spec = _group_param(sc, rows_per_group, tm)
    sh_arr, sh_spec = _group_param(sh, rows_per_group, tm)
    return pl.pallas_call(
        _norm1_body,
        grid=(N // tm,),
        in_specs=[pl.BlockSpec((tm, D), lambda i: (i, 0)), sc_spec, sh_spec,
                  pl.BlockSpec((1, D), lambda i: (0, 0))],
        out_specs=pl.BlockSpec((tm, D), lambda i: (i, 0)),
        out_shape=jax.ShapeDtypeStruct((N, D), jnp.bfloat16),
        compiler_params=_cparams(1),
        name="norm1",
    )(x, sc_arr, sh_arr, norm_w[None, :])


def _qkv_body(h_ref, w_ref, nw_ref, cos_ref, sin_ref, qt_ref, k_ref, kt_ref, v_ref, vb_ref):
    j = pl.program_id(1)
    acc = jnp.dot(h_ref[...], w_ref[...], preferred_element_type=jnp.float32)

    def norm_rope(nw):
        heads = []
        for hd in range(N_HEADS):
            xh = acc[:, hd * HEAD_DIM:(hd + 1) * HEAD_DIM]
            y = xh * lax.rsqrt(jnp.mean(xh * xh, axis=-1, keepdims=True) + EPS) * nw
            heads.append(y * cos_ref[...] + pltpu.roll(y, HEAD_DIM // 2, 1) * sin_ref[...])
        return heads

    @pl.when(j == 0)
    def _():
        for hd, y in enumerate(norm_rope(nw_ref[0:1, :])):
            qt_ref[hd] = y.astype(qt_ref.dtype)

    @pl.when(j == 1)
    def _():
        for hd, y in enumerate(norm_rope(nw_ref[1:2, :])):
            k_ref[:, hd * HEAD_DIM:(hd + 1) * HEAD_DIM] = y
            kt_ref[hd] = y.astype(kt_ref.dtype)

    @pl.when(j == 2)
    def _():
        v_ref[...] = acc
        vb_ref[...] = acc.astype(vb_ref.dtype)


def _qkv_proj(h, w_b, q_norm_w, k_norm_w, cos, sin, n_seq, seq_len):
    N, D = h.shape
    tm = min(PROJ_TM, seq_len)
    tiles = seq_len // tm
    bf = jnp.bfloat16
    nw = jnp.zeros((8, HEAD_DIM), jnp.float32).at[0].set(q_norm_w).at[1].set(k_norm_w)
    head_major = pl.BlockSpec((None, N_HEADS, tm, HEAD_DIM), lambda i, j: (i // tiles, 0, i % tiles, 0))
    rows = pl.BlockSpec((tm, ATTN_DIM), lambda i, j: (i, 0))
    table = pl.BlockSpec((tm, HEAD_DIM), lambda i, j: (i % tiles, 0))
    return pl.pallas_call(
        _qkv_body,
        grid=(N // tm, 3),
        in_specs=[pl.BlockSpec((tm, D), lambda i, j: (i, 0)),
                  pl.BlockSpec((D, W_GROUP), lambda i, j: (0, j)),
                  pl.BlockSpec((8, HEAD_DIM), lambda i, j: (0, 0)), table, table],
        out_specs=[head_major, rows, head_major, rows, rows],
        out_shape=[jax.ShapeDtypeStruct((n_seq, N_HEADS, seq_len, HEAD_DIM), bf),
                   jax.ShapeDtypeStruct((N, ATTN_DIM), jnp.float32),
                   jax.ShapeDtypeStruct((n_seq, N_HEADS, seq_len, HEAD_DIM), bf),
                   jax.ShapeDtypeStruct((N, ATTN_DIM), jnp.float32),
                   jax.ShapeDtypeStruct((N, ATTN_DIM), bf)],
        compiler_params=_cparams(2),
        name="qkv_proj",
    )(h, w_b, nw, cos, sin)


def _idx_body(h_ref, w_ref, cos_ref, sin_ref, qit_ref, ki_ref, kib_ref, wi_ref):
    acc = jnp.dot(h_ref[...], w_ref[...], preferred_element_type=jnp.float32)
    first_half = lax.broadcasted_iota(jnp.int32, (1, 2 * IDX_DIM), 1) % IDX_DIM < IDX_DIM // 2

    def rope_pair(x):
        swapped = jnp.where(first_half, pltpu.roll(x, 2 * IDX_DIM - IDX_DIM // 2, 1),
                            pltpu.roll(x, IDX_DIM // 2, 1))
        return x * cos_ref[...] + swapped * sin_ref[...]

    for pair in range(N_IDX_HEADS // 2):
        y = rope_pair(acc[:, pair * 2 * IDX_DIM:(pair + 1) * 2 * IDX_DIM])
        qit_ref[2 * pair] = y[:, :IDX_DIM].astype(qit_ref.dtype)
        qit_ref[2 * pair + 1] = y[:, IDX_DIM:].astype(qit_ref.dtype)
    tail = acc[:, N_IDX_HEADS * IDX_DIM:N_IDX_HEADS * IDX_DIM + 2 * IDX_DIM]
    ki = rope_pair(tail)[:, :IDX_DIM]
    ki_ref[...] = ki
    kib_ref[...] = ki.astype(kib_ref.dtype)
    wi_ref[...] = tail[:, IDX_DIM:IDX_DIM + N_IDX_HEADS] * ((N_IDX_HEADS ** -0.5) * (IDX_DIM ** -0.5))


def _idx_proj(h, w_b, cos, sin, n_seq, seq_len):
    N, D = h.shape
    tm = min(PROJ_TM, seq_len)
    tiles = seq_len // tm
    bf = jnp.bfloat16
    table = pl.BlockSpec((tm, 2 * IDX_DIM), lambda i: (i % tiles, 0))
    return pl.pallas_call(
        _idx_body,
        grid=(N // tm,),
        in_specs=[pl.BlockSpec((tm, D), lambda i: (i, 0)),
                  pl.BlockSpec((D, W_GROUP), lambda i: (0, COL_IDX // W_GROUP)), table, table],
        out_specs=[pl.BlockSpec((None, N_IDX_HEADS, tm, IDX_DIM), lambda i: (i // tiles, 0, i % tiles, 0)),
                   pl.BlockSpec((tm, IDX_DIM), lambda i: (i, 0)),
                   pl.BlockSpec((tm, IDX_DIM), lambda i: (i, 0)),
                   pl.BlockSpec((tm, N_IDX_HEADS), lambda i: (i, 0))],
        out_shape=[jax.ShapeDtypeStruct((n_seq, N_IDX_HEADS, seq_len, IDX_DIM), bf),
                   jax.ShapeDtypeStruct((N, IDX_DIM), jnp.float32),
                   jax.ShapeDtypeStruct((N, IDX_DIM), bf),
                   jax.ShapeDtypeStruct((N, N_IDX_HEADS), jnp.float32)],
        compiler_params=_cparams(1),
        name="idx_proj",
    )(h, w_b, cos, sin)


CONV_TN = 512


def _conv_body(h_ref, wx_ref, wb_ref, wc_ref, cw_ref, st_ref, y_ref, tail_ref, carry, *, tiles):
    i = pl.program_id(0)
    j = pl.program_id(1)
    h = h_ref[...]
    dot = functools.partial(jnp.dot, preferred_element_type=jnp.float32)
    u = dot(h, wc_ref[...]) * dot(h, wx_ref[...])
    bg = dot(h, wb_ref[...])

    @pl.when(i % tiles == 0)
    def _():
        carry[j] = st_ref[...]

    prev = carry[j]
    row = lax.broadcasted_iota(jnp.int32, (u.shape[0], 1), 0)
    u1 = jnp.where(row == 0, prev[1:2, :], pltpu.roll(u, 1, 0))
    u2 = jnp.where(row == 0, prev[0:1, :], jnp.where(row == 1, prev[1:2, :], pltpu.roll(u, 2, 0)))
    y_ref[...] = (bg * (cw_ref[0:1, :] * u2 + cw_ref[1:2, :] * u1 + cw_ref[2:3, :] * u)).astype(y_ref.dtype)
    last = u[u.shape[0] - (CONV_WIDTH - 1):, :]
    carry[j] = last
    tail_ref[...] = last


def _conv_proj(h, w_b, conv_w, state, n_seq, seq_len):
    N, D = h.shape
    tm = min(PROJ_TM, seq_len)
    tiles = seq_len // tm
    n_ct = CONV_DIM // CONV_TN

    def w_spec(col):
        return pl.BlockSpec((D, CONV_TN), lambda i, j: (0, col // CONV_TN + j))

    conv, tails = pl.pallas_call(
        functools.partial(_conv_body, tiles=tiles),
        grid=(N // tm, n_ct),
        in_specs=[pl.BlockSpec((tm, D), lambda i, j: (i, 0)),
                  w_spec(COL_XC), w_spec(COL_BG), w_spec(COL_CG),
                  pl.BlockSpec((CONV_WIDTH, CONV_TN), lambda i, j: (0, j)),
                  pl.BlockSpec((None, CONV_WIDTH - 1, CONV_TN), lambda i, j: (i // tiles, 0, j))],
        out_specs=[pl.BlockSpec((tm, CONV_TN), lambda i, j: (i, j)),
                   pl.BlockSpec((None, CONV_WIDTH - 1, CONV_TN), lambda i, j: (i, 0, j))],
        out_shape=[jax.ShapeDtypeStruct((N, CONV_DIM), jnp.bfloat16),
                   jax.ShapeDtypeStruct((N // tm, CONV_WIDTH - 1, CONV_DIM), jnp.float32)],
        scratch_shapes=[pltpu.VMEM((n_ct, CONV_WIDTH - 1, CONV_TN), jnp.float32)],
        compiler_params=_cparams(2),
        name="conv_proj",
    )(h, w_b, w_b, w_b, conv_w, state)
    return conv, tails[tiles - 1::tiles]


MERGE_TN = 512


def _merge_body(h_ref, a_ref, c_ref, wga_ref, wgc_ref, wba_ref, wbc_ref, o_ref):
    bf = jnp.bfloat16
    dot = functools.partial(jnp.dot, preferred_element_type=jnp.float32)
    h = h_ref[...]
    a_br = dot(a_ref[0], wba_ref[0:HEAD_DIM, :].astype(bf))
    for hd in range(1, N_HEADS):
        a_br = a_br + dot(a_ref[hd], wba_ref[hd * HEAD_DIM:(hd + 1) * HEAD_DIM, :].astype(bf))
    c_br = dot(c_ref[...], wbc_ref[...].astype(bf))
    merged = jax.nn.sigmoid(dot(h, wga_ref[...])) * a_br + jax.nn.sigmoid(dot(h, wgc_ref[...])) * c_br
    o_ref[...] = merged.astype(o_ref.dtype)


def _merge_proj(h, attn_t, conv, w_b, w_branch_attn, w_branch_conv):
    N, D = h.shape
    seq_len = attn_t.shape[2]
    tm = min(PROJ_TM, seq_len)
    tiles = seq_len // tm
    return pl.pallas_call(
        _merge_body,
        grid=(N // tm, D // MERGE_TN),
        in_specs=[pl.BlockSpec((tm, D), lambda i, j: (i, 0)),
                  pl.BlockSpec((None, N_HEADS, tm, HEAD_DIM), lambda i, j: (i // tiles, 0, i % tiles, 0)),
                  pl.BlockSpec((tm, CONV_DIM), lambda i, j: (i, 0)),
                  pl.BlockSpec((D, MERGE_TN), lambda i, j: (0, COL_GA // MERGE_TN + j)),
                  pl.BlockSpec((D, MERGE_TN), lambda i, j: (0, COL_GC // MERGE_TN + j)),
                  pl.BlockSpec((ATTN_DIM, MERGE_TN), lambda i, j: (0, j)),
                  pl.BlockSpec((CONV_DIM, MERGE_TN), lambda i, j: (0, j))],
        out_specs=pl.BlockSpec((tm, MERGE_TN), lambda i, j: (i, j)),
        out_shape=jax.ShapeDtypeStruct((N, D), jnp.bfloat16),
        compiler_params=_cparams(2),
        name="merge_proj",
    )(h, attn_t, conv, w_b, w_b, w_branch_attn, w_branch_conv)


def _out_body(m_ref, x_ref, g1_ref, sc_ref, sh_ref, nw_ref, wo_ref, rh_ref, rl_ref, x1_ref, h2_ref, lg_ref):
    dot = functools.partial(jnp.dot, preferred_element_type=jnp.float32)
    x1 = x_ref[...] + g1_ref[...] * dot(m_ref[...], wo_ref[...])
    x1_ref[...] = x1
    h2 = _modnorm(x1, nw_ref[...], sc_ref[...], sh_ref[...])
    h2_ref[...] = h2
    h_hi, h_lo = _hi_lo(h2)
    lg_ref[...] = dot(h_hi, rh_ref[...]) + (dot(h_lo, rh_ref[...]) + dot(h_hi, rl_ref[...]))


def _out_proj(merged, x, g1, sc2, sh2, norm2_w, w_out_b, w_router, rows_per_group):
    N, D = x.shape
    tm = min(OUT_TM, N)
    g1_arr, g1_spec = _group_param(g1, rows_per_group, tm)
    sc_arr, sc_spec = _group_param(sc2, rows_per_group, tm)
    sh_arr, sh_spec = _group_param(sh2, rows_per_group, tm)
    r_hi, r_lo = _hi_lo(w_router)
    n_r = w_router.shape[1]
    rows = pl.BlockSpec((tm, D), lambda i: (i, 0))
    return pl.pallas_call(
        _out_body,
        grid=(N // tm,),
        in_specs=[rows, rows, g1_spec, sc_spec, sh_spec,
                  pl.BlockSpec((1, D), lambda i: (0, 0)),
                  pl.BlockSpec((D, D), lambda i: (0, 0)),
                  pl.BlockSpec((D, n_r), lambda i: (0, 0)),
                  pl.BlockSpec((D, n_r), lambda i: (0, 0))],
        out_specs=[rows, rows, pl.BlockSpec((tm, n_r), lambda i: (i, 0))],
        out_shape=[jax.ShapeDtypeStruct((N, D), jnp.float32), jax.ShapeDtypeStruct((N, D), jnp.float32),
                   jax.ShapeDtypeStruct((N, n_r), jnp.float32)],
        compiler_params=_cparams(1),
        name="out_proj",
    )(merged, x, g1_arr, sc_arr, sh_arr, norm2_w[None, :], w_out_b, r_hi, r_lo)


def _short_conv(xc, bg, cg, conv_w, buf):
    u = cg * xc
    ext = jnp.concatenate([buf, u], axis=1)
    T = u.shape[1]
    y = sum(conv_w[j] * ext[:, j:j + T] for j in range(CONV_WIDTH))
    return bg * y, ext[:, -(CONV_WIDTH - 1):]


MOE_TM = 256
COMBINE_TM = 128


def _route(r_logits, b_gr, b_er):
    n = r_logits.shape[0]
    g_logits = r_logits[:, :N_GROUPS]
    g_prob = jax.nn.softmax(g_logits, axis=-1)
    g_sel = jnp.argmax(g_logits + b_gr, axis=-1)
    g_w = jnp.take_along_axis(g_prob, g_sel[:, None], axis=1)[:, 0]
    e_logits = r_logits[:, N_GROUPS:N_GROUPS * (1 + EXPERTS_PER_GROUP)].reshape(n, N_GROUPS, EXPERTS_PER_GROUP)
    e_in = jnp.take_along_axis(e_logits, g_sel[:, None, None], axis=1)[:, 0]
    _, e_idx = lax.top_k(e_in + b_er[g_sel], TOPK_EXPERTS)
    e_p = jax.nn.softmax(jnp.take_along_axis(e_in, e_idx, axis=1), axis=-1)
    eid = (g_sel[:, None] * EXPERTS_PER_GROUP + e_idx).astype(jnp.int32)
    return eid, g_w[:, None] * e_p


def _moe_plan(eid, wgt, n_tiles):
    n_exp = N_GROUPS * EXPERTS_PER_GROUP
    a = eid.size
    e_flat = eid.reshape(a)
    onehot = (e_flat[:, None] == jnp.arange(n_exp, dtype=jnp.int32)[None, :]).astype(jnp.int32)
    running = jnp.cumsum(onehot, axis=0)
    counts = running[-1]
    padded = (counts + MOE_TM - 1) // MOE_TM * MOE_TM
    pad_end = jnp.cumsum(padded)
    pad_start = pad_end - padded
    cnt_start = jnp.cumsum(counts) - counts
    pos = pad_start[e_flat] + jnp.sum(running * onehot, axis=1) - 1
    order = jnp.argsort(e_flat, stable=True).astype(jnp.int32)
    rows = n_tiles * MOE_TM
    row = jnp.arange(rows, dtype=jnp.int32)
    tile_expert = jnp.minimum(jnp.searchsorted(pad_end, jnp.arange(n_tiles, dtype=jnp.int32) * MOE_TM, side='right'),
                              n_exp - 1).astype(jnp.int32)
    row_expert = jnp.repeat(tile_expert, MOE_TM)
    rank = row - pad_start[row_expert]
    row_valid = (rank < counts[row_expert]) & (row < pad_end[-1])
    assign = order[jnp.clip(cnt_start[row_expert] + rank, 0, a - 1)]
    row_token = jnp.where(row_valid, assign // TOPK_EXPERTS, 0)
    row_weight = jnp.where(row_valid, wgt.reshape(a)[assign], 0.0)
    n_used = (pad_end[-1] // MOE_TM).astype(jnp.int32).reshape(1)
    return tile_expert, n_used, row_token, row_weight.reshape(rows, 1), pos.astype(jnp.int32)


def _moe_body(te_ref, nu_ref, tok_ref, x_hbm, rw_ref, wg_ref, wu_ref, wd_ref, y_ref,
              buf, sem, wgb, wub, wdb):
    t = pl.program_id(0)
    n_used = nu_ref[0]
    slot = lax.rem(t, 2)
    bf = jnp.bfloat16

    def row_copy(tile, slot_, r):
        tok = tok_ref[tile * MOE_TM + r]
        return pltpu.make_async_copy(x_hbm.at[pl.ds(tok, 1), :], buf.at[slot_, pl.ds(r, 1), :], sem.at[slot_])

    def gather_start(tile, slot_):
        def body(r, c):
            row_copy(tile, slot_, r).start()
            return c
        lax.fori_loop(0, MOE_TM, body, 0, unroll=8)

    def gather_wait(tile, slot_):
        def body(r, c):
            row_copy(tile, slot_, r).wait()
            return c
        lax.fori_loop(0, MOE_TM, body, 0, unroll=8)

    @pl.when(t == 0)
    def _():
        gather_start(0, 0)

    @pl.when(t < n_used)
    def _():
        expert_changed = jnp.logical_or(t == 0, te_ref[t] != te_ref[jnp.maximum(t - 1, 0)])

        @pl.when(expert_changed)
        def _():
            wgb[...] = wg_ref[...].astype(bf)
            wub[...] = wu_ref[...].astype(bf)
            wdb[...] = wd_ref[...].astype(bf)

        gather_wait(t, slot)
        for r in range(MOE_TM):
            row_copy(t + 1, 1 - slot, r).start()
        x = buf[slot].astype(bf)
        g = jnp.dot(x, wgb[...], preferred_element_type=jnp.float32)
        u = jnp.dot(x, wub[...], preferred_element_type=jnp.float32)
        hid = jax.nn.silu(g) * u * rw_ref[...]
        y_ref[...] = jnp.dot(hid.astype(bf), wdb[...], preferred_element_type=jnp.float32)

    @pl.when(t == n_used)
    def _():
        gather_wait(t, slot)

    @pl.when(t >= n_used)
    def _():
        y_ref[...] = jnp.zeros_like(y_ref)


def _moe_experts(x_all, plan, w_g, w_u, w_d, layer, n_tiles):
    tile_expert, n_used, row_token, row_weight, _ = plan
    D = x_all.shape[1]
    F = w_g.shape[-1]

    def w_spec(k_dim, n_dim):
        return pl.BlockSpec((None, None, None, k_dim, n_dim),
                            lambda t, te, nu, tok: (layer, te[t] // EXPERTS_PER_GROUP, te[t] % EXPERTS_PER_GROUP, 0, 0))

    return pl.pallas_call(
        _moe_body,
        grid_spec=pltpu.PrefetchScalarGridSpec(
            num_scalar_prefetch=3,
            grid=(n_tiles,),
            in_specs=[
                pl.BlockSpec(memory_space=pl.ANY),
                pl.BlockSpec((MOE_TM, 1), lambda t, te, nu, tok: (t, 0)),
                w_spec(D, F), w_spec(D, F), w_spec(F, D),
            ],
            out_specs=pl.BlockSpec((MOE_TM, D), lambda t, te, nu, tok: (t, 0)),
            scratch_shapes=[
                pltpu.VMEM((2, MOE_TM, D), jnp.float32),
                pltpu.SemaphoreType.DMA((2,)),
                pltpu.VMEM((D, F), jnp.bfloat16), pltpu.VMEM((D, F), jnp.bfloat16),
                pltpu.VMEM((F, D), jnp.bfloat16),
            ],
        ),
        out_shape=jax.ShapeDtypeStruct((n_tiles * MOE_TM, D), jnp.float32),
        compiler_params=pltpu.CompilerParams(
            dimension_semantics=("arbitrary",),
            vmem_limit_bytes=VMEM_LIMIT_BYTES),
        name="moe_experts",
    )(tile_expert, n_used, row_token, x_all, row_weight, w_g, w_u, w_d)


def _combine_body(pos_ref, x_ref, g_ref, y_hbm, o_ref, buf, sem):
    i = pl.program_id(0)
    slot = lax.rem(i, 2)

    def row_copy(tile, slot_, r, j):
        row = pos_ref[(tile * COMBINE_TM + r) * TOPK_EXPERTS + j]
        return pltpu.make_async_copy(y_hbm.at[pl.ds(row, 1), :], buf.at[slot_, j, pl.ds(r, 1), :], sem.at[slot_])

    def gather(tile, slot_, start):
        def body(r, c):
            for j in range(TOPK_EXPERTS):
                cp = row_copy(tile, slot_, r, j)
                if start:
                    cp.start()
                else:
                    cp.wait()
            return c
        lax.fori_loop(0, COMBINE_TM, body, 0, unroll=8)

    @pl.when(i == 0)
    def _():
        gather(0, 0, True)

    @pl.when(i + 1 < pl.num_programs(0))
    def _():
        gather(i + 1, 1 - slot, True)

    gather(i, slot, False)
    moe = buf[slot, 0]
    for j in range(1, TOPK_EXPERTS):
        moe = moe + buf[slot, j]
    o_ref[...] = x_ref[...] + g_ref[...] * moe


def _moe_combine(x, gate, pos, y_sorted, rows_per_gate):
    N, D = x.shape
    tm = COMBINE_TM
    assert N % tm == 0 and (rows_per_gate % tm == 0 or tm % rows_per_gate == 0)
    if rows_per_gate >= tm:
        g_arr = gate.reshape(-1, 1, D)
        g_spec = pl.BlockSpec((None, 1, D), lambda i, pos: (i * tm // rows_per_gate, 0, 0))
    else:
        g_arr = jnp.repeat(gate, rows_per_gate, axis=0)
        g_spec = pl.BlockSpec((tm, D), lambda i, pos: (i, 0))
    return pl.pallas_call(
        _combine_body,
        grid_spec=pltpu.PrefetchScalarGridSpec(
            num_scalar_prefetch=1,
            grid=(N // tm,),
            in_specs=[pl.BlockSpec((tm, D), lambda i, pos: (i, 0)), g_spec, pl.BlockSpec(memory_space=pl.ANY)],
            out_specs=pl.BlockSpec((tm, D), lambda i, pos: (i, 0)),
            scratch_shapes=[pltpu.VMEM((2, TOPK_EXPERTS, tm, D), jnp.float32), pltpu.SemaphoreType.DMA((2,))],
        ),
        out_shape=jax.ShapeDtypeStruct((N, D), jnp.float32),
        compiler_params=pltpu.CompilerParams(
            dimension_semantics=("arbitrary",),
            vmem_limit_bytes=VMEM_LIMIT_BYTES),
        name="moe_combine",
    )(pos, x, g_arr, y_sorted)


def _dense_block(x, mod, pos, mod_rows, layout, attend, conv_fn, lw):
    n_seq, seq_len = layout
    sh1, sc1, g1, sh2, sc2, g2 = jnp.split(mod, 6, axis=-1)
    h = _norm1(x, lw['norm1_w'], sc1, sh1, mod_rows)
    cos, sin = _rope_tables(pos, HEAD_DIM // 2, HEAD_DIM)
    q_t, k, k_t, v, v_b = _qkv_proj(h, lw['w_b'], lw['q_norm_w'], lw['k_norm_w'], cos, sin, n_seq, seq_len)
    cos_i, sin_i = _rope_tables(pos, IDX_DIM // 2, 2 * IDX_DIM)
    qi_t, ki, ki_b, wi = _idx_proj(h, lw['w_b'], cos_i, sin_i, n_seq, seq_len)
    attn = attend(q_t, k_t, k, v, v_b, qi_t, ki, ki_b, wi)
    conv, new_state = conv_fn(h)
    merged = _merge_proj(h, attn, conv, lw['w_b'], lw['w_branch_attn'], lw['w_branch_conv'])
    x1, h2, logits = _out_proj(merged, x, g1, sc2, sh2, lw['norm2_w'], lw['w_out_b'], lw['w_router'], mod_rows)
    return x1, h2, logits, g2, k, v, ki, new_state


def kernel(x_prompt, x_sample, c_prompt, c_sample, cache_k, cache_v, cache_kidx, state_conv, page_table, norm1_w, norm2_w, w_ada, b_ada, w_in, q_norm_w, k_norm_w, conv_w, w_branch_attn, w_branch_conv, w_out, w_group_router, b_group_router, w_expert_router, b_expert_router, w_e_gate, w_e_up, w_e_down):
    B, T, _ = x_prompt.shape
    Bd, S, _ = x_sample.shape
    depth = norm1_w.shape[0]
    past = page_table.shape[1] * PAGE_SIZE
    n_p, n_s = B * T, Bd * S
    pos_prompt = jnp.arange(T)
    pos_sample = jnp.tile(past + jnp.arange(S), Bd)
    yp, ys = x_prompt, x_sample
    outs = [[] for _ in range(8)]
    c_all = jnp.concatenate([c_prompt, c_sample], axis=0)
    n_c = c_all.shape[0]
    c_pad = jnp.pad(jax.nn.silu(c_all), ((0, (-n_c) % 8), (0, 0)))
    bf = jnp.bfloat16
    for l in range(depth):
        w_router = jnp.pad(jnp.concatenate([w_group_router[l], w_expert_router[l].reshape(D_MODEL, -1)], axis=1),
                           ((0, 0), (0, 128 - N_GROUPS * (1 + EXPERTS_PER_GROUP))))
        lw = {
            'norm1_w': norm1_w[l], 'norm2_w': norm2_w[l], 'w_b': _repack_w_in(w_in[l]),
            'q_norm_w': q_norm_w[l], 'k_norm_w': k_norm_w[l],
            'w_branch_attn': w_branch_attn[l], 'w_branch_conv': w_branch_conv[l],
            'w_out_b': w_out[l].astype(bf), 'w_router': w_router,
        }
        mod = _mm(c_pad, w_ada[l], tn=1024)[:n_c] + b_ada[l]

        def prompt_attend(q_t, k_t, k, v, v_b, qi_t, ki, ki_b, wi):
            return _prompt_attention(q_t, k_t, v_b.reshape(B, T, ATTN_DIM), qi_t, ki_b.reshape(B, T, IDX_DIM),
                                     wi.reshape(B, T, N_IDX_HEADS))

        def prompt_conv(h):
            zero_state = jnp.zeros((B, CONV_WIDTH - 1, CONV_DIM), jnp.float32)
            return _conv_proj(h, lw['w_b'], conv_w[l], zero_state, B, T)

        def sample_attend(q_t, k_t, k, v, v_b, qi_t, ki, ki_b, wi):
            def per_token(a):
                return a[0].transpose(1, 0, 2).reshape(Bd, S, a.shape[1], a.shape[3])
            out = _sample_attention(per_token(q_t), k.reshape(Bd, S, N_HEADS, HEAD_DIM),
                                    v.reshape(Bd, S, N_HEADS, HEAD_DIM), per_token(qi_t),
                                    ki.reshape(Bd, S, IDX_DIM), wi.reshape(Bd, S, N_IDX_HEADS),
                                    layer=l, cache_k=cache_k, cache_v=cache_v, cache_kidx=cache_kidx,
                                    page_table=page_table)
            return out.astype(bf).reshape(n_s, N_HEADS, HEAD_DIM).transpose(1, 0, 2)[None]

        def sample_conv(h):
            xbc = _mm(h, lw['w_b'][:, COL_XC:COL_GA]).reshape(Bd, S, 3 * CONV_DIM)
            xc, bg, cg = jnp.split(xbc, 3, axis=-1)
            conv, state = _short_conv(xc, bg, cg, conv_w[l], state_conv[l])
            return conv.reshape(n_s, CONV_DIM).astype(bf), state

        xp, hp, lgp, gp, kp, vp, kip, cp = _dense_block(
            yp.reshape(n_p, D_MODEL), mod[:B], pos_prompt, T, (B, T), prompt_attend, prompt_conv, lw)
        xs, hs, lgs, gs, ksm, vsm, kis, cs = _dense_block(
            ys.reshape(n_s, D_MODEL), mod[B:], pos_sample, S, (1, n_s), sample_attend, sample_conv, lw)
        for lst, val in zip(outs, (kp.reshape(B, T, N_HEADS, HEAD_DIM), vp.reshape(B, T, N_HEADS, HEAD_DIM),
                                   kip.reshape(B, T, IDX_DIM), cp,
                                   ksm.reshape(Bd, S, N_HEADS, HEAD_DIM), vsm.reshape(Bd, S, N_HEADS, HEAD_DIM),
                                   kis.reshape(Bd, S, IDX_DIM), cs)):
            lst.append(val)
        h_all = jnp.concatenate([hp, hs], axis=0)
        eid, wgt = _route(jnp.concatenate([lgp, lgs], axis=0), b_group_router[l], b_expert_router[l])
        n_assign = eid.size
        n_tiles = (n_assign + N_GROUPS * EXPERTS_PER_GROUP * (MOE_TM - 1)) // MOE_TM + 1
        plan = _moe_plan(eid, wgt, n_tiles)
        y_sorted = _moe_experts(h_all, plan, w_e_gate, w_e_up, w_e_down, l, n_tiles)
        pos = plan[4]
        yp = _moe_combine(xp, gp, pos[:n_p * TOPK_EXPERTS], y_sorted, T).reshape(B, T, D_MODEL)
        ys = _moe_combine(xs, gs, pos[n_p * TOPK_EXPERTS:], y_sorted, S).reshape(Bd, S, D_MODEL)
    return (yp, ys) + tuple(jnp.stack(o) for o in outs)
```

```python
import functools

import jax
import jax.numpy as jnp
import numpy as np
from jax import lax
from jax.experimental import pallas as pl
from jax.experimental.pallas import tpu as pltpu

D_MODEL = 2048
PAGE_SIZE = 128
N_HEADS = 8
HEAD_DIM = 128
ATTN_DIM = N_HEADS * HEAD_DIM
N_IDX_HEADS = 8
IDX_DIM = 64
TOPK_MAX = 256
CONV_DIM = D_MODEL // 2
CONV_WIDTH = 3
N_GROUPS = 4
EXPERTS_PER_GROUP = 8
TOPK_EXPERTS = 2
EXPERT_FF = 512
ROPE_THETA = 10000.0
EPS = 1e-6

VMEM_LIMIT_BYTES = 56 * 1024 * 1024


def _mm_body(a_ref, b_ref, o_ref):
    a = a_ref[...].astype(jnp.bfloat16)
    b = b_ref[...].astype(jnp.bfloat16)
    o_ref[...] = jnp.dot(a, b, preferred_element_type=jnp.float32).astype(o_ref.dtype)


def _mm(a, b, *, tm=512, tn=512, out_dtype=jnp.float32):
    M, K = a.shape
    _, N = b.shape
    tm = min(tm, M)
    tn = min(tn, N)
    return pl.pallas_call(
        _mm_body,
        grid=(pl.cdiv(M, tm), pl.cdiv(N, tn)),
        in_specs=[pl.BlockSpec((tm, K), lambda i, j: (i, 0)),
                  pl.BlockSpec((K, tn), lambda i, j: (0, j))],
        out_specs=pl.BlockSpec((tm, tn), lambda i, j: (i, j)),
        out_shape=jax.ShapeDtypeStruct((M, N), out_dtype),
        compiler_params=pltpu.CompilerParams(
            dimension_semantics=("arbitrary", "arbitrary"),
            vmem_limit_bytes=VMEM_LIMIT_BYTES),
        name="mm",
    )(a, b)


def _hi_lo(x):
    hi = x.astype(jnp.bfloat16)
    lo = (x - hi.astype(jnp.float32)).astype(jnp.bfloat16)
    return hi, lo


ATTN_TQ = 256
ATTN_SEG = 1024
ATTN_SUB = 256
ATTN_HEADS_PER_STEP = 1
MASK_NEG = -1e30
KEY_NEG_INF = -2139095041
INT32_MIN = -2 ** 31


def _slab_sum(x):
    parts = [x[r * 8:(r + 1) * 8] for r in range(x.shape[0] // 8)]
    while len(parts) > 1:
        parts = [parts[a] + parts[a + 1] for a in range(0, len(parts), 2)]
    return parts[0]


def _prompt_attn_body(qi_ref, ki_ref, wit_ref, q_ref, k_ref, vt_ref, o_ref, key_ref, bias_ref, tie_ref,
                      *, topk, idx_bits):
    tq = ATTN_TQ
    i = pl.program_id(1)
    q0 = i * tq
    nseg = (q0 + tq + ATTN_SEG - 1) // ATTN_SEG
    ntile = nseg * (ATTN_SEG // ATTN_SUB)
    qpos = q0 + lax.broadcasted_iota(jnp.int32, (1, tq), 1)
    nt_dims = (((1,), (1,)), ((), ()))

    def score_tile(t, c):
        r0 = pl.multiple_of(t * ATTN_SUB, ATTN_SUB)
        ki = ki_ref[pl.ds(r0, ATTN_SUB), :]
        acc = jnp.zeros((ATTN_SUB, tq), jnp.float32)
        for h in range(N_IDX_HEADS):
            s = lax.dot_general(ki, qi_ref[h], nt_dims, preferred_element_type=jnp.float32)
            acc = acc + jnp.maximum(s, 0.0) * wit_ref[h:h + 1, :]
        kpos = r0 + lax.broadcasted_iota(jnp.int32, (ATTN_SUB, 1), 0)
        bits = lax.bitcast_convert_type(acc, jnp.int32)
        skey = jnp.where(bits < 0, bits ^ jnp.int32(0x7FFFFFFF), bits)
        skey = jnp.where(acc == 0.0, 0, skey)
        key_ref[pl.ds(r0, ATTN_SUB), :] = jnp.where(kpos <= qpos, skey, KEY_NEG_INF)
        return c

    lax.fori_loop(0, ntile, score_tile, 0)

    def count(pred):
        def tile(t, cnt):
            r0 = pl.multiple_of(t * ATTN_SUB, ATTN_SUB)
            kk = key_ref[pl.ds(r0, ATTN_SUB), :]
            kpos = r0 + lax.broadcasted_iota(jnp.int32, (ATTN_SUB, 1), 0)
            return cnt + _slab_sum(pred(kk, kpos).astype(jnp.int32))
        cnt = lax.fori_loop(0, ntile, tile, jnp.zeros((8, tq), jnp.int32))
        return jnp.sum(cnt, axis=0, keepdims=True)

    def bit_pass(p, t_u):
        cand_u = t_u | lax.shift_left(jnp.int32(1), 31 - p)
        cand_s = cand_u ^ jnp.int32(INT32_MIN)
        c = count(lambda kk, kpos: kk >= cand_s)
        return jnp.where(c >= topk, cand_u, t_u)

    t_u = lax.fori_loop(0, 32, bit_pass, jnp.zeros((1, tq), jnp.int32))
    t_s = t_u ^ jnp.int32(INT32_MIN)

    need = topk - count(lambda kk, kpos: kk > t_s)
    surplus = (count(lambda kk, kpos: kk >= t_s) > topk) & (t_s > KEY_NEG_INF)
    tie_ref[...] = jnp.full((1, tq), 2 ** idx_bits, jnp.int32)

    @pl.when(jnp.max(surplus.astype(jnp.int32)) > 0)
    def _():
        def idx_pass(p, lo):
            cand = lo | lax.shift_left(jnp.int32(1), idx_bits - 1 - p)
            c = count(lambda kk, kpos: (kk == t_s) & (kpos < cand))
            return jnp.where(c < need, cand, lo)

        tie_ref[...] = lax.fori_loop(0, idx_bits, idx_pass, jnp.zeros((1, tq), jnp.int32))

    last_tie = tie_ref[...]

    def bias_tile(t, c):
        r0 = pl.multiple_of(t * ATTN_SUB, ATTN_SUB)
        kk = key_ref[pl.ds(r0, ATTN_SUB), :]
        kpos = r0 + lax.broadcasted_iota(jnp.int32, (ATTN_SUB, 1), 0)
        sel = (kk > t_s) | ((kk == t_s) & (kpos <= last_tie))
        sel = sel & (kk > KEY_NEG_INF)
        bias_ref[pl.ds(r0, ATTN_SUB), :] = jnp.where(sel, 0.0, MASK_NEG)
        return c

    lax.fori_loop(0, ntile, bias_tile, 0)

    scale = HEAD_DIM ** -0.5

    def head_group_body(hg, c):
        heads = [hg * ATTN_HEADS_PER_STEP + a for a in range(ATTN_HEADS_PER_STEP)]

        def seg_body(s, carry):
            r0 = pl.multiple_of(s * ATTN_SEG, ATTN_SEG)
            bias = bias_ref[pl.ds(r0, ATTN_SEG), :]
            out = []
            for h, (m, l, acc) in zip(heads, carry):
                kh = k_ref[h, pl.ds(r0, ATTN_SEG), :]
                logits = lax.dot_general(kh, q_ref[h], nt_dims, preferred_element_type=jnp.float32) * scale
                logits = logits + bias
                m_new = jnp.maximum(m, jnp.max(logits, axis=0, keepdims=True))
                alpha = jnp.exp(m - m_new)
                p = jnp.exp(logits - m_new)
                l = alpha * l + jnp.sum(p, axis=0, keepdims=True)
                acc = alpha * acc + jnp.dot(vt_ref[h, s], p.astype(jnp.bfloat16),
                                            preferred_element_type=jnp.float32)
                out.append((m_new, l, acc))
            return tuple(out)

        init = tuple((jnp.full((1, tq), MASK_NEG, jnp.float32), jnp.zeros((1, tq), jnp.float32),
                      jnp.zeros((HEAD_DIM, tq), jnp.float32)) for _ in heads)
        for h, (_, l, acc) in zip(heads, lax.fori_loop(0, nseg, seg_body, init)):
            o_ref[h] = (acc / l).astype(o_ref.dtype)
        return c

    lax.fori_loop(0, N_HEADS // ATTN_HEADS_PER_STEP, head_group_body, 0)


def _prompt_attention(q_t, k_t, v_b, qi_t, ki_b, wi):
    B, _, T, _ = q_t.shape
    topk = min(TOPK_MAX, T // 4)
    nseg = T // ATTN_SEG
    bf = jnp.bfloat16
    vt = v_b.reshape(B, nseg, ATTN_SEG, N_HEADS, HEAD_DIM).transpose(0, 3, 1, 4, 2)
    wit = wi.transpose(0, 2, 1)
    body = functools.partial(_prompt_attn_body, topk=topk, idx_bits=int(np.ceil(np.log2(T))))
    return pl.pallas_call(
        body,
        grid=(B, T // ATTN_TQ),
        in_specs=[
            pl.BlockSpec((None, N_IDX_HEADS, ATTN_TQ, IDX_DIM), lambda b, i: (b, 0, i, 0)),
            pl.BlockSpec((None, T, IDX_DIM), lambda b, i: (b, 0, 0)),
            pl.BlockSpec((None, N_IDX_HEADS, ATTN_TQ), lambda b, i: (b, 0, i)),
            pl.BlockSpec((None, N_HEADS, ATTN_TQ, HEAD_DIM), lambda b, i: (b, 0, i, 0)),
            pl.BlockSpec((None, N_HEADS, T, HEAD_DIM), lambda b, i: (b, 0, 0, 0)),
            pl.BlockSpec((None, N_HEADS, nseg, HEAD_DIM, ATTN_SEG), lambda b, i: (b, 0, 0, 0, 0)),
        ],
        out_specs=pl.BlockSpec((None, N_HEADS, HEAD_DIM, ATTN_TQ), lambda b, i: (b, 0, 0, i)),
        out_shape=jax.ShapeDtypeStruct((B, N_HEADS, HEAD_DIM, T), bf),
        scratch_shapes=[pltpu.VMEM((T, ATTN_TQ), jnp.int32), pltpu.VMEM((T, ATTN_TQ), jnp.float32),
                        pltpu.VMEM((1, ATTN_TQ), jnp.int32)],
        compiler_params=pltpu.CompilerParams(
            dimension_semantics=("arbitrary", "arbitrary"),
            vmem_limit_bytes=VMEM_LIMIT_BYTES),
        name="prompt_attn",
    )(qi_t, ki_b, wit, q_t, k_t, vt)


SAMPLE_IDX_PAGES = 8
SAMPLE_KV_PAGES = 4
SAMPLE_NB = 2


def _score_key(score):
    bits = lax.bitcast_convert_type(score, jnp.int32)
    key = jnp.where(bits < 0, bits ^ jnp.int32(0x7FFFFFFF), bits)
    return jnp.where(score == 0.0, 0, key)


def _sample_select_body(pt_ref, qi_ref, w_ref, kin_ref, exp_ref, *rest, topk, n_pages, idx_bits):
    del pt_ref
    kid_refs = rest[:SAMPLE_IDX_PAGES]
    sel_ref, key_scr, tie_scr = rest[SAMPLE_IDX_PAGES:]
    g = pl.program_id(1)
    n_q = key_scr.shape[1]
    past = n_pages * PAGE_SIZE
    qi = qi_ref[...]
    wcol = w_ref[...]
    qpos = past + lax.broadcasted_iota(jnp.int32, (n_q, 1), 0)
    lane = lax.broadcasted_iota(jnp.int32, (1, PAGE_SIZE), 1)

    def page_keys(ki_page, page):
        s = lax.dot_general(qi, ki_page.astype(jnp.bfloat16), (((1,), (1,)), ((), ())),
                            preferred_element_type=jnp.float32)
        score = _slab_sum_n(jnp.maximum(s, 0.0) * wcol, n_q)
        kpos = page * PAGE_SIZE + lane
        return jnp.where(kpos <= qpos, _score_key(score), KEY_NEG_INF)

    for j in range(SAMPLE_IDX_PAGES):
        page = g * SAMPLE_IDX_PAGES + j
        key_scr[page] = page_keys(kid_refs[j][...], page)

    @pl.when(g == 0)
    def _():
        key_scr[n_pages] = page_keys(kin_ref[...], n_pages)

    @pl.when(g == pl.num_programs(1) - 1)
    def _():
        kk = key_scr[...]
        kpos = (lax.broadcasted_iota(jnp.int32, kk.shape, 0) * PAGE_SIZE
                + lax.broadcasted_iota(jnp.int32, kk.shape, 2))

        def count(m):
            return jnp.sum(jnp.sum(m.astype(jnp.int32), axis=0), axis=1, keepdims=True)

        def bit_pass(p, t_u):
            cand_u = t_u | lax.shift_left(jnp.int32(1), 31 - p)
            cand_s = cand_u ^ jnp.int32(INT32_MIN)
            return jnp.where(count(kk >= cand_s[None]) >= topk, cand_u, t_u)

        t_u = lax.fori_loop(0, 32, bit_pass, jnp.zeros((n_q, 1), jnp.int32))
        t_s = (t_u ^ jnp.int32(INT32_MIN))[None]
        need = topk - count(kk > t_s)
        surplus = (count(kk >= t_s) > topk) & (t_s[0] > KEY_NEG_INF)
        tie_scr[...] = jnp.full((n_q, 1), 2 ** idx_bits, jnp.int32)

        @pl.when(jnp.max(surplus.astype(jnp.int32)) > 0)
        def _():
            def idx_pass(p, lo):
                cand = lo | lax.shift_left(jnp.int32(1), idx_bits - 1 - p)
                return jnp.where(count((kk == t_s) & (kpos < cand[None])) < need, cand, lo)

            tie_scr[...] = lax.fori_loop(0, idx_bits, idx_pass, jnp.zeros((n_q, 1), jnp.int32))

        last_tie = tie_scr[...]
        sel = (kk > t_s) | ((kk == t_s) & (kpos <= last_tie[None]))
        sel = sel & (kk > KEY_NEG_INF)
        flat = sel.astype(jnp.float32).reshape((n_pages + 1) * n_q, PAGE_SIZE).astype(jnp.bfloat16)
        cols = jnp.dot(flat, exp_ref[...], preferred_element_type=jnp.float32)
        sel_ref[...] = ((cols - 1.0) * (-MASK_NEG)).reshape(sel_ref.shape)


def _slab_sum_n(x, n):
    parts = [x[r * n:(r + 1) * n] for r in range(x.shape[0] // n)]
    while len(parts) > 1:
        parts = [parts[a] + parts[a + 1] for a in range(0, len(parts), 2)]
    return parts[0]


def _sample_attn_body(pt_ref, q_ref, sel_ref, seln_ref, hm_ref, kn_ref, vn_ref, *rest):
    del pt_ref
    n_pg = SAMPLE_NB * SAMPLE_KV_PAGES
    k_refs = rest[:n_pg]
    v_refs = rest[n_pg:2 * n_pg]
    o_ref, m_scr, l_scr, acc_scr = rest[2 * n_pg:]
    g = pl.program_id(1)
    n_q = seln_ref.shape[2]
    bf = jnp.bfloat16
    scale = HEAD_DIM ** -0.5

    @pl.when(g == 0)
    def _():
        m_scr[...] = jnp.full(m_scr.shape, MASK_NEG, jnp.float32)
        l_scr[...] = jnp.zeros(l_scr.shape, jnp.float32)
        acc_scr[...] = jnp.zeros(acc_scr.shape, jnp.float32)

    def pages_update(nb, pages):
        qall = q_ref[nb]
        logits = []
        for k_page, _, sel_tile, ncol in pages:
            kp = k_page.reshape(-1, HEAD_DIM).astype(bf)
            bias = jnp.concatenate([sel_tile[:, :ncol]] * N_HEADS, axis=0) + hm_ref[:, :ncol]
            logits.append(lax.dot_general(qall, kp, (((1,), (1,)), ((), ())),
                                          preferred_element_type=jnp.float32) * scale + bias)
        m = m_scr[nb]
        m_new = m
        for lg in logits:
            m_new = jnp.maximum(m_new, jnp.max(lg, axis=1, keepdims=True))
        alpha = jnp.exp(m - m_new)
        l_new = alpha * l_scr[nb]
        acc = alpha * acc_scr[nb]
        for lg, (_, v_page, _, _) in zip(logits, pages):
            p = jnp.exp(lg - m_new)
            l_new = l_new + jnp.sum(p, axis=1, keepdims=True)
            acc = acc + jnp.dot(p.astype(bf), v_page.reshape(-1, HEAD_DIM).astype(bf),
                                preferred_element_type=jnp.float32)
        l_scr[nb] = l_new
        acc_scr[nb] = acc
        m_scr[nb] = m_new

    @pl.when(g == 0)
    def _():
        for nb in range(SAMPLE_NB):
            pages_update(nb, [(kn_ref[nb], vn_ref[nb], seln_ref[nb, 0], n_q * N_HEADS)])

    for nb in range(SAMPLE_NB):
        pages_update(nb, [(k_refs[nb * SAMPLE_KV_PAGES + j][...], v_refs[nb * SAMPLE_KV_PAGES + j][...],
                           sel_ref[nb, j], PAGE_SIZE * N_HEADS) for j in range(SAMPLE_KV_PAGES)])

    @pl.when(g == pl.num_programs(1) - 1)
    def _():
        o_ref[...] = acc_scr[...] / l_scr[...]


def _sample_attention(q, k, v, qi, ki, wi, *, layer, cache_k, cache_v, cache_kidx, page_table):
    Bd, S = q.shape[:2]
    n_pages = page_table.shape[1]
    past = n_pages * PAGE_SIZE
    topk = min(TOPK_MAX, (past + S) // 4)
    bf = jnp.bfloat16
    rows = N_HEADS * S
    qi_r = qi.astype(bf).transpose(0, 2, 1, 3).reshape(Bd, N_IDX_HEADS * S, IDX_DIM)
    w_r = jnp.broadcast_to(wi.transpose(0, 2, 1).reshape(Bd, N_IDX_HEADS * S, 1),
                           (Bd, N_IDX_HEADS * S, PAGE_SIZE))
    ki_new = jnp.pad(ki, ((0, 0), (0, PAGE_SIZE - S), (0, 0)))
    idx_bits = int(np.ceil(np.log2(past + PAGE_SIZE)))
    ncol = PAGE_SIZE * N_HEADS
    col = np.arange(ncol)
    expand = jnp.asarray(col[None, :] // N_HEADS == np.arange(PAGE_SIZE)[:, None], bf)

    def kid_spec(j):
        return pl.BlockSpec((None, None, PAGE_SIZE, IDX_DIM),
                            lambda b, g, pt: (layer, pt[b, g * SAMPLE_IDX_PAGES + j], 0, 0))

    sel = pl.pallas_call(
        functools.partial(_sample_select_body, topk=topk, n_pages=n_pages, idx_bits=idx_bits),
        grid_spec=pltpu.PrefetchScalarGridSpec(
            num_scalar_prefetch=1,
            grid=(Bd, n_pages // SAMPLE_IDX_PAGES),
            in_specs=[
                pl.BlockSpec((None, N_IDX_HEADS * S, IDX_DIM), lambda b, g, pt: (b, 0, 0)),
                pl.BlockSpec((None, N_IDX_HEADS * S, PAGE_SIZE), lambda b, g, pt: (b, 0, 0)),
                pl.BlockSpec((None, PAGE_SIZE, IDX_DIM), lambda b, g, pt: (b, 0, 0)),
                pl.BlockSpec((PAGE_SIZE, ncol), lambda b, g, pt: (0, 0)),
            ] + [kid_spec(j) for j in range(SAMPLE_IDX_PAGES)],
            out_specs=pl.BlockSpec((None, n_pages + 1, S, ncol), lambda b, g, pt: (b, 0, 0, 0)),
            scratch_shapes=[pltpu.VMEM((n_pages + 1, S, PAGE_SIZE), jnp.int32), pltpu.VMEM((S, 1), jnp.int32)],
        ),
        out_shape=jax.ShapeDtypeStruct((Bd, n_pages + 1, S, ncol), jnp.float32),
        compiler_params=pltpu.CompilerParams(
            dimension_semantics=("arbitrary", "arbitrary"),
            vmem_limit_bytes=VMEM_LIMIT_BYTES),
        name="sample_select",
    )(page_table, qi_r, w_r, ki_new, expand, *([cache_kidx] * SAMPLE_IDX_PAGES))

    q_r = q.astype(bf).transpose(0, 2, 1, 3).reshape(Bd, rows, HEAD_DIM)
    head_mask = jnp.asarray(np.where(col[None, :] % N_HEADS == np.arange(rows)[:, None] // S, 0.0, MASK_NEG),
                            jnp.float32)

    nb = SAMPLE_NB
    assert Bd % nb == 0

    def kv_spec(a, j):
        return pl.BlockSpec((None, None, PAGE_SIZE, N_HEADS, HEAD_DIM),
                            lambda b, g, pt: (layer, pt[b * nb + a, g * SAMPLE_KV_PAGES + j], 0, 0, 0))

    kv_specs = [kv_spec(a, j) for a in range(nb) for j in range(SAMPLE_KV_PAGES)]
    out = pl.pallas_call(
        _sample_attn_body,
        grid_spec=pltpu.PrefetchScalarGridSpec(
            num_scalar_prefetch=1,
            grid=(Bd // nb, n_pages // SAMPLE_KV_PAGES),
            in_specs=[
                pl.BlockSpec((nb, rows, HEAD_DIM), lambda b, g, pt: (b, 0, 0)),
                pl.BlockSpec((nb, SAMPLE_KV_PAGES, S, ncol), lambda b, g, pt: (b, g, 0, 0)),
                pl.BlockSpec((nb, 1, S, ncol), lambda b, g, pt: (b, n_pages, 0, 0)),
                pl.BlockSpec((rows, ncol), lambda b, g, pt: (0, 0)),
                pl.BlockSpec((nb, S, N_HEADS, HEAD_DIM), lambda b, g, pt: (b, 0, 0, 0)),
                pl.BlockSpec((nb, S, N_HEADS, HEAD_DIM), lambda b, g, pt: (b, 0, 0, 0)),
            ] + kv_specs * 2,
            out_specs=pl.BlockSpec((nb, rows, HEAD_DIM), lambda b, g, pt: (b, 0, 0)),
            scratch_shapes=[pltpu.VMEM((nb, rows, 1), jnp.float32), pltpu.VMEM((nb, rows, 1), jnp.float32),
                            pltpu.VMEM((nb, rows, HEAD_DIM), jnp.float32)],
        ),
        out_shape=jax.ShapeDtypeStruct((Bd, rows, HEAD_DIM), jnp.float32),
        compiler_params=pltpu.CompilerParams(
            dimension_semantics=("arbitrary", "arbitrary"),
            vmem_limit_bytes=VMEM_LIMIT_BYTES),
        name="sample_attn",
    )(page_table, q_r, sel, sel, head_mask, k, v,
      *([cache_k] * (nb * SAMPLE_KV_PAGES)), *([cache_v] * (nb * SAMPLE_KV_PAGES)))
    return out.reshape(Bd, N_HEADS, S, HEAD_DIM).transpose(0, 2, 1, 3)


W_GROUP = 1024
COL_IDX = 3 * ATTN_DIM
IDX_COLS = N_IDX_HEADS * IDX_DIM + IDX_DIM + N_IDX_HEADS
REST_XC, REST_BG, REST_CG = 0, CONV_DIM, 2 * CONV_DIM
REST_GA, REST_GC = 3 * CONV_DIM, 3 * CONV_DIM + D_MODEL
PROJ_TM = 1024
OUT_TM = 256


def _split_w_in(w_in):
    bf = jnp.bfloat16
    return w_in[:, :COL_IDX + W_GROUP].astype(bf), w_in[:, COL_IDX + IDX_COLS:].astype(bf)


def _rope_tables(pos, half, width):
    inv = 1.0 / jnp.power(ROPE_THETA, jnp.arange(half, dtype=jnp.float32) / half)
    ang = pos.astype(jnp.float32)[:, None] * inv[None, :]
    cos, sin = jnp.cos(ang), jnp.sin(ang)
    reps = width // (2 * half)
    return (jnp.tile(jnp.concatenate([cos, cos], axis=1), (1, reps)),
            jnp.tile(jnp.concatenate([-sin, sin], axis=1), (1, reps)))


def _group_param(param, rows_per_group, tm):
    D = param.shape[-1]
    if rows_per_group % tm == 0:
        return param[:, None, :], pl.BlockSpec((None, 1, D), lambda i, *_: (i * tm // rows_per_group, 0, 0))
    return jnp.repeat(param, rows_per_group, axis=0), pl.BlockSpec((tm, D), lambda i, *_: (i, 0))


def _cparams(n_axes):
    return pltpu.CompilerParams(dimension_semantics=("arbitrary",) * n_axes, vmem_limit_bytes=VMEM_LIMIT_BYTES)


def _modnorm(x, w, scale, shift):
    y = x * lax.rsqrt(jnp.mean(x * x, axis=-1, keepdims=True) + EPS) * w
    return y * (1.0 + scale) + shift


def _norm1_body(x_ref, sc_ref, sh_ref, w_ref, o_ref):
    o_ref[...] = _modnorm(x_ref[...], w_ref[...], sc_ref[...], sh_ref[...]).astype(o_ref.dtype)


def _norm1(x, norm_w, sc, sh, rows_per_group):
    N, D = x.shape
    tm = min(512, N)
    sc_arr, sc_spec = _group_param(sc, rows_per_group, tm)
    sh_arr, sh_spec = _group_param(sh, rows_per_group, tm)
    return pl.pallas_call(
        _norm1_body,
        grid=(N // tm,),
        in_specs=[pl.BlockSpec((tm, D), lambda i: (i, 0)), sc_spec, sh_spec,
                  pl.BlockSpec((1, D), lambda i: (0, 0))],
        out_specs=pl.BlockSpec((tm, D), lambda i: (i, 0)),
        out_shape=jax.ShapeDtypeStruct((N, D), jnp.bfloat16),
        compiler_params=_cparams(1),
        name="norm1",
    )(x, sc_arr, sh_arr, norm_w[None, :])


def _qkv_body(h_ref, w_ref, nw_ref, cos_ref, sin_ref, qt_ref, k_ref, kt_ref, v_ref, vb_ref):
    j = pl.program_id(1)
    acc = jnp.dot(h_ref[...], w_ref[...], preferred_element_type=jnp.float32)

    def norm_rope(nw):
        heads = []
        for hd in range(N_HEADS):
            xh = acc[:, hd * HEAD_DIM:(hd + 1) * HEAD_DIM]
            y = xh * lax.rsqrt(jnp.mean(xh * xh, axis=-1, keepdims=True) + EPS) * nw
            heads.append(y * cos_ref[...] + pltpu.roll(y, HEAD_DIM // 2, 1) * sin_ref[...])
        return heads

    @pl.when(j == 0)
    def _():
        for hd, y in enumerate(norm_rope(nw_ref[0:1, :])):
            qt_ref[hd] = y.astype(qt_ref.dtype)

    @pl.when(j == 1)
    def _():
        for hd, y in enumerate(norm_rope(nw_ref[1:2, :])):
            k_ref[:, hd * HEAD_DIM:(hd + 1) * HEAD_DIM] = y
            kt_ref[hd] = y.astype(kt_ref.dtype)

    @pl.when(j == 2)
    def _():
        v_ref[...] = acc
        vb_ref[...] = acc.astype(vb_ref.dtype)


def _qkv_proj(h, w_b, q_norm_w, k_norm_w, cos, sin, n_seq, seq_len):
    N, D = h.shape
    tm = min(PROJ_TM, seq_len)
    tiles = seq_len // tm
    bf = jnp.bfloat16
    nw = jnp.zeros((8, HEAD_DIM), jnp.float32).at[0].set(q_norm_w).at[1].set(k_norm_w)
    head_major = pl.BlockSpec((None, N_HEADS, tm, HEAD_DIM), lambda i, j: (i // tiles, 0, i % tiles, 0))
    rows = pl.BlockSpec((tm, ATTN_DIM), lambda i, j: (i, 0))
    table = pl.BlockSpec((tm, HEAD_DIM), lambda i, j: (i % tiles, 0))
    return pl.pallas_call(
        _qkv_body,
        grid=(N // tm, 3),
        in_specs=[pl.BlockSpec((tm, D), lambda i, j: (i, 0)),
                  pl.BlockSpec((D, W_GROUP), lambda i, j: (0, j)),
                  pl.BlockSpec((8, HEAD_DIM), lambda i, j: (0, 0)), table, table],
        out_specs=[head_major, rows, head_major, rows, rows],
        out_shape=[jax.ShapeDtypeStruct((n_seq, N_HEADS, seq_len, HEAD_DIM), bf),
                   jax.ShapeDtypeStruct((N, ATTN_DIM), jnp.float32),
                   jax.ShapeDtypeStruct((n_seq, N_HEADS, seq_len, HEAD_DIM), bf),
                   jax.ShapeDtypeStruct((N, ATTN_DIM), jnp.float32),
                   jax.ShapeDtypeStruct((N, ATTN_DIM), bf)],
        compiler_params=_cparams(2),
        name="qkv_proj",
    )(h, w_b, nw, cos, sin)


def _idx_body(h_ref, w_ref, cos_ref, sin_ref, qit_ref, ki_ref, kib_ref, wi_ref):
    acc = jnp.dot(h_ref[...], w_ref[...], preferred_element_type=jnp.float32)
    first_half = lax.broadcasted_iota(jnp.int32, (1, 2 * IDX_DIM), 1) % IDX_DIM < IDX_DIM // 2

    def rope_pair(x):
        swapped = jnp.where(first_half, pltpu.roll(x, 2 * IDX_DIM - IDX_DIM // 2, 1),
                            pltpu.roll(x, IDX_DIM // 2, 1))
        return x * cos_ref[...] + swapped * sin_ref[...]

    for pair in range(N_IDX_HEADS // 2):
        y = rope_pair(acc[:, pair * 2 * IDX_DIM:(pair + 1) * 2 * IDX_DIM])
        qit_ref[2 * pair] = y[:, :IDX_DIM].astype(qit_ref.dtype)
        qit_ref[2 * pair + 1] = y[:, IDX_DIM:].astype(qit_ref.dtype)
    tail = acc[:, N_IDX_HEADS * IDX_DIM:N_IDX_HEADS * IDX_DIM + 2 * IDX_DIM]
    ki = rope_pair(tail)[:, :IDX_DIM]
    ki_ref[...] = ki
    kib_ref[...] = ki.astype(kib_ref.dtype)
    wi_ref[...] = tail[:, IDX_DIM:IDX_DIM + N_IDX_HEADS] * ((N_IDX_HEADS ** -0.5) * (IDX_DIM ** -0.5))


def _idx_proj(h, w_b, cos, sin, n_seq, seq_len):
    N, D = h.shape
    tm = min(PROJ_TM, seq_len)
    tiles = seq_len // tm
    bf = jnp.bfloat16
    table = pl.BlockSpec((tm, 2 * IDX_DIM), lambda i: (i % tiles, 0))
    return pl.pallas_call(
        _idx_body,
        grid=(N // tm,),
        in_specs=[pl.BlockSpec((tm, D), lambda i: (i, 0)),
                  pl.BlockSpec((D, W_GROUP), lambda i: (0, COL_IDX // W_GROUP)), table, table],
        out_specs=[pl.BlockSpec((None, N_IDX_HEADS, tm, IDX_DIM), lambda i: (i // tiles, 0, i % tiles, 0)),
                   pl.BlockSpec((tm, IDX_DIM), lambda i: (i, 0)),
                   pl.BlockSpec((tm, IDX_DIM), lambda i: (i, 0)),
                   pl.BlockSpec((tm, N_IDX_HEADS), lambda i: (i, 0))],
        out_shape=[jax.ShapeDtypeStruct((n_seq, N_IDX_HEADS, seq_len, IDX_DIM), bf),
                   jax.ShapeDtypeStruct((N, IDX_DIM), jnp.float32),
                   jax.ShapeDtypeStruct((N, IDX_DIM), bf),
                   jax.ShapeDtypeStruct((N, N_IDX_HEADS), jnp.float32)],
        compiler_params=_cparams(1),
        name="idx_proj",
    )(h, w_b, cos, sin)


CONV_TN = 512


def _conv_body(h_ref, wx_ref, wb_ref, wc_ref, cw_ref, st_ref, y_ref, tail_ref, carry, *, tiles):
    i = pl.program_id(0)
    j = pl.program_id(1)
    h = h_ref[...]
    dot = functools.partial(jnp.dot, preferred_element_type=jnp.float32)
    u = dot(h, wc_ref[...]) * dot(h, wx_ref[...])
    bg = dot(h, wb_ref[...])

    @pl.when(i % tiles == 0)
    def _():
        carry[j] = st_ref[...]

    prev = carry[j]
    row = lax.broadcasted_iota(jnp.int32, (u.shape[0], 1), 0)
    u1 = jnp.where(row == 0, prev[1:2, :], pltpu.roll(u, 1, 0))
    u2 = jnp.where(row == 0, prev[0:1, :], jnp.where(row == 1, prev[1:2, :], pltpu.roll(u, 2, 0)))
    y_ref[...] = (bg * (cw_ref[0:1, :] * u2 + cw_ref[1:2, :] * u1 + cw_ref[2:3, :] * u)).astype(y_ref.dtype)
    last = u[u.shape[0] - (CONV_WIDTH - 1):, :]
    carry[j] = last
    tail_ref[...] = last


def _conv_proj(h, w_b, conv_w, state, n_seq, seq_len):
    N, D = h.shape
    tm = min(PROJ_TM, seq_len)
    tiles = seq_len // tm
    n_ct = CONV_DIM // CONV_TN

    def w_spec(col):
        return pl.BlockSpec((D, CONV_TN), lambda i, j: (0, col // CONV_TN + j))

    conv, tails = pl.pallas_call(
        functools.partial(_conv_body, tiles=tiles),
        grid=(N // tm, n_ct),
        in_specs=[pl.BlockSpec((tm, D), lambda i, j: (i, 0)),
                  w_spec(REST_XC), w_spec(REST_BG), w_spec(REST_CG),
                  pl.BlockSpec((CONV_WIDTH, CONV_TN), lambda i, j: (0, j)),
                  pl.BlockSpec((None, CONV_WIDTH - 1, CONV_TN), lambda i, j: (i // tiles, 0, j))],
        out_specs=[pl.BlockSpec((tm, CONV_TN), lambda i, j: (i, j)),
                   pl.BlockSpec((None, CONV_WIDTH - 1, CONV_TN), lambda i, j: (i, 0, j))],
        out_shape=[jax.ShapeDtypeStruct((N, CONV_DIM), jnp.bfloat16),
                   jax.ShapeDtypeStruct((N // tm, CONV_WIDTH - 1, CONV_DIM), jnp.float32)],
        scratch_shapes=[pltpu.VMEM((n_ct, CONV_WIDTH - 1, CONV_TN), jnp.float32)],
        compiler_params=_cparams(2),
        name="conv_proj",
    )(h, w_b, w_b, w_b, conv_w, state)
    return conv, tails[tiles - 1::tiles]


MERGE_TN = 512


def _merge_body(h_ref, a_ref, c_ref, wga_ref, wgc_ref, wba_ref, wbc_ref, o_ref):
    bf = jnp.bfloat16
    dot = functools.partial(jnp.dot, preferred_element_type=jnp.float32)
    h = h_ref[...]
    a_br = dot(a_ref[...], wba_ref[...].astype(bf))
    c_br = dot(c_ref[...], wbc_ref[...].astype(bf))
    merged = jax.nn.sigmoid(dot(h, wga_ref[...])) * a_br + jax.nn.sigmoid(dot(h, wgc_ref[...])) * c_br
    o_ref[...] = merged.astype(o_ref.dtype)


def _merge_proj(h, attn, conv, w_g, w_branch_attn, w_branch_conv):
    N, D = h.shape
    tm = min(PROJ_TM, N)
    return pl.pallas_call(
        _merge_body,
        grid=(N // tm, D // MERGE_TN),
        in_specs=[pl.BlockSpec((tm, D), lambda i, j: (i, 0)),
                  pl.BlockSpec((tm, ATTN_DIM), lambda i, j: (i, 0)),
                  pl.BlockSpec((tm, CONV_DIM), lambda i, j: (i, 0)),
                  pl.BlockSpec((D, MERGE_TN), lambda i, j: (0, REST_GA // MERGE_TN + j)),
                  pl.BlockSpec((D, MERGE_TN), lambda i, j: (0, REST_GC // MERGE_TN + j)),
                  pl.BlockSpec((ATTN_DIM, MERGE_TN), lambda i, j: (0, j)),
                  pl.BlockSpec((CONV_DIM, MERGE_TN), lambda i, j: (0, j))],
        out_specs=pl.BlockSpec((tm, MERGE_TN), lambda i, j: (i, j)),
        out_shape=jax.ShapeDtypeStruct((N, D), jnp.bfloat16),
        compiler_params=_cparams(2),
        name="merge_proj",
    )(h, attn, conv, w_g, w_g, w_branch_attn, w_branch_conv)


def _out_body(m_ref, x_ref, g1_ref, sc_ref, sh_ref, nw_ref, wo_ref, rh_ref, rl_ref, x1_ref, h2_ref, lg_ref):
    dot = functools.partial(jnp.dot, preferred_element_type=jnp.float32)
    x1 = x_ref[...] + g1_ref[...] * dot(m_ref[...], wo_ref[...])
    x1_ref[...] = x1
    h2 = _modnorm(x1, nw_ref[...], sc_ref[...], sh_ref[...])
    h2_ref[...] = h2
    h_hi, h_lo = _hi_lo(h2)
    lg_ref[...] = dot(h_hi, rh_ref[...]) + (dot(h_lo, rh_ref[...]) + dot(h_hi, rl_ref[...]))


def _out_proj(merged, x, g1, sc2, sh2, norm2_w, w_out_b, w_router, rows_per_group):
    N, D = x.shape
    tm = min(OUT_TM, N)
    g1_arr, g1_spec = _group_param(g1, rows_per_group, tm)
    sc_arr, sc_spec = _group_param(sc2, rows_per_group, tm)
    sh_arr, sh_spec = _group_param(sh2, rows_per_group, tm)
    r_hi, r_lo = _hi_lo(w_router)
    n_r = w_router.shape[1]
    rows = pl.BlockSpec((tm, D), lambda i: (i, 0))
    return pl.pallas_call(
        _out_body,
        grid=(N // tm,),
        in_specs=[rows, rows, g1_spec, sc_spec, sh_spec,
                  pl.BlockSpec((1, D), lambda i: (0, 0)),
                  pl.BlockSpec((D, D), lambda i: (0, 0)),
                  pl.BlockSpec((D, n_r), lambda i: (0, 0)),
                  pl.BlockSpec((D, n_r), lambda i: (0, 0))],
        out_specs=[rows, rows, pl.BlockSpec((tm, n_r), lambda i: (i, 0))],
        out_shape=[jax.ShapeDtypeStruct((N, D), jnp.float32), jax.ShapeDtypeStruct((N, D), jnp.float32),
                   jax.ShapeDtypeStruct((N, n_r), jnp.float32)],
        compiler_params=_cparams(1),
        name="out_proj",
    )(merged, x, g1_arr, sc_arr, sh_arr, norm2_w[None, :], w_out_b, r_hi, r_lo)


def _short_conv(xc, bg, cg, conv_w, buf):
    u = cg * xc
    ext = jnp.concatenate([buf, u], axis=1)
    T = u.shape[1]
    y = sum(conv_w[j] * ext[:, j:j + T] for j in range(CONV_WIDTH))
    return bg * y, ext[:, -(CONV_WIDTH - 1):]


MOE_TM = 256
COMBINE_TM = 128


def _route(r_logits, b_gr, b_er):
    n = r_logits.shape[0]
    g_logits = r_logits[:, :N_GROUPS]
    g_prob = jax.nn.softmax(g_logits, axis=-1)
    g_sel = jnp.argmax(g_logits + b_gr, axis=-1)
    g_w = jnp.take_along_axis(g_prob, g_sel[:, None], axis=1)[:, 0]
    e_logits = r_logits[:, N_GROUPS:N_GROUPS * (1 + EXPERTS_PER_GROUP)].reshape(n, N_GROUPS, EXPERTS_PER_GROUP)
    e_in = jnp.take_along_axis(e_logits, g_sel[:, None, None], axis=1)[:, 0]
    _, e_idx = lax.top_k(e_in + b_er[g_sel], TOPK_EXPERTS)
    e_p = jax.nn.softmax(jnp.take_along_axis(e_in, e_idx, axis=1), axis=-1)
    eid = (g_sel[:, None] * EXPERTS_PER_GROUP + e_idx).astype(jnp.int32)
    return eid, g_w[:, None] * e_p


def _moe_plan(eid, wgt, n_tiles):
    n_exp = N_GROUPS * EXPERTS_PER_GROUP
    a = eid.size
    e_flat = eid.reshape(a)
    onehot = (e_flat[:, None] == jnp.arange(n_exp, dtype=jnp.int32)[None, :]).astype(jnp.int32)
    running = jnp.cumsum(onehot, axis=0)
    counts = running[-1]
    padded = (counts + MOE_TM - 1) // MOE_TM * MOE_TM
    pad_end = jnp.cumsum(padded)
    pad_start = pad_end - padded
    cnt_start = jnp.cumsum(counts) - counts
    pos = pad_start[e_flat] + jnp.sum(running * onehot, axis=1) - 1
    order = jnp.argsort(e_flat, stable=True).astype(jnp.int32)
    rows = n_tiles * MOE_TM
    row = jnp.arange(rows, dtype=jnp.int32)
    tile_expert = jnp.minimum(jnp.searchsorted(pad_end, jnp.arange(n_tiles, dtype=jnp.int32) * MOE_TM, side='right'),
                              n_exp - 1).astype(jnp.int32)
    row_expert = jnp.repeat(tile_expert, MOE_TM)
    rank = row - pad_start[row_expert]
    row_valid = (rank < counts[row_expert]) & (row < pad_end[-1])
    assign = order[jnp.clip(cnt_start[row_expert] + rank, 0, a - 1)]
    row_token = jnp.where(row_valid, assign // TOPK_EXPERTS, 0)
    row_weight = jnp.where(row_valid, wgt.reshape(a)[assign], 0.0)
    n_used = (pad_end[-1] // MOE_TM).astype(jnp.int32).reshape(1)
    return tile_expert, n_used, row_token, row_weight.reshape(rows, 1), pos.astype(jnp.int32)


def _moe_body(te_ref, nu_ref, tok_ref, x_hbm, rw_ref, wg_ref, wu_ref, wd_ref, y_ref,
              buf, sem, wgb, wub, wdb):
    t = pl.program_id(0)
    n_used = nu_ref[0]
    slot = lax.rem(t, 2)
    bf = jnp.bfloat16

    def row_copy(tile, slot_, r):
        tok = tok_ref[tile * MOE_TM + r]
        return pltpu.make_async_copy(x_hbm.at[pl.ds(tok, 1), :], buf.at[slot_, pl.ds(r, 1), :], sem.at[slot_])

    def gather_start(tile, slot_):
        def body(r, c):
            row_copy(tile, slot_, r).start()
            return c
        lax.fori_loop(0, MOE_TM, body, 0, unroll=8)

    def gather_wait(tile, slot_):
        def body(r, c):
            row_copy(tile, slot_, r).wait()
            return c
        lax.fori_loop(0, MOE_TM, body, 0, unroll=8)

    @pl.when(t == 0)
    def _():
        gather_start(0, 0)

    @pl.when(t + 1 < n_used)
    def _():
        gather_start(t + 1, 1 - slot)

    @pl.when(t < n_used)
    def _():
        expert_changed = jnp.logical_or(t == 0, te_ref[t] != te_ref[jnp.maximum(t - 1, 0)])

        @pl.when(expert_changed)
        def _():
            wgb[...] = wg_ref[...].astype(bf)
            wub[...] = wu_ref[...].astype(bf)
            wdb[...] = wd_ref[...].astype(bf)

        gather_wait(t, slot)
        x = buf[slot].astype(bf)
        g = jnp.dot(x, wgb[...], preferred_element_type=jnp.float32)
        u = jnp.dot(x, wub[...], preferred_element_type=jnp.float32)
        hid = jax.nn.silu(g) * u * rw_ref[...]
        y_ref[...] = jnp.dot(hid.astype(bf), wdb[...], preferred_element_type=jnp.float32)

    @pl.when(t >= n_used)
    def _():
        y_ref[...] = jnp.zeros_like(y_ref)


def _moe_experts(x_all, plan, w_g, w_u, w_d, layer, n_tiles):
    tile_expert, n_used, row_token, row_weight, _ = plan
    D = x_all.shape[1]
    F = w_g.shape[-1]

    def w_spec(k_dim, n_dim):
        return pl.BlockSpec((None, None, None, k_dim, n_dim),
                            lambda t, te, nu, tok: (layer, te[t] // EXPERTS_PER_GROUP, te[t] % EXPERTS_PER_GROUP, 0, 0))

    return pl.pallas_call(
        _moe_body,
        grid_spec=pltpu.PrefetchScalarGridSpec(
            num_scalar_prefetch=3,
            grid=(n_tiles,),
            in_specs=[
                pl.BlockSpec(memory_space=pl.ANY),
                pl.BlockSpec((MOE_TM, 1), lambda t, te, nu, tok: (t, 0)),
                w_spec(D, F), w_spec(D, F), w_spec(F, D),
            ],
            out_specs=pl.BlockSpec((MOE_TM, D), lambda t, te, nu, tok: (t, 0)),
            scratch_shapes=[
                pltpu.VMEM((2, MOE_TM, D), jnp.float32),
                pltpu.SemaphoreType.DMA((2,)),
                pltpu.VMEM((D, F), jnp.bfloat16), pltpu.VMEM((D, F), jnp.bfloat16),
                pltpu.VMEM((F, D), jnp.bfloat16),
            ],
        ),
        out_shape=jax.ShapeDtypeStruct((n_tiles * MOE_TM, D), jnp.float32),
        compiler_params=pltpu.CompilerParams(
            dimension_semantics=("arbitrary",),
            vmem_limit_bytes=VMEM_LIMIT_BYTES),
        name="moe_experts",
    )(tile_expert, n_used, row_token, x_all, row_weight, w_g, w_u, w_d)


def _combine_body(pos_ref, x_ref, g_ref, y_hbm, o_ref, buf, sem):
    i = pl.program_id(0)
    slot = lax.rem(i, 2)

    def row_copy(tile, slot_, r, j):
        row = pos_ref[(tile * COMBINE_TM + r) * TOPK_EXPERTS + j]
        return pltpu.make_async_copy(y_hbm.at[pl.ds(row, 1), :], buf.at[slot_, j, pl.ds(r, 1), :], sem.at[slot_])

    def gather(tile, slot_, start):
        def body(r, c):
            for j in range(TOPK_EXPERTS):
                cp = row_copy(tile, slot_, r, j)
                if start:
                    cp.start()
                else:
                    cp.wait()
            return c
        lax.fori_loop(0, COMBINE_TM, body, 0, unroll=8)

    @pl.when(i == 0)
    def _():
        gather(0, 0, True)

    @pl.when(i + 1 < pl.num_programs(0))
    def _():
        gather(i + 1, 1 - slot, True)

    gather(i, slot, False)
    moe = buf[slot, 0]
    for j in range(1, TOPK_EXPERTS):
        moe = moe + buf[slot, j]
    o_ref[...] = x_ref[...] + g_ref[...] * moe


def _moe_combine(x, gate, pos, y_sorted, rows_per_gate):
    N, D = x.shape
    tm = COMBINE_TM
    assert N % tm == 0 and (rows_per_gate % tm == 0 or tm % rows_per_gate == 0)
    if rows_per_gate >= tm:
        g_arr = gate.reshape(-1, 1, D)
        g_spec = pl.BlockSpec((None, 1, D), lambda i, pos: (i * tm // rows_per_gate, 0, 0))
    else:
        g_arr = jnp.repeat(gate, rows_per_gate, axis=0)
        g_spec = pl.BlockSpec((tm, D), lambda i, pos: (i, 0))
    return pl.pallas_call(
        _combine_body,
        grid_spec=pltpu.PrefetchScalarGridSpec(
            num_scalar_prefetch=1,
            grid=(N // tm,),
            in_specs=[pl.BlockSpec((tm, D), lambda i, pos: (i, 0)), g_spec, pl.BlockSpec(memory_space=pl.ANY)],
            out_specs=pl.BlockSpec((tm, D), lambda i, pos: (i, 0)),
            scratch_shapes=[pltpu.VMEM((2, TOPK_EXPERTS, tm, D), jnp.float32), pltpu.SemaphoreType.DMA((2,))],
        ),
        out_shape=jax.ShapeDtypeStruct((N, D), jnp.float32),
        compiler_params=pltpu.CompilerParams(
            dimension_semantics=("arbitrary",),
            vmem_limit_bytes=VMEM_LIMIT_BYTES),
        name="moe_combine",
    )(pos, x, g_arr, y_sorted)


def _dense_block(x, mod, pos, mod_rows, layout, attend, conv_fn, lw):
    n_seq, seq_len = layout
    sh1, sc1, g1, sh2, sc2, g2 = jnp.split(mod, 6, axis=-1)
    h = _norm1(x, lw['norm1_w'], sc1, sh1, mod_rows)
    cos, sin = _rope_tables(pos, HEAD_DIM // 2, HEAD_DIM)
    q_t, k, k_t, v, v_b = _qkv_proj(h, lw['w_head'], lw['q_norm_w'], lw['k_norm_w'], cos, sin, n_seq, seq_len)
    cos_i, sin_i = _rope_tables(pos, IDX_DIM // 2, 2 * IDX_DIM)
    qi_t, ki, ki_b, wi = _idx_proj(h, lw['w_head'], cos_i, sin_i, n_seq, seq_len)
    attn = attend(q_t, k_t, k, v, v_b, qi_t, ki, ki_b, wi)
    conv, new_state = conv_fn(h)
    merged = _merge_proj(h, attn, conv, lw['w_rest'], lw['w_branch_attn'], lw['w_branch_conv'])
    x1, h2, logits = _out_proj(merged, x, g1, sc2, sh2, lw['norm2_w'], lw['w_out_b'], lw['w_router'], mod_rows)
    return x1, h2, logits, g2, k, v, ki, new_state


def kernel(x_prompt, x_sample, c_prompt, c_sample, cache_k, cache_v, cache_kidx, state_conv, page_table, norm1_w, norm2_w, w_ada, b_ada, w_in, q_norm_w, k_norm_w, conv_w, w_branch_attn, w_branch_conv, w_out, w_group_router, b_group_router, w_expert_router, b_expert_router, w_e_gate, w_e_up, w_e_down):
    B, T, _ = x_prompt.shape
    Bd, S, _ = x_sample.shape
    depth = norm1_w.shape[0]
    past = page_table.shape[1] * PAGE_SIZE
    n_p, n_s = B * T, Bd * S
    pos_prompt = jnp.arange(T)
    pos_sample = jnp.tile(past + jnp.arange(S), Bd)
    yp, ys = x_prompt, x_sample
    outs = [[] for _ in range(8)]
    c_all = jnp.concatenate([c_prompt, c_sample], axis=0)
    n_c = c_all.shape[0]
    c_pad = jnp.pad(jax.nn.silu(c_all), ((0, (-n_c) % 8), (0, 0)))
    bf = jnp.bfloat16
    for l in range(depth):
        w_router = jnp.pad(jnp.concatenate([w_group_router[l], w_expert_router[l].reshape(D_MODEL, -1)], axis=1),
                           ((0, 0), (0, 128 - N_GROUPS * (1 + EXPERTS_PER_GROUP))))
        w_head, w_rest = _split_w_in(w_in[l])
        lw = {
            'norm1_w': norm1_w[l], 'norm2_w': norm2_w[l], 'w_head': w_head, 'w_rest': w_rest,
            'q_norm_w': q_norm_w[l], 'k_norm_w': k_norm_w[l],
            'w_branch_attn': w_branch_attn[l], 'w_branch_conv': w_branch_conv[l],
            'w_out_b': w_out[l].astype(bf), 'w_router': w_router,
        }
        mod = _mm(c_pad, w_ada[l], tn=1024)[:n_c] + b_ada[l]

        def prompt_attend(q_t, k_t, k, v, v_b, qi_t, ki, ki_b, wi):
            out = _prompt_attention(q_t, k_t, v_b.reshape(B, T, ATTN_DIM), qi_t, ki_b.reshape(B, T, IDX_DIM),
                                    wi.reshape(B, T, N_IDX_HEADS))
            return out.transpose(0, 3, 1, 2).reshape(n_p, ATTN_DIM)

        def prompt_conv(h):
            zero_state = jnp.zeros((B, CONV_WIDTH - 1, CONV_DIM), jnp.float32)
            return _conv_proj(h, w_rest, conv_w[l], zero_state, B, T)

        def sample_attend(q_t, k_t, k, v, v_b, qi_t, ki, ki_b, wi):
            def per_token(a):
                return a[0].transpose(1, 0, 2).reshape(Bd, S, a.shape[1], a.shape[3])
            out = _sample_attention(per_token(q_t), k.reshape(Bd, S, N_HEADS, HEAD_DIM),
                                    v.reshape(Bd, S, N_HEADS, HEAD_DIM), per_token(qi_t),
                                    ki.reshape(Bd, S, IDX_DIM), wi.reshape(Bd, S, N_IDX_HEADS),
                                    layer=l, cache_k=cache_k, cache_v=cache_v, cache_kidx=cache_kidx,
                                    page_table=page_table)
            return out.reshape(n_s, ATTN_DIM).astype(bf)

        def sample_conv(h):
            xbc = _mm(h, w_rest[:, :REST_GA]).reshape(Bd, S, 3 * CONV_DIM)
            xc, bg, cg = jnp.split(xbc, 3, axis=-1)
            conv, state = _short_conv(xc, bg, cg, conv_w[l], state_conv[l])
            return conv.reshape(n_s, CONV_DIM).astype(bf), state

        xp, hp, lgp, gp, kp, vp, kip, cp = _dense_block(
            yp.reshape(n_p, D_MODEL), mod[:B], pos_prompt, T, (B, T), prompt_attend, prompt_conv, lw)
        xs, hs, lgs, gs, ksm, vsm, kis, cs = _dense_block(
            ys.reshape(n_s, D_MODEL), mod[B:], pos_sample, S, (1, n_s), sample_attend, sample_conv, lw)
        for lst, val in zip(outs, (kp.reshape(B, T, N_HEADS, HEAD_DIM), vp.reshape(B, T, N_HEADS, HEAD_DIM),
                                   kip.reshape(B, T, IDX_DIM), cp,
                                   ksm.reshape(Bd, S, N_HEADS, HEAD_DIM), vsm.reshape(Bd, S, N_HEADS, HEAD_DIM),
                                   kis.reshape(Bd, S, IDX_DIM), cs)):
            lst.append(val)
        h_all = jnp.concatenate([hp, hs], axis=0)
        eid, wgt = _route(jnp.concatenate([lgp, lgs], axis=0), b_group_router[l], b_expert_router[l])
        n_assign = eid.size
        n_tiles = (n_assign + N_GROUPS * EXPERTS_PER_GROUP * (MOE_TM - 1)) // MOE_TM
        plan = _moe_plan(eid, wgt, n_tiles)
        y_sorted = _moe_experts(h_all, plan, w_e_gate, w_e_up, w_e_down, l, n_tiles)
        pos = plan[4]
        yp = _moe_combine(xp, gp, pos[:n_p * TOPK_EXPERTS], y_sorted, T).reshape(B, T, D_MODEL)
        ys = _moe_combine(xs, gs, pos[n_p * TOPK_EXPERTS:], y_sorted, S).reshape(Bd, S, D_MODEL)
    return (yp, ys) + tuple(jnp.stack(o) for o in outs)
```

```python
import functools

import jax
import jax.numpy as jnp
import numpy as np
from jax import lax
from jax.experimental import pallas as pl
from jax.experimental.pallas import tpu as pltpu

D_MODEL = 2048
PAGE_SIZE = 128
N_HEADS = 8
HEAD_DIM = 128
ATTN_DIM = N_HEADS * HEAD_DIM
N_IDX_HEADS = 8
IDX_DIM = 64
TOPK_MAX = 256
CONV_DIM = D_MODEL // 2
CONV_WIDTH = 3
N_GROUPS = 4
EXPERTS_PER_GROUP = 8
TOPK_EXPERTS = 2
EXPERT_FF = 512
ROPE_THETA = 10000.0
EPS = 1e-6

VMEM_LIMIT_BYTES = 56 * 1024 * 1024


def _mm_body(a_ref, b_ref, o_ref):
    a = a_ref[...].astype(jnp.bfloat16)
    b = b_ref[...].astype(jnp.bfloat16)
    o_ref[...] = jnp.dot(a, b, preferred_element_type=jnp.float32).astype(o_ref.dtype)


def _mm(a, b, *, tm=512, tn=512, out_dtype=jnp.float32):
    M, K = a.shape
    _, N = b.shape
    tm = min(tm, M)
    tn = min(tn, N)
    return pl.pallas_call(
        _mm_body,
        grid=(pl.cdiv(M, tm), pl.cdiv(N, tn)),
        in_specs=[pl.BlockSpec((tm, K), lambda i, j: (i, 0)),
                  pl.BlockSpec((K, tn), lambda i, j: (0, j))],
        out_specs=pl.BlockSpec((tm, tn), lambda i, j: (i, j)),
        out_shape=jax.ShapeDtypeStruct((M, N), out_dtype),
        compiler_params=pltpu.CompilerParams(
            dimension_semantics=("arbitrary", "arbitrary"),
            vmem_limit_bytes=VMEM_LIMIT_BYTES),
        name="mm",
    )(a, b)


def _hi_lo(x):
    hi = x.astype(jnp.bfloat16)
    lo = (x - hi.astype(jnp.float32)).astype(jnp.bfloat16)
    return hi, lo


ATTN_TQ = 256
ATTN_SEG = 1024
ATTN_SUB = 256
ATTN_HEADS_PER_STEP = 1
MASK_NEG = -1e30
KEY_NEG_INF = -2139095041
INT32_MIN = -2 ** 31


def _slab_sum(x):
    parts = [x[r * 8:(r + 1) * 8] for r in range(x.shape[0] // 8)]
    while len(parts) > 1:
        parts = [parts[a] + parts[a + 1] for a in range(0, len(parts), 2)]
    return parts[0]


def _prompt_attn_body(qi_ref, ki_ref, wit_ref, q_ref, k_ref, vt_ref, o_ref, key_ref, bias_ref, tie_ref,
                      *, topk, idx_bits):
    tq = ATTN_TQ
    i = pl.program_id(1)
    q0 = i * tq
    nseg = (q0 + tq + ATTN_SEG - 1) // ATTN_SEG
    ntile = nseg * (ATTN_SEG // ATTN_SUB)
    qpos = q0 + lax.broadcasted_iota(jnp.int32, (1, tq), 1)
    nt_dims = (((1,), (1,)), ((), ()))

    def score_tile(t, c):
        r0 = pl.multiple_of(t * ATTN_SUB, ATTN_SUB)
        ki = ki_ref[pl.ds(r0, ATTN_SUB), :]
        acc = jnp.zeros((ATTN_SUB, tq), jnp.float32)
        for h in range(N_IDX_HEADS):
            s = lax.dot_general(ki, qi_ref[h], nt_dims, preferred_element_type=jnp.float32)
            acc = acc + jnp.maximum(s, 0.0) * wit_ref[h:h + 1, :]
        kpos = r0 + lax.broadcasted_iota(jnp.int32, (ATTN_SUB, 1), 0)
        bits = lax.bitcast_convert_type(acc, jnp.int32)
        skey = jnp.where(bits < 0, bits ^ jnp.int32(0x7FFFFFFF), bits)
        skey = jnp.where(acc == 0.0, 0, skey)
        key_ref[pl.ds(r0, ATTN_SUB), :] = jnp.where(kpos <= qpos, skey, KEY_NEG_INF)
        return c

    lax.fori_loop(0, ntile, score_tile, 0)

    def count(pred):
        def tile(t, cnt):
            r0 = pl.multiple_of(t * ATTN_SUB, ATTN_SUB)
            kk = key_ref[pl.ds(r0, ATTN_SUB), :]
            kpos = r0 + lax.broadcasted_iota(jnp.int32, (ATTN_SUB, 1), 0)
            return cnt + _slab_sum(pred(kk, kpos).astype(jnp.int32))
        cnt = lax.fori_loop(0, ntile, tile, jnp.zeros((8, tq), jnp.int32))
        return jnp.sum(cnt, axis=0, keepdims=True)

    def bit_pass(p, t_u):
        cand_u = t_u | lax.shift_left(jnp.int32(1), 31 - p)
        cand_s = cand_u ^ jnp.int32(INT32_MIN)
        c = count(lambda kk, kpos: kk >= cand_s)
        return jnp.where(c >= topk, cand_u, t_u)

    t_u = lax.fori_loop(0, 32, bit_pass, jnp.zeros((1, tq), jnp.int32))
    t_s = t_u ^ jnp.int32(INT32_MIN)

    need = topk - count(lambda kk, kpos: kk > t_s)
    surplus = (count(lambda kk, kpos: kk >= t_s) > topk) & (t_s > KEY_NEG_INF)
    tie_ref[...] = jnp.full((1, tq), 2 ** idx_bits, jnp.int32)

    @pl.when(jnp.max(surplus.astype(jnp.int32)) > 0)
    def _():
        def idx_pass(p, lo):
            cand = lo | lax.shift_left(jnp.int32(1), idx_bits - 1 - p)
            c = count(lambda kk, kpos: (kk == t_s) & (kpos < cand))
            return jnp.where(c < need, cand, lo)

        tie_ref[...] = lax.fori_loop(0, idx_bits, idx_pass, jnp.zeros((1, tq), jnp.int32))

    last_tie = tie_ref[...]

    def bias_tile(t, c):
        r0 = pl.multiple_of(t * ATTN_SUB, ATTN_SUB)
        kk = key_ref[pl.ds(r0, ATTN_SUB), :]
        kpos = r0 + lax.broadcasted_iota(jnp.int32, (ATTN_SUB, 1), 0)
        sel = (kk > t_s) | ((kk == t_s) & (kpos <= last_tie))
        sel = sel & (kk > KEY_NEG_INF)
        bias_ref[pl.ds(r0, ATTN_SUB), :] = jnp.where(sel, 0.0, MASK_NEG)
        return c

    lax.fori_loop(0, ntile, bias_tile, 0)

    scale = HEAD_DIM ** -0.5

    def head_group_body(hg, c):
        heads = [hg * ATTN_HEADS_PER_STEP + a for a in range(ATTN_HEADS_PER_STEP)]

        def seg_body(s, carry):
            r0 = pl.multiple_of(s * ATTN_SEG, ATTN_SEG)
            bias = bias_ref[pl.ds(r0, ATTN_SEG), :]
            out = []
            for h, (m, l, acc) in zip(heads, carry):
                kh = k_ref[h, pl.ds(r0, ATTN_SEG), :]
                logits = lax.dot_general(kh, q_ref[h], nt_dims, preferred_element_type=jnp.float32) * scale
                logits = logits + bias
                m_new = jnp.maximum(m, jnp.max(logits, axis=0, keepdims=True))
                alpha = jnp.exp(m - m_new)
                p = jnp.exp(logits - m_new)
                l = alpha * l + jnp.sum(p, axis=0, keepdims=True)
                acc = alpha * acc + jnp.dot(vt_ref[h, s], p.astype(jnp.bfloat16),
                                            preferred_element_type=jnp.float32)
                out.append((m_new, l, acc))
            return tuple(out)

        init = tuple((jnp.full((1, tq), MASK_NEG, jnp.float32), jnp.zeros((1, tq), jnp.float32),
                      jnp.zeros((HEAD_DIM, tq), jnp.float32)) for _ in heads)
        for h, (_, l, acc) in zip(heads, lax.fori_loop(0, nseg, seg_body, init)):
            o_ref[h] = (acc / l).astype(o_ref.dtype)
        return c

    lax.fori_loop(0, N_HEADS // ATTN_HEADS_PER_STEP, head_group_body, 0)


def _prompt_attention(q_t, k_t, v_b, qi_t, ki_b, wi):
    B, _, T, _ = q_t.shape
    topk = min(TOPK_MAX, T // 4)
    nseg = T // ATTN_SEG
    bf = jnp.bfloat16
    vt = v_b.reshape(B, nseg, ATTN_SEG, N_HEADS, HEAD_DIM).transpose(0, 3, 1, 4, 2)
    wit = wi.transpose(0, 2, 1)
    body = functools.partial(_prompt_attn_body, topk=topk, idx_bits=int(np.ceil(np.log2(T))))
    return pl.pallas_call(
        body,
        grid=(B, T // ATTN_TQ),
        in_specs=[
            pl.BlockSpec((None, N_IDX_HEADS, ATTN_TQ, IDX_DIM), lambda b, i: (b, 0, i, 0)),
            pl.BlockSpec((None, T, IDX_DIM), lambda b, i: (b, 0, 0)),
            pl.BlockSpec((None, N_IDX_HEADS, ATTN_TQ), lambda b, i: (b, 0, i)),
            pl.BlockSpec((None, N_HEADS, ATTN_TQ, HEAD_DIM), lambda b, i: (b, 0, i, 0)),
            pl.BlockSpec((None, N_HEADS, T, HEAD_DIM), lambda b, i: (b, 0, 0, 0)),
            pl.BlockSpec((None, N_HEADS, nseg, HEAD_DIM, ATTN_SEG), lambda b, i: (b, 0, 0, 0, 0)),
        ],
        out_specs=pl.BlockSpec((None, N_HEADS, HEAD_DIM, ATTN_TQ), lambda b, i: (b, 0, 0, i)),
        out_shape=jax.ShapeDtypeStruct((B, N_HEADS, HEAD_DIM, T), bf),
        scratch_shapes=[pltpu.VMEM((T, ATTN_TQ), jnp.int32), pltpu.VMEM((T, ATTN_TQ), jnp.float32),
                        pltpu.VMEM((1, ATTN_TQ), jnp.int32)],
        compiler_params=pltpu.CompilerParams(
            dimension_semantics=("arbitrary", "arbitrary"),
            vmem_limit_bytes=VMEM_LIMIT_BYTES),
        name="prompt_attn",
    )(qi_t, ki_b, wit, q_t, k_t, vt)


SAMPLE_IDX_PAGES = 8
SAMPLE_KV_PAGES = 4
SAMPLE_NB = 4


def _score_key(score):
    bits = lax.bitcast_convert_type(score, jnp.int32)
    key = jnp.where(bits < 0, bits ^ jnp.int32(0x7FFFFFFF), bits)
    return jnp.where(score == 0.0, 0, key)


def _sample_select_body(pt_ref, qi_ref, w_ref, kin_ref, exp_ref, *rest, topk, n_pages, idx_bits):
    del pt_ref
    kid_refs = rest[:SAMPLE_IDX_PAGES]
    sel_ref, key_scr, tie_scr = rest[SAMPLE_IDX_PAGES:]
    g = pl.program_id(1)
    n_q = key_scr.shape[1]
    past = n_pages * PAGE_SIZE
    qi = qi_ref[...]
    wcol = w_ref[...]
    qpos = past + lax.broadcasted_iota(jnp.int32, (n_q, 1), 0)
    lane = lax.broadcasted_iota(jnp.int32, (1, PAGE_SIZE), 1)

    def page_keys(ki_page, page):
        s = lax.dot_general(qi, ki_page.astype(jnp.bfloat16), (((1,), (1,)), ((), ())),
                            preferred_element_type=jnp.float32)
        score = _slab_sum_n(jnp.maximum(s, 0.0) * wcol, n_q)
        kpos = page * PAGE_SIZE + lane
        return jnp.where(kpos <= qpos, _score_key(score), KEY_NEG_INF)

    for j in range(SAMPLE_IDX_PAGES):
        page = g * SAMPLE_IDX_PAGES + j
        key_scr[page] = page_keys(kid_refs[j][...], page)

    @pl.when(g == 0)
    def _():
        key_scr[n_pages] = page_keys(kin_ref[...], n_pages)

    @pl.when(g == pl.num_programs(1) - 1)
    def _():
        kk = key_scr[...]
        kpos = (lax.broadcasted_iota(jnp.int32, kk.shape, 0) * PAGE_SIZE
                + lax.broadcasted_iota(jnp.int32, kk.shape, 2))

        def count(m):
            return jnp.sum(jnp.sum(m.astype(jnp.int32), axis=0), axis=1, keepdims=True)

        def bit_pass(p, t_u):
            cand_u = t_u | lax.shift_left(jnp.int32(1), 31 - p)
            cand_s = cand_u ^ jnp.int32(INT32_MIN)
            return jnp.where(count(kk >= cand_s[None]) >= topk, cand_u, t_u)

        t_u = lax.fori_loop(0, 32, bit_pass, jnp.zeros((n_q, 1), jnp.int32))
        t_s = (t_u ^ jnp.int32(INT32_MIN))[None]
        need = topk - count(kk > t_s)
        surplus = (count(kk >= t_s) > topk) & (t_s[0] > KEY_NEG_INF)
        tie_scr[...] = jnp.full((n_q, 1), 2 ** idx_bits, jnp.int32)

        @pl.when(jnp.max(surplus.astype(jnp.int32)) > 0)
        def _():
            def idx_pass(p, lo):
                cand = lo | lax.shift_left(jnp.int32(1), idx_bits - 1 - p)
                return jnp.where(count((kk == t_s) & (kpos < cand[None])) < need, cand, lo)

            tie_scr[...] = lax.fori_loop(0, idx_bits, idx_pass, jnp.zeros((n_q, 1), jnp.int32))

        last_tie = tie_scr[...]
        sel = (kk > t_s) | ((kk == t_s) & (kpos <= last_tie[None]))
        sel = sel & (kk > KEY_NEG_INF)
        flat = sel.astype(jnp.float32).reshape((n_pages + 1) * n_q, PAGE_SIZE).astype(jnp.bfloat16)
        cols = jnp.dot(flat, exp_ref[...], preferred_element_type=jnp.float32)
        sel_ref[...] = ((cols - 1.0) * (-MASK_NEG)).reshape(sel_ref.shape)


def _slab_sum_n(x, n):
    parts = [x[r * n:(r + 1) * n] for r in range(x.shape[0] // n)]
    while len(parts) > 1:
        parts = [parts[a] + parts[a + 1] for a in range(0, len(parts), 2)]
    return parts[0]


def _sample_attn_body(pt_ref, q_ref, sel_ref, seln_ref, hm_ref, kn_ref, vn_ref, *rest):
    del pt_ref
    n_pg = SAMPLE_NB * SAMPLE_KV_PAGES
    k_refs = rest[:n_pg]
    v_refs = rest[n_pg:2 * n_pg]
    o_ref, m_scr, l_scr, acc_scr = rest[2 * n_pg:]
    g = pl.program_id(1)
    n_q = seln_ref.shape[2]
    bf = jnp.bfloat16
    scale = HEAD_DIM ** -0.5

    @pl.when(g == 0)
    def _():
        m_scr[...] = jnp.full(m_scr.shape, MASK_NEG, jnp.float32)
        l_scr[...] = jnp.zeros(l_scr.shape, jnp.float32)
        acc_scr[...] = jnp.zeros(acc_scr.shape, jnp.float32)

    def pages_update(nb, pages):
        qall = q_ref[nb]
        logits = []
        for k_page, _, sel_tile, ncol in pages:
            kp = k_page.reshape(-1, HEAD_DIM).astype(bf)
            bias = jnp.concatenate([sel_tile[:, :ncol]] * N_HEADS, axis=0) + hm_ref[:, :ncol]
            logits.append(lax.dot_general(qall, kp, (((1,), (1,)), ((), ())),
                                          preferred_element_type=jnp.float32) * scale + bias)
        m = m_scr[nb]
        m_new = m
        for lg in logits:
            m_new = jnp.maximum(m_new, jnp.max(lg, axis=1, keepdims=True))
        alpha = jnp.exp(m - m_new)
        l_new = alpha * l_scr[nb]
        acc = alpha * acc_scr[nb]
        for lg, (_, v_page, _, _) in zip(logits, pages):
            p = jnp.exp(lg - m_new)
            l_new = l_new + jnp.sum(p, axis=1, keepdims=True)
            acc = acc + jnp.dot(p.astype(bf), v_page.reshape(-1, HEAD_DIM).astype(bf),
                                preferred_element_type=jnp.float32)
        l_scr[nb] = l_new
        acc_scr[nb] = acc
        m_scr[nb] = m_new

    @pl.when(g == 0)
    def _():
        for nb in range(SAMPLE_NB):
            pages_update(nb, [(kn_ref[nb], vn_ref[nb], seln_ref[nb, 0], n_q * N_HEADS)])

    for nb in range(SAMPLE_NB):
        pages_update(nb, [(k_refs[nb * SAMPLE_KV_PAGES + j][...], v_refs[nb * SAMPLE_KV_PAGES + j][...],
                           sel_ref[nb, j], PAGE_SIZE * N_HEADS) for j in range(SAMPLE_KV_PAGES)])

    @pl.when(g == pl.num_programs(1) - 1)
    def _():
        o_ref[...] = acc_scr[...] / l_scr[...]


def _sample_attention(q, k, v, qi, ki, wi, *, layer, cache_k, cache_v, cache_kidx, page_table):
    Bd, S = q.shape[:2]
    n_pages = page_table.shape[1]
    past = n_pages * PAGE_SIZE
    topk = min(TOPK_MAX, (past + S) // 4)
    bf = jnp.bfloat16
    rows = N_HEADS * S
    qi_r = qi.astype(bf).transpose(0, 2, 1, 3).reshape(Bd, N_IDX_HEADS * S, IDX_DIM)
    w_r = jnp.broadcast_to(wi.transpose(0, 2, 1).reshape(Bd, N_IDX_HEADS * S, 1),
                           (Bd, N_IDX_HEADS * S, PAGE_SIZE))
    ki_new = jnp.pad(ki, ((0, 0), (0, PAGE_SIZE - S), (0, 0)))
    idx_bits = int(np.ceil(np.log2(past + PAGE_SIZE)))
    ncol = PAGE_SIZE * N_HEADS
    col = np.arange(ncol)
    expand = jnp.asarray(col[None, :] // N_HEADS == np.arange(PAGE_SIZE)[:, None], bf)

    def kid_spec(j):
        return pl.BlockSpec((None, None, PAGE_SIZE, IDX_DIM),
                            lambda b, g, pt: (layer, pt[b, g * SAMPLE_IDX_PAGES + j], 0, 0))

    sel = pl.pallas_call(
        functools.partial(_sample_select_body, topk=topk, n_pages=n_pages, idx_bits=idx_bits),
        grid_spec=pltpu.PrefetchScalarGridSpec(
            num_scalar_prefetch=1,
            grid=(Bd, n_pages // SAMPLE_IDX_PAGES),
            in_specs=[
                pl.BlockSpec((None, N_IDX_HEADS * S, IDX_DIM), lambda b, g, pt: (b, 0, 0)),
                pl.BlockSpec((None, N_IDX_HEADS * S, PAGE_SIZE), lambda b, g, pt: (b, 0, 0)),
                pl.BlockSpec((None, PAGE_SIZE, IDX_DIM), lambda b, g, pt: (b, 0, 0)),
                pl.BlockSpec((PAGE_SIZE, ncol), lambda b, g, pt: (0, 0)),
            ] + [kid_spec(j) for j in range(SAMPLE_IDX_PAGES)],
            out_specs=pl.BlockSpec((None, n_pages + 1, S, ncol), lambda b, g, pt: (b, 0, 0, 0)),
            scratch_shapes=[pltpu.VMEM((n_pages + 1, S, PAGE_SIZE), jnp.int32), pltpu.VMEM((S, 1), jnp.int32)],
        ),
        out_shape=jax.ShapeDtypeStruct((Bd, n_pages + 1, S, ncol), jnp.float32),
        compiler_params=pltpu.CompilerParams(
            dimension_semantics=("arbitrary", "arbitrary"),
            vmem_limit_bytes=VMEM_LIMIT_BYTES),
        name="sample_select",
    )(page_table, qi_r, w_r, ki_new, expand, *([cache_kidx] * SAMPLE_IDX_PAGES))

    q_r = q.astype(bf).transpose(0, 2, 1, 3).reshape(Bd, rows, HEAD_DIM)
    head_mask = jnp.asarray(np.where(col[None, :] % N_HEADS == np.arange(rows)[:, None] // S, 0.0, MASK_NEG),
                            jnp.float32)

    nb = SAMPLE_NB
    assert Bd % nb == 0

    def kv_spec(a, j):
        return pl.BlockSpec((None, None, PAGE_SIZE, N_HEADS, HEAD_DIM),
                            lambda b, g, pt: (layer, pt[b * nb + a, g * SAMPLE_KV_PAGES + j], 0, 0, 0))

    kv_specs = [kv_spec(a, j) for a in range(nb) for j in range(SAMPLE_KV_PAGES)]
    out = pl.pallas_call(
        _sample_attn_body,
        grid_spec=pltpu.PrefetchScalarGridSpec(
            num_scalar_prefetch=1,
            grid=(Bd // nb, n_pages // SAMPLE_KV_PAGES),
            in_specs=[
                pl.BlockSpec((nb, rows, HEAD_DIM), lambda b, g, pt: (b, 0, 0)),
                pl.BlockSpec((nb, SAMPLE_KV_PAGES, S, ncol), lambda b, g, pt: (b, g, 0, 0)),
                pl.BlockSpec((nb, 1, S, ncol), lambda b, g, pt: (b, n_pages, 0, 0)),
                pl.BlockSpec((rows, ncol), lambda b, g, pt: (0, 0)),
                pl.BlockSpec((nb, S, N_HEADS, HEAD_DIM), lambda b, g, pt: (b, 0, 0, 0)),
                pl.BlockSpec((nb, S, N_HEADS, HEAD_DIM), lambda b, g, pt: (b, 0, 0, 0)),
            ] + kv_specs * 2,
            out_specs=pl.BlockSpec((nb, rows, HEAD_DIM), lambda b, g, pt: (b, 0, 0)),
            scratch_shapes=[pltpu.VMEM((nb, rows, 1), jnp.float32), pltpu.VMEM((nb, rows, 1), jnp.float32),
                            pltpu.VMEM((nb, rows, HEAD_DIM), jnp.float32)],
        ),
        out_shape=jax.ShapeDtypeStruct((Bd, rows, HEAD_DIM), jnp.float32),
        compiler_params=pltpu.CompilerParams(
            dimension_semantics=("arbitrary", "arbitrary"),
            vmem_limit_bytes=VMEM_LIMIT_BYTES),
        name="sample_attn",
    )(page_table, q_r, sel, sel, head_mask, k, v,
      *([cache_k] * (nb * SAMPLE_KV_PAGES)), *([cache_v] * (nb * SAMPLE_KV_PAGES)))
    return out.reshape(Bd, N_HEADS, S, HEAD_DIM).transpose(0, 2, 1, 3)


W_GROUP = 1024
COL_IDX = 3 * ATTN_DIM
IDX_COLS = N_IDX_HEADS * IDX_DIM + IDX_DIM + N_IDX_HEADS
REST_XC, REST_BG, REST_CG = 0, CONV_DIM, 2 * CONV_DIM
REST_GA, REST_GC = 3 * CONV_DIM, 3 * CONV_DIM + D_MODEL
PROJ_TM = 1024
OUT_TM = 256


def _split_w_in(w_in):
    bf = jnp.bfloat16
    return w_in[:, :COL_IDX + W_GROUP].astype(bf), w_in[:, COL_IDX + IDX_COLS:].astype(bf)


def _rope_tables(pos, half, width):
    inv = 1.0 / jnp.power(ROPE_THETA, jnp.arange(half, dtype=jnp.float32) / half)
    ang = pos.astype(jnp.float32)[:, None] * inv[None, :]
    cos, sin = jnp.cos(ang), jnp.sin(ang)
    reps = width // (2 * half)
    return (jnp.tile(jnp.concatenate([cos, cos], axis=1), (1, reps)),
            jnp.tile(jnp.concatenate([-sin, sin], axis=1), (1, reps)))


def _group_param(param, rows_per_group, tm):
    D = param.shape[-1]
    if rows_per_group % tm == 0:
        return param[:, None, :], pl.BlockSpec((None, 1, D), lambda i, *_: (i * tm // rows_per_group, 0, 0))
    return jnp.repeat(param, rows_per_group, axis=0), pl.BlockSpec((tm, D), lambda i, *_: (i, 0))


def _cparams(n_axes):
    return pltpu.CompilerParams(dimension_semantics=("arbitrary",) * n_axes, vmem_limit_bytes=VMEM_LIMIT_BYTES)


def _modnorm(x, w, scale, shift):
    y = x * lax.rsqrt(jnp.mean(x * x, axis=-1, keepdims=True) + EPS) * w
    return y * (1.0 + scale) + shift


def _norm1_body(x_ref, sc_ref, sh_ref, w_ref, o_ref):
    o_ref[...] = _modnorm(x_ref[...], w_ref[...], sc_ref[...], sh_ref[...]).astype(o_ref.dtype)


def _norm1(x, norm_w, sc, sh, rows_per_group):
    N, D = x.shape
    tm = min(512, N)
    sc_arr, sc_spec = _group_param(sc, rows_per_group, tm)
    sh_arr, sh_spec = _group_param(sh, rows_per_group, tm)
    return pl.pallas_call(
        _norm1_body,
        grid=(N // tm,),
        in_specs=[pl.BlockSpec((tm, D), lambda i: (i, 0)), sc_spec, sh_spec,
                  pl.BlockSpec((1, D), lambda i: (0, 0))],
        out_specs=pl.BlockSpec((tm, D), lambda i: (i, 0)),
        out_shape=jax.ShapeDtypeStruct((N, D), jnp.bfloat16),
        compiler_params=_cparams(1),
        name="norm1",
    )(x, sc_arr, sh_arr, norm_w[None, :])


def _qkv_body(h_ref, w_ref, nw_ref, cos_ref, sin_ref, qt_ref, k_ref, kt_ref, v_ref, vb_ref):
    j = pl.program_id(1)
    acc = jnp.dot(h_ref[...], w_ref[...], preferred_element_type=jnp.float32)

    def norm_rope(nw):
        heads = []
        for hd in range(N_HEADS):
            xh = acc[:, hd * HEAD_DIM:(hd + 1) * HEAD_DIM]
            y = xh * lax.rsqrt(jnp.mean(xh * xh, axis=-1, keepdims=True) + EPS) * nw
            heads.append(y * cos_ref[...] + pltpu.roll(y, HEAD_DIM // 2, 1) * sin_ref[...])
        return heads

    @pl.when(j == 0)
    def _():
        for hd, y in enumerate(norm_rope(nw_ref[0:1, :])):
            qt_ref[hd] = y.astype(qt_ref.dtype)

    @pl.when(j == 1)
    def _():
        for hd, y in enumerate(norm_rope(nw_ref[1:2, :])):
            k_ref[:, hd * HEAD_DIM:(hd + 1) * HEAD_DIM] = y
            kt_ref[hd] = y.astype(kt_ref.dtype)

    @pl.when(j == 2)
    def _():
        v_ref[...] = acc
        vb_ref[...] = acc.astype(vb_ref.dtype)


def _qkv_proj(h, w_b, q_norm_w, k_norm_w, cos, sin, n_seq, seq_len):
    N, D = h.shape
    tm = min(PROJ_TM, seq_len)
    tiles = seq_len // tm
    bf = jnp.bfloat16
    nw = jnp.zeros((8, HEAD_DIM), jnp.float32).at[0].set(q_norm_w).at[1].set(k_norm_w)
    head_major = pl.BlockSpec((None, N_HEADS, tm, HEAD_DIM), lambda i, j: (i // tiles, 0, i % tiles, 0))
    rows = pl.BlockSpec((tm, ATTN_DIM), lambda i, j: (i, 0))
    table = pl.BlockSpec((tm, HEAD_DIM), lambda i, j: (i % tiles, 0))
    return pl.pallas_call(
        _qkv_body,
        grid=(N // tm, 3),
        in_specs=[pl.BlockSpec((tm, D), lambda i, j: (i, 0)),
                  pl.BlockSpec((D, W_GROUP), lambda i, j: (0, j)),
                  pl.BlockSpec((8, HEAD_DIM), lambda i, j: (0, 0)), table, table],
        out_specs=[head_major, rows, head_major, rows, rows],
        out_shape=[jax.ShapeDtypeStruct((n_seq, N_HEADS, seq_len, HEAD_DIM), bf),
                   jax.ShapeDtypeStruct((N, ATTN_DIM), jnp.float32),
                   jax.ShapeDtypeStruct((n_seq, N_HEADS, seq_len, HEAD_DIM), bf),
                   jax.ShapeDtypeStruct((N, ATTN_DIM), jnp.float32),
                   jax.ShapeDtypeStruct((N, ATTN_DIM), bf)],
        compiler_params=_cparams(2),
        name="qkv_proj",
    )(h, w_b, nw, cos, sin)


def _idx_body(h_ref, w_ref, cos_ref, sin_ref, qit_ref, ki_ref, kib_ref, wi_ref):
    acc = jnp.dot(h_ref[...], w_ref[...], preferred_element_type=jnp.float32)
    first_half = lax.broadcasted_iota(jnp.int32, (1, 2 * IDX_DIM), 1) % IDX_DIM < IDX_DIM // 2

    def rope_pair(x):
        swapped = jnp.where(first_half, pltpu.roll(x, 2 * IDX_DIM - IDX_DIM // 2, 1),
                            pltpu.roll(x, IDX_DIM // 2, 1))
        return x * cos_ref[...] + swapped * sin_ref[...]

    for pair in range(N_IDX_HEADS // 2):
        y = rope_pair(acc[:, pair * 2 * IDX_DIM:(pair + 1) * 2 * IDX_DIM])
        qit_ref[2 * pair] = y[:, :IDX_DIM].astype(qit_ref.dtype)
        qit_ref[2 * pair + 1] = y[:, IDX_DIM:].astype(qit_ref.dtype)
    tail = acc[:, N_IDX_HEADS * IDX_DIM:N_IDX_HEADS * IDX_DIM + 2 * IDX_DIM]
    ki = rope_pair(tail)[:, :IDX_DIM]
    ki_ref[...] = ki
    kib_ref[...] = ki.astype(kib_ref.dtype)
    wi_ref[...] = tail[:, IDX_DIM:IDX_DIM + N_IDX_HEADS] * ((N_IDX_HEADS ** -0.5) * (IDX_DIM ** -0.5))


def _idx_proj(h, w_b, cos, sin, n_seq, seq_len):
    N, D = h.shape
    tm = min(PROJ_TM, seq_len)
    tiles = seq_len // tm
    bf = jnp.bfloat16
    table = pl.BlockSpec((tm, 2 * IDX_DIM), lambda i: (i % tiles, 0))
    return pl.pallas_call(
        _idx_body,
        grid=(N // tm,),
        in_specs=[pl.BlockSpec((tm, D), lambda i: (i, 0)),
                  pl.BlockSpec((D, W_GROUP), lambda i: (0, COL_IDX // W_GROUP)), table, table],
        out_specs=[pl.BlockSpec((None, N_IDX_HEADS, tm, IDX_DIM), lambda i: (i // tiles, 0, i % tiles, 0)),
                   pl.BlockSpec((tm, IDX_DIM), lambda i: (i, 0)),
                   pl.BlockSpec((tm, IDX_DIM), lambda i: (i, 0)),
                   pl.BlockSpec((tm, N_IDX_HEADS), lambda i: (i, 0))],
        out_shape=[jax.ShapeDtypeStruct((n_seq, N_IDX_HEADS, seq_len, IDX_DIM), bf),
                   jax.ShapeDtypeStruct((N, IDX_DIM), jnp.float32),
                   jax.ShapeDtypeStruct((N, IDX_DIM), bf),
                   jax.ShapeDtypeStruct((N, N_IDX_HEADS), jnp.float32)],
        compiler_params=_cparams(1),
        name="idx_proj",
    )(h, w_b, cos, sin)


CONV_TN = 512


def _conv_body(h_ref, wx_ref, wb_ref, wc_ref, cw_ref, st_ref, y_ref, tail_ref, carry, *, tiles):
    i = pl.program_id(0)
    j = pl.program_id(1)
    h = h_ref[...]
    dot = functools.partial(jnp.dot, preferred_element_type=jnp.float32)
    u = dot(h, wc_ref[...]) * dot(h, wx_ref[...])
    bg = dot(h, wb_ref[...])

    @pl.when(i % tiles == 0)
    def _():
        carry[j] = st_ref[...]

    prev = carry[j]
    row = lax.broadcasted_iota(jnp.int32, (u.shape[0], 1), 0)
    u1 = jnp.where(row == 0, prev[1:2, :], pltpu.roll(u, 1, 0))
    u2 = jnp.where(row == 0, prev[0:1, :], jnp.where(row == 1, prev[1:2, :], pltpu.roll(u, 2, 0)))
    y_ref[...] = (bg * (cw_ref[0:1, :] * u2 + cw_ref[1:2, :] * u1 + cw_ref[2:3, :] * u)).astype(y_ref.dtype)
    last = u[u.shape[0] - (CONV_WIDTH - 1):, :]
    carry[j] = last
    tail_ref[...] = last


def _conv_proj(h, w_b, conv_w, state, n_seq, seq_len):
    N, D = h.shape
    tm = min(PROJ_TM, seq_len)
    tiles = seq_len // tm
    n_ct = CONV_DIM // CONV_TN

    def w_spec(col):
        return pl.BlockSpec((D, CONV_TN), lambda i, j: (0, col // CONV_TN + j))

    conv, tails = pl.pallas_call(
        functools.partial(_conv_body, tiles=tiles),
        grid=(N // tm, n_ct),
        in_specs=[pl.BlockSpec((tm, D), lambda i, j: (i, 0)),
                  w_spec(REST_XC), w_spec(REST_BG), w_spec(REST_CG),
                  pl.BlockSpec((CONV_WIDTH, CONV_TN), lambda i, j: (0, j)),
                  pl.BlockSpec((None, CONV_WIDTH - 1, CONV_TN), lambda i, j: (i // tiles, 0, j))],
        out_specs=[pl.BlockSpec((tm, CONV_TN), lambda i, j: (i, j)),
                   pl.BlockSpec((None, CONV_WIDTH - 1, CONV_TN), lambda i, j: (i, 0, j))],
        out_shape=[jax.ShapeDtypeStruct((N, CONV_DIM), jnp.bfloat16),
                   jax.ShapeDtypeStruct((N // tm, CONV_WIDTH - 1, CONV_DIM), jnp.float32)],
        scratch_shapes=[pltpu.VMEM((n_ct, CONV_WIDTH - 1, CONV_TN), jnp.float32)],
        compiler_params=_cparams(2),
        name="conv_proj",
    )(h, w_b, w_b, w_b, conv_w, state)
    return conv, tails[tiles - 1::tiles]


MERGE_TN = 512


def _merge_body(h_ref, a_ref, c_ref, wga_ref, wgc_ref, wba_ref, wbc_ref, o_ref):
    bf = jnp.bfloat16
    dot = functools.partial(jnp.dot, preferred_element_type=jnp.float32)
    h = h_ref[...]
    a_br = dot(a_ref[...], wba_ref[...].astype(bf))
    c_br = dot(c_ref[...], wbc_ref[...].astype(bf))
    merged = jax.nn.sigmoid(dot(h, wga_ref[...])) * a_br + jax.nn.sigmoid(dot(h, wgc_ref[...])) * c_br
    o_ref[...] = merged.astype(o_ref.dtype)


def _merge_proj(h, attn, conv, w_g, w_branch_attn, w_branch_conv):
    N, D = h.shape
    tm = min(PROJ_TM, N)
    return pl.pallas_call(
        _merge_body,
        grid=(N // tm, D // MERGE_TN),
        in_specs=[pl.BlockSpec((tm, D), lambda i, j: (i, 0)),
                  pl.BlockSpec((tm, ATTN_DIM), lambda i, j: (i, 0)),
                  pl.BlockSpec((tm, CONV_DIM), lambda i, j: (i, 0)),
                  pl.BlockSpec((D, MERGE_TN), lambda i, j: (0, REST_GA // MERGE_TN + j)),
                  pl.BlockSpec((D, MERGE_TN), lambda i, j: (0, REST_GC // MERGE_TN + j)),
                  pl.BlockSpec((ATTN_DIM, MERGE_TN), lambda i, j: (0, j)),
                  pl.BlockSpec((CONV_DIM, MERGE_TN), lambda i, j: (0, j))],
        out_specs=pl.BlockSpec((tm, MERGE_TN), lambda i, j: (i, j)),
        out_shape=jax.ShapeDtypeStruct((N, D), jnp.bfloat16),
        compiler_params=_cparams(2),
        name="merge_proj",
    )(h, attn, conv, w_g, w_g, w_branch_attn, w_branch_conv)


def _out_body(m_ref, x_ref, g1_ref, sc_ref, sh_ref, nw_ref, wo_ref, rh_ref, rl_ref, x1_ref, h2_ref, lg_ref):
    dot = functools.partial(jnp.dot, preferred_element_type=jnp.float32)
    x1 = x_ref[...] + g1_ref[...] * dot(m_ref[...], wo_ref[...])
    x1_ref[...] = x1
    h2 = _modnorm(x1, nw_ref[...], sc_ref[...], sh_ref[...])
    h2_ref[...] = h2
    h_hi, h_lo = _hi_lo(h2)
    lg_ref[...] = dot(h_hi, rh_ref[...]) + (dot(h_lo, rh_ref[...]) + dot(h_hi, rl_ref[...]))


def _out_proj(merged, x, g1, sc2, sh2, norm2_w, w_out_b, w_router, rows_per_group):
    N, D = x.shape
    tm = min(OUT_TM, N)
    g1_arr, g1_spec = _group_param(g1, rows_per_group, tm)
    sc_arr, sc_spec = _group_param(sc2, rows_per_group, tm)
    sh_arr, sh_spec = _group_param(sh2, rows_per_group, tm)
    r_hi, r_lo = _hi_lo(w_router)
    n_r = w_router.shape[1]
    rows = pl.BlockSpec((tm, D), lambda i: (i, 0))
    return pl.pallas_call(
        _out_body,
        grid=(N // tm,),
        in_specs=[rows, rows, g1_spec, sc_spec, sh_spec,
                  pl.BlockSpec((1, D), lambda i: (0, 0)),
                  pl.BlockSpec((D, D), lambda i: (0, 0)),
                  pl.BlockSpec((D, n_r), lambda i: (0, 0)),
                  pl.BlockSpec((D, n_r), lambda i: (0, 0))],
        out_specs=[rows, rows, pl.BlockSpec((tm, n_r), lambda i: (i, 0))],
        out_shape=[jax.ShapeDtypeStruct((N, D), jnp.float32), jax.ShapeDtypeStruct((N, D), jnp.float32),
                   jax.ShapeDtypeStruct((N, n_r), jnp.float32)],
        compiler_params=_cparams(1),
        name="out_proj",
    )(merged, x, g1_arr, sc_arr, sh_arr, norm2_w[None, :], w_out_b, r_hi, r_lo)


def _short_conv(xc, bg, cg, conv_w, buf):
    u = cg * xc
    ext = jnp.concatenate([buf, u], axis=1)
    T = u.shape[1]
    y = sum(conv_w[j] * ext[:, j:j + T] for j in range(CONV_WIDTH))
    return bg * y, ext[:, -(CONV_WIDTH - 1):]


MOE_TM = 256
COMBINE_TM = 128


def _route(r_logits, b_gr, b_er):
    n = r_logits.shape[0]
    g_logits = r_logits[:, :N_GROUPS]
    g_prob = jax.nn.softmax(g_logits, axis=-1)
    g_sel = jnp.argmax(g_logits + b_gr, axis=-1)
    g_w = jnp.take_along_axis(g_prob, g_sel[:, None], axis=1)[:, 0]
    e_logits = r_logits[:, N_GROUPS:N_GROUPS * (1 + EXPERTS_PER_GROUP)].reshape(n, N_GROUPS, EXPERTS_PER_GROUP)
    e_in = jnp.take_along_axis(e_logits, g_sel[:, None, None], axis=1)[:, 0]
    _, e_idx = lax.top_k(e_in + b_er[g_sel], TOPK_EXPERTS)
    e_p = jax.nn.softmax(jnp.take_along_axis(e_in, e_idx, axis=1), axis=-1)
    eid = (g_sel[:, None] * EXPERTS_PER_GROUP + e_idx).astype(jnp.int32)
    return eid, g_w[:, None] * e_p


def _moe_plan(eid, wgt, n_tiles):
    n_exp = N_GROUPS * EXPERTS_PER_GROUP
    a = eid.size
    e_flat = eid.reshape(a)
    order = jnp.argsort(e_flat, stable=True).astype(jnp.int32)
    sorted_pos = jnp.argsort(order).astype(jnp.int32)
    bounds = jnp.searchsorted(e_flat[order], jnp.arange(n_exp + 1, dtype=jnp.int32), side='left').astype(jnp.int32)
    cnt_start, counts = bounds[:-1], bounds[1:] - bounds[:-1]
    padded = (counts + MOE_TM - 1) // MOE_TM * MOE_TM
    pad_end = jnp.cumsum(padded)
    pad_start = pad_end - padded
    pos = sorted_pos + (pad_start - cnt_start)[e_flat]
    tile_row0 = jnp.arange(n_tiles, dtype=jnp.int32) * MOE_TM
    tile_expert = jnp.minimum(jnp.searchsorted(pad_end, tile_row0, side='right'), n_exp - 1).astype(jnp.int32)
    rank = (tile_row0 - pad_start[tile_expert])[:, None] + jnp.arange(MOE_TM, dtype=jnp.int32)[None, :]
    row_valid = (rank < counts[tile_expert][:, None]) & (tile_row0 < pad_end[-1])[:, None]
    assign = order[jnp.clip(cnt_start[tile_expert][:, None] + rank, 0, a - 1)]
    rows = n_tiles * MOE_TM
    row_token = jnp.where(row_valid, assign // TOPK_EXPERTS, 0).reshape(rows)
    row_weight = jnp.where(row_valid, wgt.reshape(a)[assign], 0.0).reshape(rows, 1)
    n_used = (pad_end[-1] // MOE_TM).astype(jnp.int32).reshape(1)
    return tile_expert, n_used, row_token, row_weight, pos.astype(jnp.int32)


def _moe_body(te_ref, nu_ref, tok_ref, x_hbm, rw_ref, wg_ref, wu_ref, wd_ref, y_ref,
              buf, sem, wgb, wub, wdb):
    t = pl.program_id(0)
    n_used = nu_ref[0]
    slot = lax.rem(t, 2)
    bf = jnp.bfloat16

    def row_copy(tile, slot_, r):
        tok = tok_ref[tile * MOE_TM + r]
        return pltpu.make_async_copy(x_hbm.at[pl.ds(tok, 1), :], buf.at[slot_, pl.ds(r, 1), :], sem.at[slot_])

    def gather_start(tile, slot_):
        def body(r, c):
            row_copy(tile, slot_, r).start()
            return c
        lax.fori_loop(0, MOE_TM, body, 0, unroll=8)

    def gather_wait(tile, slot_):
        def body(r, c):
            row_copy(tile, slot_, r).wait()
            return c
        lax.fori_loop(0, MOE_TM, body, 0, unroll=8)

    @pl.when(t == 0)
    def _():
        gather_start(0, 0)

    @pl.when(t + 1 < n_used)
    def _():
        gather_start(t + 1, 1 - slot)

    @pl.when(t < n_used)
    def _():
        expert_changed = jnp.logical_or(t == 0, te_ref[t] != te_ref[jnp.maximum(t - 1, 0)])

        @pl.when(expert_changed)
        def _():
            wgb[...] = wg_ref[...].astype(bf)
            wub[...] = wu_ref[...].astype(bf)
            wdb[...] = wd_ref[...].astype(bf)

        gather_wait(t, slot)
        x = buf[slot].astype(bf)
        g = jnp.dot(x, wgb[...], preferred_element_type=jnp.float32)
        u = jnp.dot(x, wub[...], preferred_element_type=jnp.float32)
        hid = jax.nn.silu(g) * u * rw_ref[...]
        y_ref[...] = jnp.dot(hid.astype(bf), wdb[...], preferred_element_type=jnp.float32)

    @pl.when(t >= n_used)
    def _():
        y_ref[...] = jnp.zeros_like(y_ref)


def _moe_experts(x_all, plan, w_g, w_u, w_d, layer, n_tiles):
    tile_expert, n_used, row_token, row_weight, _ = plan
    D = x_all.shape[1]
    F = w_g.shape[-1]

    def w_spec(k_dim, n_dim):
        return pl.BlockSpec((None, None, None, k_dim, n_dim),
                            lambda t, te, nu, tok: (layer, te[t] // EXPERTS_PER_GROUP, te[t] % EXPERTS_PER_GROUP, 0, 0))

    return pl.pallas_call(
        _moe_body,
        grid_spec=pltpu.PrefetchScalarGridSpec(
            num_scalar_prefetch=3,
            grid=(n_tiles,),
            in_specs=[
                pl.BlockSpec(memory_space=pl.ANY),
                pl.BlockSpec((MOE_TM, 1), lambda t, te, nu, tok: (t, 0)),
                w_spec(D, F), w_spec(D, F), w_spec(F, D),
            ],
            out_specs=pl.BlockSpec((MOE_TM, D), lambda t, te, nu, tok: (t, 0)),
            scratch_shapes=[
                pltpu.VMEM((2, MOE_TM, D), jnp.float32),
                pltpu.SemaphoreType.DMA((2,)),
                pltpu.VMEM((D, F), jnp.bfloat16), pltpu.VMEM((D, F), jnp.bfloat16),
                pltpu.VMEM((F, D), jnp.bfloat16),
            ],
        ),
        out_shape=jax.ShapeDtypeStruct((n_tiles * MOE_TM, D), jnp.float32),
        compiler_params=pltpu.CompilerParams(
            dimension_semantics=("arbitrary",),
            vmem_limit_bytes=VMEM_LIMIT_BYTES),
        name="moe_experts",
    )(tile_expert, n_used, row_token, x_all, row_weight, w_g, w_u, w_d)


def _combine_body(pos_ref, x_ref, g_ref, y_hbm, o_ref, buf, sem):
    i = pl.program_id(0)
    slot = lax.rem(i, 2)

    def row_copy(tile, slot_, r, j):
        row = pos_ref[(tile * COMBINE_TM + r) * TOPK_EXPERTS + j]
        return pltpu.make_async_copy(y_hbm.at[pl.ds(row, 1), :], buf.at[slot_, j, pl.ds(r, 1), :], sem.at[slot_])

    def gather(tile, slot_, start):
        def body(r, c):
            for j in range(TOPK_EXPERTS):
                cp = row_copy(tile, slot_, r, j)
                if start:
                    cp.start()
                else:
                    cp.wait()
            return c
        lax.fori_loop(0, COMBINE_TM, body, 0, unroll=8)

    @pl.when(i == 0)
    def _():
        gather(0, 0, True)

    @pl.when(i + 1 < pl.num_programs(0))
    def _():
        gather(i + 1, 1 - slot, True)

    gather(i, slot, False)
    moe = buf[slot, 0]
    for j in range(1, TOPK_EXPERTS):
        moe = moe + buf[slot, j]
    o_ref[...] = x_ref[...] + g_ref[...] * moe


def _moe_combine(x, gate, pos, y_sorted, rows_per_gate):
    N, D = x.shape
    tm = COMBINE_TM
    assert N % tm == 0 and (rows_per_gate % tm == 0 or tm % rows_per_gate == 0)
    if rows_per_gate >= tm:
        g_arr = gate.reshape(-1, 1, D)
        g_spec = pl.BlockSpec((None, 1, D), lambda i, pos: (i * tm // rows_per_gate, 0, 0))
    else:
        g_arr = jnp.repeat(gate, rows_per_gate, axis=0)
        g_spec = pl.BlockSpec((tm, D), lambda i, pos: (i, 0))
    return pl.pallas_call(
        _combine_body,
        grid_spec=pltpu.PrefetchScalarGridSpec(
            num_scalar_prefetch=1,
            grid=(N // tm,),
            in_specs=[pl.BlockSpec((tm, D), lambda i, pos: (i, 0)), g_spec, pl.BlockSpec(memory_space=pl.ANY)],
            out_specs=pl.BlockSpec((tm, D), lambda i, pos: (i, 0)),
            scratch_shapes=[pltpu.VMEM((2, TOPK_EXPERTS, tm, D), jnp.float32), pltpu.SemaphoreType.DMA((2,))],
        ),
        out_shape=jax.ShapeDtypeStruct((N, D), jnp.float32),
        compiler_params=pltpu.CompilerParams(
            dimension_semantics=("arbitrary",),
            vmem_limit_bytes=VMEM_LIMIT_BYTES),
        name="moe_combine",
    )(pos, x, g_arr, y_sorted)


def _dense_block(x, mod, pos, mod_rows, layout, attend, conv_fn, lw):
    n_seq, seq_len = layout
    sh1, sc1, g1, sh2, sc2, g2 = jnp.split(mod, 6, axis=-1)
    h = _norm1(x, lw['norm1_w'], sc1, sh1, mod_rows)
    cos, sin = _rope_tables(pos, HEAD_DIM // 2, HEAD_DIM)
    q_t, k, k_t, v, v_b = _qkv_proj(h, lw['w_head'], lw['q_norm_w'], lw['k_norm_w'], cos, sin, n_seq, seq_len)
    cos_i, sin_i = _rope_tables(pos, IDX_DIM // 2, 2 * IDX_DIM)
    qi_t, ki, ki_b, wi = _idx_proj(h, lw['w_head'], cos_i, sin_i, n_seq, seq_len)
    attn = attend(q_t, k_t, k, v, v_b, qi_t, ki, ki_b, wi)
    conv, new_state = conv_fn(h)
    merged = _merge_proj(h, attn, conv, lw['w_rest'], lw['w_branch_attn'], lw['w_branch_conv'])
    x1, h2, logits = _out_proj(merged, x, g1, sc2, sh2, lw['norm2_w'], lw['w_out_b'], lw['w_router'], mod_rows)
    return x1, h2, logits, g2, k, v, ki, new_state


def kernel(x_prompt, x_sample, c_prompt, c_sample, cache_k, cache_v, cache_kidx, state_conv, page_table, norm1_w, norm2_w, w_ada, b_ada, w_in, q_norm_w, k_norm_w, conv_w, w_branch_attn, w_branch_conv, w_out, w_group_router, b_group_router, w_expert_router, b_expert_router, w_e_gate, w_e_up, w_e_down):
    B, T, _ = x_prompt.shape
    Bd, S, _ = x_sample.shape
    depth = norm1_w.shape[0]
    past = page_table.shape[1] * PAGE_SIZE
    n_p, n_s = B * T, Bd * S
    pos_prompt = jnp.arange(T)
    pos_sample = jnp.tile(past + jnp.arange(S), Bd)
    yp, ys = x_prompt, x_sample
    outs = [[] for _ in range(8)]
    c_all = jnp.concatenate([c_prompt, c_sample], axis=0)
    n_c = c_all.shape[0]
    c_pad = jnp.pad(jax.nn.silu(c_all), ((0, (-n_c) % 8), (0, 0)))
    bf = jnp.bfloat16
    for l in range(depth):
        w_router = jnp.pad(jnp.concatenate([w_group_router[l], w_expert_router[l].reshape(D_MODEL, -1)], axis=1),
                           ((0, 0), (0, 128 - N_GROUPS * (1 + EXPERTS_PER_GROUP))))
        w_head, w_rest = _split_w_in(w_in[l])
        lw = {
            'norm1_w': norm1_w[l], 'norm2_w': norm2_w[l], 'w_head': w_head, 'w_rest': w_rest,
            'q_norm_w': q_norm_w[l], 'k_norm_w': k_norm_w[l],
            'w_branch_attn': w_branch_attn[l], 'w_branch_conv': w_branch_conv[l],
            'w_out_b': w_out[l].astype(bf), 'w_router': w_router,
        }
        mod = _mm(c_pad, w_ada[l], tn=1024)[:n_c] + b_ada[l]

        def prompt_attend(q_t, k_t, k, v, v_b, qi_t, ki, ki_b, wi):
            out = _prompt_attention(q_t, k_t, v_b.reshape(B, T, ATTN_DIM), qi_t, ki_b.reshape(B, T, IDX_DIM),
                                    wi.reshape(B, T, N_IDX_HEADS))
            return out.transpose(0, 3, 1, 2).reshape(n_p, ATTN_DIM)

        def prompt_conv(h):
            zero_state = jnp.zeros((B, CONV_WIDTH - 1, CONV_DIM), jnp.float32)
            return _conv_proj(h, w_rest, conv_w[l], zero_state, B, T)

        def sample_attend(q_t, k_t, k, v, v_b, qi_t, ki, ki_b, wi):
            def per_token(a):
                return a[0].transpose(1, 0, 2).reshape(Bd, S, a.shape[1], a.shape[3])
            out = _sample_attention(per_token(q_t), k.reshape(Bd, S, N_HEADS, HEAD_DIM),
                                    v.reshape(Bd, S, N_HEADS, HEAD_DIM), per_token(qi_t),
                                    ki.reshape(Bd, S, IDX_DIM), wi.reshape(Bd, S, N_IDX_HEADS),
                                    layer=l, cache_k=cache_k, cache_v=cache_v, cache_kidx=cache_kidx,
                                    page_table=page_table)
            return out.reshape(n_s, ATTN_DIM).astype(bf)

        def sample_conv(h):
            xbc = _mm(h, w_rest[:, :REST_GA]).reshape(Bd, S, 3 * CONV_DIM)
            xc, bg, cg = jnp.split(xbc, 3, axis=-1)
            conv, state = _short_conv(xc, bg, cg, conv_w[l], state_conv[l])
            return conv.reshape(n_s, CONV_DIM).astype(bf), state

        xp, hp, lgp, gp, kp, vp, kip, cp = _dense_block(
            yp.reshape(n_p, D_MODEL), mod[:B], pos_prompt, T, (B, T), prompt_attend, prompt_conv, lw)
        xs, hs, lgs, gs, ksm, vsm, kis, cs = _dense_block(
            ys.reshape(n_s, D_MODEL), mod[B:], pos_sample, S, (1, n_s), sample_attend, sample_conv, lw)
        for lst, val in zip(outs, (kp.reshape(B, T, N_HEADS, HEAD_DIM), vp.reshape(B, T, N_HEADS, HEAD_DIM),
                                   kip.reshape(B, T, IDX_DIM), cp,
                                   ksm.reshape(Bd, S, N_HEADS, HEAD_DIM), vsm.reshape(Bd, S, N_HEADS, HEAD_DIM),
                                   kis.reshape(Bd, S, IDX_DIM), cs)):
            lst.append(val)
        h_all = jnp.concatenate([hp, hs], axis=0)
        eid, wgt = _route(jnp.concatenate([lgp, lgs], axis=0), b_group_router[l], b_expert_router[l])
        n_assign = eid.size
        n_tiles = (n_assign + N_GROUPS * EXPERTS_PER_GROUP * (MOE_TM - 1)) // MOE_TM
        plan = _moe_plan(eid, wgt, n_tiles)
        y_sorted = _moe_experts(h_all, plan, w_e_gate, w_e_up, w_e_down, l, n_tiles)
        pos = plan[4]
        yp = _moe_combine(xp, gp, pos[:n_p * TOPK_EXPERTS], y_sorted, T).reshape(B, T, D_MODEL)
        ys = _moe_combine(xs, gs, pos[n_p * TOPK_EXPERTS:], y_sorted, S).reshape(Bd, S, D_MODEL)
    return (yp, ys) + tuple(jnp.stack(o) for o in outs)
```

```python
import functools

import jax
import jax.numpy as jnp
import numpy as np
from jax import lax
from jax.experimental import pallas as pl
from jax.experimental.pallas import tpu as pltpu

D_MODEL = 2048
PAGE_SIZE = 128
N_HEADS = 8
HEAD_DIM = 128
ATTN_DIM = N_HEADS * HEAD_DIM
N_IDX_HEADS = 8
IDX_DIM = 64
TOPK_MAX = 256
CONV_DIM = D_MODEL // 2
CONV_WIDTH = 3
N_GROUPS = 4
EXPERTS_PER_GROUP = 8
TOPK_EXPERTS = 2
EXPERT_FF = 512
ROPE_THETA = 10000.0
EPS = 1e-6

VMEM_LIMIT_BYTES = 56 * 1024 * 1024


def _mm_body(a_ref, b_ref, o_ref):
    a = a_ref[...].astype(jnp.bfloat16)
    b = b_ref[...].astype(jnp.bfloat16)
    o_ref[...] = jnp.dot(a, b, preferred_element_type=jnp.float32).astype(o_ref.dtype)


def _mm(a, b, *, tm=512, tn=512, out_dtype=jnp.float32):
    M, K = a.shape
    _, N = b.shape
    tm = min(tm, M)
    tn = min(tn, N)
    return pl.pallas_call(
        _mm_body,
        grid=(pl.cdiv(M, tm), pl.cdiv(N, tn)),
        in_specs=[pl.BlockSpec((tm, K), lambda i, j: (i, 0)),
                  pl.BlockSpec((K, tn), lambda i, j: (0, j))],
        out_specs=pl.BlockSpec((tm, tn), lambda i, j: (i, j)),
        out_shape=jax.ShapeDtypeStruct((M, N), out_dtype),
        compiler_params=pltpu.CompilerParams(
            dimension_semantics=("arbitrary", "arbitrary"),
            vmem_limit_bytes=VMEM_LIMIT_BYTES),
        name="mm",
    )(a, b)


def _hi_lo(x):
    hi = x.astype(jnp.bfloat16)
    lo = (x - hi.astype(jnp.float32)).astype(jnp.bfloat16)
    return hi, lo


ATTN_TQ = 256
ATTN_SEG = 1024
ATTN_SUB = 256
ATTN_HEADS_PER_STEP = 1
MASK_NEG = -1e30
KEY_NEG_INF = -2139095041
INT32_MIN = -2 ** 31


def _slab_sum(x):
    parts = [x[r * 8:(r + 1) * 8] for r in range(x.shape[0] // 8)]
    while len(parts) > 1:
        parts = [parts[a] + parts[a + 1] for a in range(0, len(parts), 2)]
    return parts[0]


def _prompt_attn_body(qi_ref, ki_ref, wit_ref, q_ref, k_ref, vt_ref, o_ref, key_ref, bias_ref, tie_ref,
                      *, topk, idx_bits):
    tq = ATTN_TQ
    i = pl.program_id(1)
    q0 = i * tq
    nseg = (q0 + tq + ATTN_SEG - 1) // ATTN_SEG
    ntile = nseg * (ATTN_SEG // ATTN_SUB)
    qpos = q0 + lax.broadcasted_iota(jnp.int32, (1, tq), 1)
    nt_dims = (((1,), (1,)), ((), ()))

    def score_tile(t, c):
        r0 = pl.multiple_of(t * ATTN_SUB, ATTN_SUB)
        ki = ki_ref[pl.ds(r0, ATTN_SUB), :]
        acc = jnp.zeros((ATTN_SUB, tq), jnp.float32)
        for h in range(N_IDX_HEADS):
            s = lax.dot_general(ki, qi_ref[h], nt_dims, preferred_element_type=jnp.float32)
            acc = acc + jnp.maximum(s, 0.0) * wit_ref[h:h + 1, :]
        kpos = r0 + lax.broadcasted_iota(jnp.int32, (ATTN_SUB, 1), 0)
        bits = lax.bitcast_convert_type(acc, jnp.int32)
        skey = jnp.where(bits < 0, bits ^ jnp.int32(0x7FFFFFFF), bits)
        skey = jnp.where(acc == 0.0, 0, skey)
        key_ref[pl.ds(r0, ATTN_SUB), :] = jnp.where(kpos <= qpos, skey, KEY_NEG_INF)
        return c

    lax.fori_loop(0, ntile, score_tile, 0)

    def count(pred):
        def tile(t, cnt):
            r0 = pl.multiple_of(t * ATTN_SUB, ATTN_SUB)
            kk = key_ref[pl.ds(r0, ATTN_SUB), :]
            kpos = r0 + lax.broadcasted_iota(jnp.int32, (ATTN_SUB, 1), 0)
            return cnt + _slab_sum(pred(kk, kpos).astype(jnp.int32))
        cnt = lax.fori_loop(0, ntile, tile, jnp.zeros((8, tq), jnp.int32))
        return jnp.sum(cnt, axis=0, keepdims=True)

    def bit_pass(p, t_u):
        cand_u = t_u | lax.shift_left(jnp.int32(1), 31 - p)
        cand_s = cand_u ^ jnp.int32(INT32_MIN)
        c = count(lambda kk, kpos: kk >= cand_s)
        return jnp.where(c >= topk, cand_u, t_u)

    t_u = lax.fori_loop(0, 32, bit_pass, jnp.zeros((1, tq), jnp.int32))
    t_s = t_u ^ jnp.int32(INT32_MIN)

    need = topk - count(lambda kk, kpos: kk > t_s)
    surplus = (count(lambda kk, kpos: kk >= t_s) > topk) & (t_s > KEY_NEG_INF)
    tie_ref[...] = jnp.full((1, tq), 2 ** idx_bits, jnp.int32)

    @pl.when(jnp.max(surplus.astype(jnp.int32)) > 0)
    def _():
        def idx_pass(p, lo):
            cand = lo | lax.shift_left(jnp.int32(1), idx_bits - 1 - p)
            c = count(lambda kk, kpos: (kk == t_s) & (kpos < cand))
            return jnp.where(c < need, cand, lo)

        tie_ref[...] = lax.fori_loop(0, idx_bits, idx_pass, jnp.zeros((1, tq), jnp.int32))

    last_tie = tie_ref[...]

    def bias_tile(t, c):
        r0 = pl.multiple_of(t * ATTN_SUB, ATTN_SUB)
        kk = key_ref[pl.ds(r0, ATTN_SUB), :]
        kpos = r0 + lax.broadcasted_iota(jnp.int32, (ATTN_SUB, 1), 0)
        sel = (kk > t_s) | ((kk == t_s) & (kpos <= last_tie))
        sel = sel & (kk > KEY_NEG_INF)
        bias_ref[pl.ds(r0, ATTN_SUB), :] = jnp.where(sel, 0.0, MASK_NEG)
        return c

    lax.fori_loop(0, ntile, bias_tile, 0)

    scale = HEAD_DIM ** -0.5

    def head_group_body(hg, c):
        heads = [hg * ATTN_HEADS_PER_STEP + a for a in range(ATTN_HEADS_PER_STEP)]

        def seg_body(s, carry):
            r0 = pl.multiple_of(s * ATTN_SEG, ATTN_SEG)
            bias = bias_ref[pl.ds(r0, ATTN_SEG), :]
            out = []
            for h, (m, l, acc) in zip(heads, carry):
                kh = k_ref[h, pl.ds(r0, ATTN_SEG), :]
                logits = lax.dot_general(kh, q_ref[h], nt_dims, preferred_element_type=jnp.float32) * scale
                logits = logits + bias
                m_new = jnp.maximum(m, jnp.max(logits, axis=0, keepdims=True))
                alpha = jnp.exp(m - m_new)
                p = jnp.exp(logits - m_new)
                l = alpha * l + jnp.sum(p, axis=0, keepdims=True)
                acc = alpha * acc + jnp.dot(vt_ref[h, s], p.astype(jnp.bfloat16),
                                            preferred_element_type=jnp.float32)
                out.append((m_new, l, acc))
            return tuple(out)

        init = tuple((jnp.full((1, tq), MASK_NEG, jnp.float32), jnp.zeros((1, tq), jnp.float32),
                      jnp.zeros((HEAD_DIM, tq), jnp.float32)) for _ in heads)
        for h, (_, l, acc) in zip(heads, lax.fori_loop(0, nseg, seg_body, init)):
            o_ref[h] = (acc / l).astype(o_ref.dtype)
        return c

    lax.fori_loop(0, N_HEADS // ATTN_HEADS_PER_STEP, head_group_body, 0)


def _prompt_attention(q_t, k_t, v_b, qi_t, ki_b, wi):
    B, _, T, _ = q_t.shape
    topk = min(TOPK_MAX, T // 4)
    nseg = T // ATTN_SEG
    bf = jnp.bfloat16
    vt = v_b.reshape(B, nseg, ATTN_SEG, N_HEADS, HEAD_DIM).transpose(0, 3, 1, 4, 2)
    wit = wi.transpose(0, 2, 1)
    body = functools.partial(_prompt_attn_body, topk=topk, idx_bits=int(np.ceil(np.log2(T))))
    return pl.pallas_call(
        body,
        grid=(B, T // ATTN_TQ),
        in_specs=[
            pl.BlockSpec((None, N_IDX_HEADS, ATTN_TQ, IDX_DIM), lambda b, i: (b, 0, i, 0)),
            pl.BlockSpec((None, T, IDX_DIM), lambda b, i: (b, 0, 0)),
            pl.BlockSpec((None, N_IDX_HEADS, ATTN_TQ), lambda b, i: (b, 0, i)),
            pl.BlockSpec((None, N_HEADS, ATTN_TQ, HEAD_DIM), lambda b, i: (b, 0, i, 0)),
            pl.BlockSpec((None, N_HEADS, T, HEAD_DIM), lambda b, i: (b, 0, 0, 0)),
            pl.BlockSpec((None, N_HEADS, nseg, HEAD_DIM, ATTN_SEG), lambda b, i: (b, 0, 0, 0, 0)),
        ],
        out_specs=pl.BlockSpec((None, N_HEADS, HEAD_DIM, ATTN_TQ), lambda b, i: (b, 0, 0, i)),
        out_shape=jax.ShapeDtypeStruct((B, N_HEADS, HEAD_DIM, T), bf),
        scratch_shapes=[pltpu.VMEM((T, ATTN_TQ), jnp.int32), pltpu.VMEM((T, ATTN_TQ), jnp.float32),
                        pltpu.VMEM((1, ATTN_TQ), jnp.int32)],
        compiler_params=pltpu.CompilerParams(
            dimension_semantics=("arbitrary", "arbitrary"),
            vmem_limit_bytes=VMEM_LIMIT_BYTES),
        name="prompt_attn",
    )(qi_t, ki_b, wit, q_t, k_t, vt)


SAMPLE_IDX_PAGES = 8
SAMPLE_SEL_NB = 2
SAMPLE_KV_PAGES = 4
SAMPLE_NB = 4


def _score_key(score):
    bits = lax.bitcast_convert_type(score, jnp.int32)
    key = jnp.where(bits < 0, bits ^ jnp.int32(0x7FFFFFFF), bits)
    return jnp.where(score == 0.0, 0, key)


def _sample_select_body(pt_ref, qi_ref, w_ref, kin_ref, exp_ref, *rest, topk, n_pages, idx_bits):
    del pt_ref
    nb = SAMPLE_SEL_NB
    kid_refs = rest[:nb * SAMPLE_IDX_PAGES]
    sel_ref, key_scr, tie_scr = rest[nb * SAMPLE_IDX_PAGES:]
    g = pl.program_id(1)
    n_q = key_scr.shape[2]
    past = n_pages * PAGE_SIZE
    qpos = past + lax.broadcasted_iota(jnp.int32, (n_q, 1), 0)
    lane = lax.broadcasted_iota(jnp.int32, (1, PAGE_SIZE), 1)

    def page_keys(a, ki_page, page):
        s = lax.dot_general(qi_ref[a], ki_page.astype(jnp.bfloat16), (((1,), (1,)), ((), ())),
                            preferred_element_type=jnp.float32)
        score = _slab_sum_n(jnp.maximum(s, 0.0) * w_ref[a], n_q)
        kpos = page * PAGE_SIZE + lane
        return jnp.where(kpos <= qpos, _score_key(score), KEY_NEG_INF)

    for a in range(nb):
        for j in range(SAMPLE_IDX_PAGES):
            page = g * SAMPLE_IDX_PAGES + j
            key_scr[a, page] = page_keys(a, kid_refs[a * SAMPLE_IDX_PAGES + j][...], page)

    @pl.when(g == 0)
    def _():
        for a in range(nb):
            key_scr[a, n_pages] = page_keys(a, kin_ref[a], n_pages)

    @pl.when(g == pl.num_programs(1) - 1)
    def _():
        kks = [key_scr[a] for a in range(nb)]
        kpos = (lax.broadcasted_iota(jnp.int32, kks[0].shape, 0) * PAGE_SIZE
                + lax.broadcasted_iota(jnp.int32, kks[0].shape, 2))

        def count(m):
            return jnp.sum(jnp.sum(m.astype(jnp.int32), axis=0), axis=1, keepdims=True)

        def bit_pass(p, t_us):
            bit = lax.shift_left(jnp.int32(1), 31 - p)
            out = []
            for kk, t_u in zip(kks, t_us):
                cand_u = t_u | bit
                cand_s = cand_u ^ jnp.int32(INT32_MIN)
                out.append(jnp.where(count(kk >= cand_s[None]) >= topk, cand_u, t_u))
            return tuple(out)

        t_us = lax.fori_loop(0, 32, bit_pass, tuple(jnp.zeros((n_q, 1), jnp.int32) for _ in range(nb)))
        for a, (kk, t_u) in enumerate(zip(kks, t_us)):
            t_s = (t_u ^ jnp.int32(INT32_MIN))[None]
            need = topk - count(kk > t_s)
            surplus = (count(kk >= t_s) > topk) & (t_s[0] > KEY_NEG_INF)
            tie_scr[a] = jnp.full((n_q, 1), 2 ** idx_bits, jnp.int32)

            @pl.when(jnp.max(surplus.astype(jnp.int32)) > 0)
            def _(a=a, kk=kk, t_s=t_s, need=need):
                def idx_pass(p, lo):
                    cand = lo | lax.shift_left(jnp.int32(1), idx_bits - 1 - p)
                    return jnp.where(count((kk == t_s) & (kpos < cand[None])) < need, cand, lo)

                tie_scr[a] = lax.fori_loop(0, idx_bits, idx_pass, jnp.zeros((n_q, 1), jnp.int32))

            last_tie = tie_scr[a]
            sel = (kk > t_s) | ((kk == t_s) & (kpos <= last_tie[None]))
            sel = sel & (kk > KEY_NEG_INF)
            flat = sel.astype(jnp.float32).reshape((n_pages + 1) * n_q, PAGE_SIZE).astype(jnp.bfloat16)
            cols = jnp.dot(flat, exp_ref[...], preferred_element_type=jnp.float32)
            sel_ref[a] = ((cols - 1.0) * (-MASK_NEG)).reshape(sel_ref.shape[1:])


def _slab_sum_n(x, n):
    parts = [x[r * n:(r + 1) * n] for r in range(x.shape[0] // n)]
    while len(parts) > 1:
        parts = [parts[a] + parts[a + 1] for a in range(0, len(parts), 2)]
    return parts[0]


def _sample_attn_body(pt_ref, q_ref, sel_ref, seln_ref, hm_ref, kn_ref, vn_ref, *rest):
    del pt_ref
    n_pg = SAMPLE_NB * SAMPLE_KV_PAGES
    k_refs = rest[:n_pg]
    v_refs = rest[n_pg:2 * n_pg]
    o_ref, m_scr, l_scr, acc_scr = rest[2 * n_pg:]
    g = pl.program_id(1)
    n_q = seln_ref.shape[2]
    bf = jnp.bfloat16
    scale = HEAD_DIM ** -0.5

    @pl.when(g == 0)
    def _():
        m_scr[...] = jnp.full(m_scr.shape, MASK_NEG, jnp.float32)
        l_scr[...] = jnp.zeros(l_scr.shape, jnp.float32)
        acc_scr[...] = jnp.zeros(acc_scr.shape, jnp.float32)

    def pages_update(nb, pages):
        qall = q_ref[nb]
        logits = []
        for k_page, _, sel_tile, ncol in pages:
            kp = k_page.reshape(-1, HEAD_DIM).astype(bf)
            bias = jnp.concatenate([sel_tile[:, :ncol]] * N_HEADS, axis=0) + hm_ref[:, :ncol]
            logits.append(lax.dot_general(qall, kp, (((1,), (1,)), ((), ())),
                                          preferred_element_type=jnp.float32) * scale + bias)
        m = m_scr[nb]
        m_new = m
        for lg in logits:
            m_new = jnp.maximum(m_new, jnp.max(lg, axis=1, keepdims=True))
        alpha = jnp.exp(m - m_new)
        l_new = alpha * l_scr[nb]
        acc = alpha * acc_scr[nb]
        for lg, (_, v_page, _, _) in zip(logits, pages):
            p = jnp.exp(lg - m_new)
            l_new = l_new + jnp.sum(p, axis=1, keepdims=True)
            acc = acc + jnp.dot(p.astype(bf), v_page.reshape(-1, HEAD_DIM).astype(bf),
                                preferred_element_type=jnp.float32)
        l_scr[nb] = l_new
        acc_scr[nb] = acc
        m_scr[nb] = m_new

    @pl.when(g == 0)
    def _():
        for nb in range(SAMPLE_NB):
            pages_update(nb, [(kn_ref[nb], vn_ref[nb], seln_ref[nb, 0], n_q * N_HEADS)])

    for nb in range(SAMPLE_NB):
        pages_update(nb, [(k_refs[nb * SAMPLE_KV_PAGES + j][...], v_refs[nb * SAMPLE_KV_PAGES + j][...],
                           sel_ref[nb, j], PAGE_SIZE * N_HEADS) for j in range(SAMPLE_KV_PAGES)])

    @pl.when(g == pl.num_programs(1) - 1)
    def _():
        o_ref[...] = acc_scr[...] / l_scr[...]


def _sample_attention(q, k, v, qi, ki, wi, *, layer, cache_k, cache_v, cache_kidx, page_table):
    Bd, S = q.shape[:2]
    n_pages = page_table.shape[1]
    past = n_pages * PAGE_SIZE
    topk = min(TOPK_MAX, (past + S) // 4)
    bf = jnp.bfloat16
    rows = N_HEADS * S
    qi_r = qi.astype(bf).transpose(0, 2, 1, 3).reshape(Bd, N_IDX_HEADS * S, IDX_DIM)
    w_r = jnp.broadcast_to(wi.transpose(0, 2, 1).reshape(Bd, N_IDX_HEADS * S, 1),
                           (Bd, N_IDX_HEADS * S, PAGE_SIZE))
    ki_new = jnp.pad(ki, ((0, 0), (0, PAGE_SIZE - S), (0, 0)))
    idx_bits = int(np.ceil(np.log2(past + PAGE_SIZE)))
    ncol = PAGE_SIZE * N_HEADS
    col = np.arange(ncol)
    expand = jnp.asarray(col[None, :] // N_HEADS == np.arange(PAGE_SIZE)[:, None], bf)

    sb = SAMPLE_SEL_NB
    assert Bd % sb == 0

    def kid_spec(a, j):
        return pl.BlockSpec((None, None, PAGE_SIZE, IDX_DIM),
                            lambda b, g, pt: (layer, pt[b * sb + a, g * SAMPLE_IDX_PAGES + j], 0, 0))

    sel = pl.pallas_call(
        functools.partial(_sample_select_body, topk=topk, n_pages=n_pages, idx_bits=idx_bits),
        grid_spec=pltpu.PrefetchScalarGridSpec(
            num_scalar_prefetch=1,
            grid=(Bd // sb, n_pages // SAMPLE_IDX_PAGES),
            in_specs=[
                pl.BlockSpec((sb, N_IDX_HEADS * S, IDX_DIM), lambda b, g, pt: (b, 0, 0)),
                pl.BlockSpec((sb, N_IDX_HEADS * S, PAGE_SIZE), lambda b, g, pt: (b, 0, 0)),
                pl.BlockSpec((sb, PAGE_SIZE, IDX_DIM), lambda b, g, pt: (b, 0, 0)),
                pl.BlockSpec((PAGE_SIZE, ncol), lambda b, g, pt: (0, 0)),
            ] + [kid_spec(a, j) for a in range(sb) for j in range(SAMPLE_IDX_PAGES)],
            out_specs=pl.BlockSpec((sb, n_pages + 1, S, ncol), lambda b, g, pt: (b, 0, 0, 0)),
            scratch_shapes=[pltpu.VMEM((sb, n_pages + 1, S, PAGE_SIZE), jnp.int32),
                            pltpu.VMEM((sb, S, 1), jnp.int32)],
        ),
        out_shape=jax.ShapeDtypeStruct((Bd, n_pages + 1, S, ncol), jnp.float32),
        compiler_params=pltpu.CompilerParams(
            dimension_semantics=("arbitrary", "arbitrary"),
            vmem_limit_bytes=VMEM_LIMIT_BYTES),
        name="sample_select",
    )(page_table, qi_r, w_r, ki_new, expand, *([cache_kidx] * (sb * SAMPLE_IDX_PAGES)))

    q_r = q.astype(bf).transpose(0, 2, 1, 3).reshape(Bd, rows, HEAD_DIM)
    head_mask = jnp.asarray(np.where(col[None, :] % N_HEADS == np.arange(rows)[:, None] // S, 0.0, MASK_NEG),
                            jnp.float32)

    nb = SAMPLE_NB
    assert Bd % nb == 0

    def kv_spec(a, j):
        return pl.BlockSpec((None, None, PAGE_SIZE, N_HEADS, HEAD_DIM),
                            lambda b, g, pt: (layer, pt[b * nb + a, g * SAMPLE_KV_PAGES + j], 0, 0, 0))

    kv_specs = [kv_spec(a, j) for a in range(nb) for j in range(SAMPLE_KV_PAGES)]
    out = pl.pallas_call(
        _sample_attn_body,
        grid_spec=pltpu.PrefetchScalarGridSpec(
            num_scalar_prefetch=1,
            grid=(Bd // nb, n_pages // SAMPLE_KV_PAGES),
            in_specs=[
                pl.BlockSpec((nb, rows, HEAD_DIM), lambda b, g, pt: (b, 0, 0)),
                pl.BlockSpec((nb, SAMPLE_KV_PAGES, S, ncol), lambda b, g, pt: (b, g, 0, 0)),
                pl.BlockSpec((nb, 1, S, ncol), lambda b, g, pt: (b, n_pages, 0, 0)),
                pl.BlockSpec((rows, ncol), lambda b, g, pt: (0, 0)),
                pl.BlockSpec((nb, S, N_HEADS, HEAD_DIM), lambda b, g, pt: (b, 0, 0, 0)),
                pl.BlockSpec((nb, S, N_HEADS, HEAD_DIM), lambda b, g, pt: (b, 0, 0, 0)),
            ] + kv_specs * 2,
            out_specs=pl.BlockSpec((nb, rows, HEAD_DIM), lambda b, g, pt: (b, 0, 0)),
            scratch_shapes=[pltpu.VMEM((nb, rows, 1), jnp.float32), pltpu.VMEM((nb, rows, 1), jnp.float32),
                            pltpu.VMEM((nb, rows, HEAD_DIM), jnp.float32)],
        ),
        out_shape=jax.ShapeDtypeStruct((Bd, rows, HEAD_DIM), jnp.float32),
        compiler_params=pltpu.CompilerParams(
            dimension_semantics=("arbitrary", "arbitrary"),
            vmem_limit_bytes=VMEM_LIMIT_BYTES),
        name="sample_attn",
    )(page_table, q_r, sel, sel, head_mask, k, v,
      *([cache_k] * (nb * SAMPLE_KV_PAGES)), *([cache_v] * (nb * SAMPLE_KV_PAGES)))
    return out.reshape(Bd, N_HEADS, S, HEAD_DIM).transpose(0, 2, 1, 3)


W_GROUP = 1024
COL_IDX = 3 * ATTN_DIM
IDX_COLS = N_IDX_HEADS * IDX_DIM + IDX_DIM + N_IDX_HEADS
REST_XC, REST_BG, REST_CG = 0, CONV_DIM, 2 * CONV_DIM
REST_GA, REST_GC = 3 * CONV_DIM, 3 * CONV_DIM + D_MODEL
PROJ_TM = 1024
OUT_TM = 256


def _split_w_in(w_in):
    bf = jnp.bfloat16
    return w_in[:, :COL_IDX + W_GROUP].astype(bf), w_in[:, COL_IDX + IDX_COLS:].astype(bf)


def _rope_tables(pos, half, width):
    inv = 1.0 / jnp.power(ROPE_THETA, jnp.arange(half, dtype=jnp.float32) / half)
    ang = pos.astype(jnp.float32)[:, None] * inv[None, :]
    cos, sin = jnp.cos(ang), jnp.sin(ang)
    reps = width // (2 * half)
    return (jnp.tile(jnp.concatenate([cos, cos], axis=1), (1, reps)),
            jnp.tile(jnp.concatenate([-sin, sin], axis=1), (1, reps)))


def _group_param(param, rows_per_group, tm):
    D = param.shape[-1]
    if rows_per_group % tm == 0:
        return param[:, None, :], pl.BlockSpec((None, 1, D), lambda i, *_: (i * tm // rows_per_group, 0, 0))
    return jnp.repeat(param, rows_per_group, axis=0), pl.BlockSpec((tm, D), lambda i, *_: (i, 0))


def _cparams(n_axes):
    return pltpu.CompilerParams(dimension_semantics=("arbitrary",) * n_axes, vmem_limit_bytes=VMEM_LIMIT_BYTES)


def _modnorm(x, w, scale, shift):
    y = x * lax.rsqrt(jnp.mean(x * x, axis=-1, keepdims=True) + EPS) * w
    return y * (1.0 + scale) + shift


def _norm1_body(x_ref, sc_ref, sh_ref, w_ref, o_ref):
    o_ref[...] = _modnorm(x_ref[...], w_ref[...], sc_ref[...], sh_ref[...]).astype(o_ref.dtype)


def _norm1(x, norm_w, sc, sh, rows_per_group):
    N, D = x.shape
    tm = min(512, N)
    sc_arr, sc_spec = _group_param(sc, rows_per_group, tm)
    sh_arr, sh_spec = _group_param(sh, rows_per_group, tm)
    return pl.pallas_call(
        _norm1_body,
        grid=(N // tm,),
        in_specs=[pl.BlockSpec((tm, D), lambda i: (i, 0)), sc_spec, sh_spec,
                  pl.BlockSpec((1, D), lambda i: (0, 0))],
        out_specs=pl.BlockSpec((tm, D), lambda i: (i, 0)),
        out_shape=jax.ShapeDtypeStruct((N, D), jnp.bfloat16),
        compiler_params=_cparams(1),
        name="norm1",
    )(x, sc_arr, sh_arr, norm_w[None, :])


def _qkv_body(h_ref, w_ref, nw_ref, cos_ref, sin_ref, qt_ref, k_ref, kt_ref, v_ref, vb_ref):
    j = pl.program_id(1)
    acc = jnp.dot(h_ref[...], w_ref[...], preferred_element_type=jnp.float32)

    def norm_rope(nw):
        heads = []
        for hd in range(N_HEADS):
            xh = acc[:, hd * HEAD_DIM:(hd + 1) * HEAD_DIM]
            y = xh * lax.rsqrt(jnp.mean(xh * xh, axis=-1, keepdims=True) + EPS) * nw
            heads.append(y * cos_ref[...] + pltpu.roll(y, HEAD_DIM // 2, 1) * sin_ref[...])
        return heads

    @pl.when(j == 0)
    def _():
        for hd, y in enumerate(norm_rope(nw_ref[0:1, :])):
            qt_ref[hd] = y.astype(qt_ref.dtype)

    @pl.when(j == 1)
    def _():
        for hd, y in enumerate(norm_rope(nw_ref[1:2, :])):
            k_ref[:, hd * HEAD_DIM:(hd + 1) * HEAD_DIM] = y
            kt_ref[hd] = y.astype(kt_ref.dtype)

    @pl.when(j == 2)
    def _():
        v_ref[...] = acc
        vb_ref[...] = acc.astype(vb_ref.dtype)


def _qkv_proj(h, w_b, q_norm_w, k_norm_w, cos, sin, n_seq, seq_len):
    N, D = h.shape
    tm = min(PROJ_TM, seq_len)
    tiles = seq_len // tm
    bf = jnp.bfloat16
    nw = jnp.zeros((8, HEAD_DIM), jnp.float32).at[0].set(q_norm_w).at[1].set(k_norm_w)
    head_major = pl.BlockSpec((None, N_HEADS, tm, HEAD_DIM), lambda i, j: (i // tiles, 0, i % tiles, 0))
    rows = pl.BlockSpec((tm, ATTN_DIM), lambda i, j: (i, 0))
    table = pl.BlockSpec((tm, HEAD_DIM), lambda i, j: (i % tiles, 0))
    return pl.pallas_call(
        _qkv_body,
        grid=(N // tm, 3),
        in_specs=[pl.BlockSpec((tm, D), lambda i, j: (i, 0)),
                  pl.BlockSpec((D, W_GROUP), lambda i, j: (0, j)),
                  pl.BlockSpec((8, HEAD_DIM), lambda i, j: (0, 0)), table, table],
        out_specs=[head_major, rows, head_major, rows, rows],
        out_shape=[jax.ShapeDtypeStruct((n_seq, N_HEADS, seq_len, HEAD_DIM), bf),
                   jax.ShapeDtypeStruct((N, ATTN_DIM), jnp.float32),
                   jax.ShapeDtypeStruct((n_seq, N_HEADS, seq_len, HEAD_DIM), bf),
                   jax.ShapeDtypeStruct((N, ATTN_DIM), jnp.float32),
                   jax.ShapeDtypeStruct((N, ATTN_DIM), bf)],
        compiler_params=_cparams(2),
        name="qkv_proj",
    )(h, w_b, nw, cos, sin)


def _idx_body(h_ref, w_ref, cos_ref, sin_ref, qit_ref, ki_ref, kib_ref, wi_ref):
    acc = jnp.dot(h_ref[...], w_ref[...], preferred_element_type=jnp.float32)
    first_half = lax.broadcasted_iota(jnp.int32, (1, 2 * IDX_DIM), 1) % IDX_DIM < IDX_DIM // 2

    def rope_pair(x):
        swapped = jnp.where(first_half, pltpu.roll(x, 2 * IDX_DIM - IDX_DIM // 2, 1),
                            pltpu.roll(x, IDX_DIM // 2, 1))
        return x * cos_ref[...] + swapped * sin_ref[...]

    for pair in range(N_IDX_HEADS // 2):
        y = rope_pair(acc[:, pair * 2 * IDX_DIM:(pair + 1) * 2 * IDX_DIM])
        qit_ref[2 * pair] = y[:, :IDX_DIM].astype(qit_ref.dtype)
        qit_ref[2 * pair + 1] = y[:, IDX_DIM:].astype(qit_ref.dtype)
    tail = acc[:, N_IDX_HEADS * IDX_DIM:N_IDX_HEADS * IDX_DIM + 2 * IDX_DIM]
    ki = rope_pair(tail)[:, :IDX_DIM]
    ki_ref[...] = ki
    kib_ref[...] = ki.astype(kib_ref.dtype)
    wi_ref[...] = tail[:, IDX_DIM:IDX_DIM + N_IDX_HEADS] * ((N_IDX_HEADS ** -0.5) * (IDX_DIM ** -0.5))


def _idx_proj(h, w_b, cos, sin, n_seq, seq_len):
    N, D = h.shape
    tm = min(PROJ_TM, seq_len)
    tiles = seq_len // tm
    bf = jnp.bfloat16
    table = pl.BlockSpec((tm, 2 * IDX_DIM), lambda i: (i % tiles, 0))
    return pl.pallas_call(
        _idx_body,
        grid=(N // tm,),
        in_specs=[pl.BlockSpec((tm, D), lambda i: (i, 0)),
                  pl.BlockSpec((D, W_GROUP), lambda i: (0, COL_IDX // W_GROUP)), table, table],
        out_specs=[pl.BlockSpec((None, N_IDX_HEADS, tm, IDX_DIM), lambda i: (i // tiles, 0, i % tiles, 0)),
                   pl.BlockSpec((tm, IDX_DIM), lambda i: (i, 0)),
                   pl.BlockSpec((tm, IDX_DIM), lambda i: (i, 0)),
                   pl.BlockSpec((tm, N_IDX_HEADS), lambda i: (i, 0))],
        out_shape=[jax.ShapeDtypeStruct((n_seq, N_IDX_HEADS, seq_len, IDX_DIM), bf),
                   jax.ShapeDtypeStruct((N, IDX_DIM), jnp.float32),
                   jax.ShapeDtypeStruct((N, IDX_DIM), bf),
                   jax.ShapeDtypeStruct((N, N_IDX_HEADS), jnp.float32)],
        compiler_params=_cparams(1),
        name="idx_proj",
    )(h, w_b, cos, sin)


CONV_TN = 512


def _conv_body(h_ref, wx_ref, wb_ref, wc_ref, cw_ref, st_ref, y_ref, tail_ref, carry, *, tiles):
    i = pl.program_id(0)
    j = pl.program_id(1)
    h = h_ref[...]
    dot = functools.partial(jnp.dot, preferred_element_type=jnp.float32)
    u = dot(h, wc_ref[...]) * dot(h, wx_ref[...])
    bg = dot(h, wb_ref[...])

    @pl.when(i % tiles == 0)
    def _():
        carry[j] = st_ref[...]

    prev = carry[j]
    row = lax.broadcasted_iota(jnp.int32, (u.shape[0], 1), 0)
    u1 = jnp.where(row == 0, prev[1:2, :], pltpu.roll(u, 1, 0))
    u2 = jnp.where(row == 0, prev[0:1, :], jnp.where(row == 1, prev[1:2, :], pltpu.roll(u, 2, 0)))
    y_ref[...] = (bg * (cw_ref[0:1, :] * u2 + cw_ref[1:2, :] * u1 + cw_ref[2:3, :] * u)).astype(y_ref.dtype)
    last = u[u.shape[0] - (CONV_WIDTH - 1):, :]
    carry[j] = last
    tail_ref[...] = last


def _conv_proj(h, w_b, conv_w, state, n_seq, seq_len):
    N, D = h.shape
    tm = min(PROJ_TM, seq_len)
    tiles = seq_len // tm
    n_ct = CONV_DIM // CONV_TN

    def w_spec(col):
        return pl.BlockSpec((D, CONV_TN), lambda i, j: (0, col // CONV_TN + j))

    conv, tails = pl.pallas_call(
        functools.partial(_conv_body, tiles=tiles),
        grid=(N // tm, n_ct),
        in_specs=[pl.BlockSpec((tm, D), lambda i, j: (i, 0)),
                  w_spec(REST_XC), w_spec(REST_BG), w_spec(REST_CG),
                  pl.BlockSpec((CONV_WIDTH, CONV_TN), lambda i, j: (0, j)),
                  pl.BlockSpec((None, CONV_WIDTH - 1, CONV_TN), lambda i, j: (i // tiles, 0, j))],
        out_specs=[pl.BlockSpec((tm, CONV_TN), lambda i, j: (i, j)),
                   pl.BlockSpec((None, CONV_WIDTH - 1, CONV_TN), lambda i, j: (i, 0, j))],
        out_shape=[jax.ShapeDtypeStruct((N, CONV_DIM), jnp.bfloat16),
                   jax.ShapeDtypeStruct((N // tm, CONV_WIDTH - 1, CONV_DIM), jnp.float32)],
        scratch_shapes=[pltpu.VMEM((n_ct, CONV_WIDTH - 1, CONV_TN), jnp.float32)],
        compiler_params=_cparams(2),
        name="conv_proj",
    )(h, w_b, w_b, w_b, conv_w, state)
    return conv, tails[tiles - 1::tiles]


MERGE_TN = 512


def _merge_body(h_ref, a_ref, c_ref, wga_ref, wgc_ref, wba_ref, wbc_ref, o_ref):
    bf = jnp.bfloat16
    dot = functools.partial(jnp.dot, preferred_element_type=jnp.float32)
    h = h_ref[...]
    a_br = dot(a_ref[...], wba_ref[...].astype(bf))
    c_br = dot(c_ref[...], wbc_ref[...].astype(bf))
    merged = jax.nn.sigmoid(dot(h, wga_ref[...])) * a_br + jax.nn.sigmoid(dot(h, wgc_ref[...])) * c_br
    o_ref[...] = merged.astype(o_ref.dtype)


def _merge_proj(h, attn, conv, w_g, w_branch_attn, w_branch_conv):
    N, D = h.shape
    tm = min(PROJ_TM, N)
    return pl.pallas_call(
        _merge_body,
        grid=(N // tm, D // MERGE_TN),
        in_specs=[pl.BlockSpec((tm, D), lambda i, j: (i, 0)),
                  pl.BlockSpec((tm, ATTN_DIM), lambda i, j: (i, 0)),
                  pl.BlockSpec((tm, CONV_DIM), lambda i, j: (i, 0)),
                  pl.BlockSpec((D, MERGE_TN), lambda i, j: (0, REST_GA // MERGE_TN + j)),
                  pl.BlockSpec((D, MERGE_TN), lambda i, j: (0, REST_GC // MERGE_TN + j)),
                  pl.BlockSpec((ATTN_DIM, MERGE_TN), lambda i, j: (0, j)),
                  pl.BlockSpec((CONV_DIM, MERGE_TN), lambda i, j: (0, j))],
        out_specs=pl.BlockSpec((tm, MERGE_TN), lambda i, j: (i, j)),
        out_shape=jax.ShapeDtypeStruct((N, D), jnp.bfloat16),
        compiler_params=_cparams(2),
        name="merge_proj",
    )(h, attn, conv, w_g, w_g, w_branch_attn, w_branch_conv)


def _out_body(m_ref, x_ref, g1_ref, sc_ref, sh_ref, nw_ref, wo_ref, rh_ref, rl_ref, x1_ref, h2_ref, lg_ref):
    dot = functools.partial(jnp.dot, preferred_element_type=jnp.float32)
    x1 = x_ref[...] + g1_ref[...] * dot(m_ref[...], wo_ref[...])
    x1_ref[...] = x1
    h2 = _modnorm(x1, nw_ref[...], sc_ref[...], sh_ref[...])
    h2_ref[...] = h2
    h_hi, h_lo = _hi_lo(h2)
    lg_ref[...] = dot(h_hi, rh_ref[...]) + (dot(h_lo, rh_ref[...]) + dot(h_hi, rl_ref[...]))


def _out_proj(merged, x, g1, sc2, sh2, norm2_w, w_out_b, w_router, rows_per_group):
    N, D = x.shape
    tm = min(OUT_TM, N)
    g1_arr, g1_spec = _group_param(g1, rows_per_group, tm)
    sc_arr, sc_spec = _group_param(sc2, rows_per_group, tm)
    sh_arr, sh_spec = _group_param(sh2, rows_per_group, tm)
    r_hi, r_lo = _hi_lo(w_router)
    n_r = w_router.shape[1]
    rows = pl.BlockSpec((tm, D), lambda i: (i, 0))
    return pl.pallas_call(
        _out_body,
        grid=(N // tm,),
        in_specs=[rows, rows, g1_spec, sc_spec, sh_spec,
                  pl.BlockSpec((1, D), lambda i: (0, 0)),
                  pl.BlockSpec((D, D), lambda i: (0, 0)),
                  pl.BlockSpec((D, n_r), lambda i: (0, 0)),
                  pl.BlockSpec((D, n_r), lambda i: (0, 0))],
        out_specs=[rows, rows, pl.BlockSpec((tm, n_r), lambda i: (i, 0))],
        out_shape=[jax.ShapeDtypeStruct((N, D), jnp.float32), jax.ShapeDtypeStruct((N, D), jnp.float32),
                   jax.ShapeDtypeStruct((N, n_r), jnp.float32)],
        compiler_params=_cparams(1),
        name="out_proj",
    )(merged, x, g1_arr, sc_arr, sh_arr, norm2_w[None, :], w_out_b, r_hi, r_lo)


def _short_conv(xc, bg, cg, conv_w, buf):
    u = cg * xc
    ext = jnp.concatenate([buf, u], axis=1)
    T = u.shape[1]
    y = sum(conv_w[j] * ext[:, j:j + T] for j in range(CONV_WIDTH))
    return bg * y, ext[:, -(CONV_WIDTH - 1):]


MOE_TM = 256
COMBINE_TM = 128


def _route(r_logits, b_gr, b_er):
    n = r_logits.shape[0]
    g_logits = r_logits[:, :N_GROUPS]
    g_prob = jax.nn.softmax(g_logits, axis=-1)
    g_sel = jnp.argmax(g_logits + b_gr, axis=-1)
    g_w = jnp.take_along_axis(g_prob, g_sel[:, None], axis=1)[:, 0]
    e_logits = r_logits[:, N_GROUPS:N_GROUPS * (1 + EXPERTS_PER_GROUP)].reshape(n, N_GROUPS, EXPERTS_PER_GROUP)
    e_in = jnp.take_along_axis(e_logits, g_sel[:, None, None], axis=1)[:, 0]
    _, e_idx = lax.top_k(e_in + b_er[g_sel], TOPK_EXPERTS)
    e_p = jax.nn.softmax(jnp.take_along_axis(e_in, e_idx, axis=1), axis=-1)
    eid = (g_sel[:, None] * EXPERTS_PER_GROUP + e_idx).astype(jnp.int32)
    return eid, g_w[:, None] * e_p


def _moe_plan(eid, wgt, n_tiles):
    n_exp = N_GROUPS * EXPERTS_PER_GROUP
    a = eid.size
    e_flat = eid.reshape(a)
    order = jnp.argsort(e_flat, stable=True).astype(jnp.int32)
    sorted_pos = jnp.argsort(order).astype(jnp.int32)
    bounds = jnp.searchsorted(e_flat[order], jnp.arange(n_exp + 1, dtype=jnp.int32), side='left').astype(jnp.int32)
    cnt_start, counts = bounds[:-1], bounds[1:] - bounds[:-1]
    padded = (counts + MOE_TM - 1) // MOE_TM * MOE_TM
    pad_end = jnp.cumsum(padded)
    pad_start = pad_end - padded
    pos = sorted_pos + (pad_start - cnt_start)[e_flat]
    tile_row0 = jnp.arange(n_tiles, dtype=jnp.int32) * MOE_TM
    tile_expert = jnp.minimum(jnp.searchsorted(pad_end, tile_row0, side='right'), n_exp - 1).astype(jnp.int32)
    rank = (tile_row0 - pad_start[tile_expert])[:, None] + jnp.arange(MOE_TM, dtype=jnp.int32)[None, :]
    row_valid = (rank < counts[tile_expert][:, None]) & (tile_row0 < pad_end[-1])[:, None]
    assign = order[jnp.clip(cnt_start[tile_expert][:, None] + rank, 0, a - 1)]
    rows = n_tiles * MOE_TM
    row_token = jnp.where(row_valid, assign // TOPK_EXPERTS, 0).reshape(rows)
    row_weight = jnp.where(row_valid, wgt.reshape(a)[assign], 0.0).reshape(rows, 1)
    n_used = (pad_end[-1] // MOE_TM).astype(jnp.int32).reshape(1)
    return tile_expert, n_used, row_token, row_weight, pos.astype(jnp.int32)


def _moe_body(te_ref, nu_ref, tok_ref, x_hbm, rw_ref, wg_ref, wu_ref, wd_ref, y_ref,
              buf, sem, wgb, wub, wdb):
    t = pl.program_id(0)
    n_used = nu_ref[0]
    slot = lax.rem(t, 2)
    bf = jnp.bfloat16

    def row_copy(tile, slot_, r):
        tok = tok_ref[tile * MOE_TM + r]
        return pltpu.make_async_copy(x_hbm.at[pl.ds(tok, 1), :], buf.at[slot_, pl.ds(r, 1), :], sem.at[slot_])

    def gather_start(tile, slot_):
        def body(r, c):
            row_copy(tile, slot_, r).start()
            return c
        lax.fori_loop(0, MOE_TM, body, 0, unroll=8)

    def gather_wait(tile, slot_):
        def body(r, c):
            row_copy(tile, slot_, r).wait()
            return c
        lax.fori_loop(0, MOE_TM, body, 0, unroll=8)

    @pl.when(t == 0)
    def _():
        gather_start(0, 0)

    @pl.when(t + 1 < n_used)
    def _():
        gather_start(t + 1, 1 - slot)

    @pl.when(t < n_used)
    def _():
        expert_changed = jnp.logical_or(t == 0, te_ref[t] != te_ref[jnp.maximum(t - 1, 0)])

        @pl.when(expert_changed)
        def _():
            wgb[...] = wg_ref[...].astype(bf)
            wub[...] = wu_ref[...].astype(bf)
            wdb[...] = wd_ref[...].astype(bf)

        gather_wait(t, slot)
        x = buf[slot].astype(bf)
        g = jnp.dot(x, wgb[...], preferred_element_type=jnp.float32)
        u = jnp.dot(x, wub[...], preferred_element_type=jnp.float32)
        hid = jax.nn.silu(g) * u * rw_ref[...]
        y_ref[...] = jnp.dot(hid.astype(bf), wdb[...], preferred_element_type=jnp.float32)

    @pl.when(t >= n_used)
    def _():
        y_ref[...] = jnp.zeros_like(y_ref)


def _moe_experts(x_all, plan, w_g, w_u, w_d, layer, n_tiles):
    tile_expert, n_used, row_token, row_weight, _ = plan
    D = x_all.shape[1]
    F = w_g.shape[-1]

    def w_spec(k_dim, n_dim):
        return pl.BlockSpec((None, None, None, k_dim, n_dim),
                            lambda t, te, nu, tok: (layer, te[t] // EXPERTS_PER_GROUP, te[t] % EXPERTS_PER_GROUP, 0, 0))

    return pl.pallas_call(
        _moe_body,
        grid_spec=pltpu.PrefetchScalarGridSpec(
            num_scalar_prefetch=3,
            grid=(n_tiles,),
            in_specs=[
                pl.BlockSpec(memory_space=pl.ANY),
                pl.BlockSpec((MOE_TM, 1), lambda t, te, nu, tok: (t, 0)),
                w_spec(D, F), w_spec(D, F), w_spec(F, D),
            ],
            out_specs=pl.BlockSpec((MOE_TM, D), lambda t, te, nu, tok: (t, 0)),
            scratch_shapes=[
                pltpu.VMEM((2, MOE_TM, D), jnp.float32),
                pltpu.SemaphoreType.DMA((2,)),
                pltpu.VMEM((D, F), jnp.bfloat16), pltpu.VMEM((D, F), jnp.bfloat16),
                pltpu.VMEM((F, D), jnp.bfloat16),
            ],
        ),
        out_shape=jax.ShapeDtypeStruct((n_tiles * MOE_TM, D), jnp.float32),
        compiler_params=pltpu.CompilerParams(
            dimension_semantics=("arbitrary",),
            vmem_limit_bytes=VMEM_LIMIT_BYTES),
        name="moe_experts",
    )(tile_expert, n_used, row_token, x_all, row_weight, w_g, w_u, w_d)


def _combine_body(pos_ref, x_ref, g_ref, y_hbm, o_ref, buf, sem):
    i = pl.program_id(0)
    slot = lax.rem(i, 2)

    def row_copy(tile, slot_, r, j):
        row = pos_ref[(tile * COMBINE_TM + r) * TOPK_EXPERTS + j]
        return pltpu.make_async_copy(y_hbm.at[pl.ds(row, 1), :], buf.at[slot_, j, pl.ds(r, 1), :], sem.at[slot_])

    def gather(tile, slot_, start):
        def body(r, c):
            for j in range(TOPK_EXPERTS):
                cp = row_copy(tile, slot_, r, j)
                if start:
                    cp.start()
                else:
                    cp.wait()
            return c
        lax.fori_loop(0, COMBINE_TM, body, 0, unroll=8)

    @pl.when(i == 0)
    def _():
        gather(0, 0, True)

    @pl.when(i + 1 < pl.num_programs(0))
    def _():
        gather(i + 1, 1 - slot, True)

    gather(i, slot, False)
    moe = buf[slot, 0]
    for j in range(1, TOPK_EXPERTS):
        moe = moe + buf[slot, j]
    o_ref[...] = x_ref[...] + g_ref[...] * moe


def _moe_combine(x, gate, pos, y_sorted, rows_per_gate):
    N, D = x.shape
    tm = COMBINE_TM
    assert N % tm == 0 and (rows_per_gate % tm == 0 or tm % rows_per_gate == 0)
    if rows_per_gate >= tm:
        g_arr = gate.reshape(-1, 1, D)
        g_spec = pl.BlockSpec((None, 1, D), lambda i, pos: (i * tm // rows_per_gate, 0, 0))
    else:
        g_arr = jnp.repeat(gate, rows_per_gate, axis=0)
        g_spec = pl.BlockSpec((tm, D), lambda i, pos: (i, 0))
    return pl.pallas_call(
        _combine_body,
        grid_spec=pltpu.PrefetchScalarGridSpec(
            num_scalar_prefetch=1,
            grid=(N // tm,),
            in_specs=[pl.BlockSpec((tm, D), lambda i, pos: (i, 0)), g_spec, pl.BlockSpec(memory_space=pl.ANY)],
            out_specs=pl.BlockSpec((tm, D), lambda i, pos: (i, 0)),
            scratch_shapes=[pltpu.VMEM((2, TOPK_EXPERTS, tm, D), jnp.float32), pltpu.SemaphoreType.DMA((2,))],
        ),
        out_shape=jax.ShapeDtypeStruct((N, D), jnp.float32),
        compiler_params=pltpu.CompilerParams(
            dimension_semantics=("arbitrary",),
            vmem_limit_bytes=VMEM_LIMIT_BYTES),
        name="moe_combine",
    )(pos, x, g_arr, y_sorted)


def _dense_block(x, mod, pos, mod_rows, layout, attend, conv_fn, lw):
    n_seq, seq_len = layout
    sh1, sc1, g1, sh2, sc2, g2 = jnp.split(mod, 6, axis=-1)
    h = _norm1(x, lw['norm1_w'], sc1, sh1, mod_rows)
    cos, sin = _rope_tables(pos, HEAD_DIM // 2, HEAD_DIM)
    q_t, k, k_t, v, v_b = _qkv_proj(h, lw['w_head'], lw['q_norm_w'], lw['k_norm_w'], cos, sin, n_seq, seq_len)
    cos_i, sin_i = _rope_tables(pos, IDX_DIM // 2, 2 * IDX_DIM)
    qi_t, ki, ki_b, wi = _idx_proj(h, lw['w_head'], cos_i, sin_i, n_seq, seq_len)
    attn = attend(q_t, k_t, k, v, v_b, qi_t, ki, ki_b, wi)
    conv, new_state = conv_fn(h)
    merged = _merge_proj(h, attn, conv, lw['w_rest'], lw['w_branch_attn'], lw['w_branch_conv'])
    x1, h2, logits = _out_proj(merged, x, g1, sc2, sh2, lw['norm2_w'], lw['w_out_b'], lw['w_router'], mod_rows)
    return x1, h2, logits, g2, k, v, ki, new_state


def kernel(x_prompt, x_sample, c_prompt, c_sample, cache_k, cache_v, cache_kidx, state_conv, page_table, norm1_w, norm2_w, w_ada, b_ada, w_in, q_norm_w, k_norm_w, conv_w, w_branch_attn, w_branch_conv, w_out, w_group_router, b_group_router, w_expert_router, b_expert_router, w_e_gate, w_e_up, w_e_down):
    B, T, _ = x_prompt.shape
    Bd, S, _ = x_sample.shape
    depth = norm1_w.shape[0]
    past = page_table.shape[1] * PAGE_SIZE
    n_p, n_s = B * T, Bd * S
    pos_prompt = jnp.arange(T)
    pos_sample = jnp.tile(past + jnp.arange(S), Bd)
    yp, ys = x_prompt, x_sample
    outs = [[] for _ in range(8)]
    c_all = jnp.concatenate([c_prompt, c_sample], axis=0)
    n_c = c_all.shape[0]
    c_pad = jnp.pad(jax.nn.silu(c_all), ((0, (-n_c) % 8), (0, 0)))
    bf = jnp.bfloat16
    for l in range(depth):
        w_router = jnp.pad(jnp.concatenate([w_group_router[l], w_expert_router[l].reshape(D_MODEL, -1)], axis=1),
                           ((0, 0), (0, 128 - N_GROUPS * (1 + EXPERTS_PER_GROUP))))
        w_head, w_rest = _split_w_in(w_in[l])
        lw = {
            'norm1_w': norm1_w[l], 'norm2_w': norm2_w[l], 'w_head': w_head, 'w_rest': w_rest,
            'q_norm_w': q_norm_w[l], 'k_norm_w': k_norm_w[l],
            'w_branch_attn': w_branch_attn[l], 'w_branch_conv': w_branch_conv[l],
            'w_out_b': w_out[l].astype(bf), 'w_router': w_router,
        }
        mod = _mm(c_pad, w_ada[l], tn=1024)[:n_c] + b_ada[l]

        def prompt_attend(q_t, k_t, k, v, v_b, qi_t, ki, ki_b, wi):
            out = _prompt_attention(q_t, k_t, v_b.reshape(B, T, ATTN_DIM), qi_t, ki_b.reshape(B, T, IDX_DIM),
                                    wi.reshape(B, T, N_IDX_HEADS))
            return out.transpose(0, 3, 1, 2).reshape(n_p, ATTN_DIM)

        def prompt_conv(h):
            zero_state = jnp.zeros((B, CONV_WIDTH - 1, CONV_DIM), jnp.float32)
            return _conv_proj(h, w_rest, conv_w[l], zero_state, B, T)

        def sample_attend(q_t, k_t, k, v, v_b, qi_t, ki, ki_b, wi):
            def per_token(a):
                return a[0].transpose(1, 0, 2).reshape(Bd, S, a.shape[1], a.shape[3])
            out = _sample_attention(per_token(q_t), k.reshape(Bd, S, N_HEADS, HEAD_DIM),
                                    v.reshape(Bd, S, N_HEADS, HEAD_DIM), per_token(qi_t),
                                    ki.reshape(Bd, S, IDX_DIM), wi.reshape(Bd, S, N_IDX_HEADS),
                                    layer=l, cache_k=cache_k, cache_v=cache_v, cache_kidx=cache_kidx,
                                    page_table=page_table)
            return out.reshape(n_s, ATTN_DIM).astype(bf)

        def sample_conv(h):
            xbc = _mm(h, w_rest[:, :REST_GA]).reshape(Bd, S, 3 * CONV_DIM)
            xc, bg, cg = jnp.split(xbc, 3, axis=-1)
            conv, state = _short_conv(xc, bg, cg, conv_w[l], state_conv[l])
            return conv.reshape(n_s, CONV_DIM).astype(bf), state

        xp, hp, lgp, gp, kp, vp, kip, cp = _dense_block(
            yp.reshape(n_p, D_MODEL), mod[:B], pos_prompt, T, (B, T), prompt_attend, prompt_conv, lw)
        xs, hs, lgs, gs, ksm, vsm, kis, cs = _dense_block(
            ys.reshape(n_s, D_MODEL), mod[B:], pos_sample, S, (1, n_s), sample_attend, sample_conv, lw)
        for lst, val in zip(outs, (kp.reshape(B, T, N_HEADS, HEAD_DIM), vp.reshape(B, T, N_HEADS, HEAD_DIM),
                                   kip.reshape(B, T, IDX_DIM), cp,
                                   ksm.reshape(Bd, S, N_HEADS, HEAD_DIM), vsm.reshape(Bd, S, N_HEADS, HEAD_DIM),
                                   kis.reshape(Bd, S, IDX_DIM), cs)):
            lst.append(val)
        h_all = jnp.concatenate([hp, hs], axis=0)
        eid, wgt = _route(jnp.concatenate([lgp, lgs], axis=0), b_group_router[l], b_expert_router[l])
        n_assign = eid.size
        n_tiles = (n_assign + N_GROUPS * EXPERTS_PER_GROUP * (MOE_TM - 1)) // MOE_TM
        plan = _moe_plan(eid, wgt, n_tiles)
        y_sorted = _moe_experts(h_all, plan, w_e_gate, w_e_up, w_e_down, l, n_tiles)
        pos = plan[4]
        yp = _moe_combine(xp, gp, pos[:n_p * TOPK_EXPERTS], y_sorted, T).reshape(B, T, D_MODEL)
        ys = _moe_combine(xs, gs, pos[n_p * TOPK_EXPERTS:], y_sorted, S).reshape(Bd, S, D_MODEL)
    return (yp, ys) + tuple(jnp.stack(o) for o in outs)
```
